```python
import math
import jax
import jax.numpy as jnp
from jax import lax
import numpy as np

D_MODEL = 1024
BATCH = 8
SEQ = 2048
DEPTH = 2
DEC_BATCH = 128
DEC_SEQ = 1
PAST_LEN = 16384
PAGE_SIZE = 128

N_MIX = 4
N_HEADS = 4
HEAD_DIM = 64
MIX_WIDTH = N_HEADS * HEAD_DIM
RWKV_W_RANK = 64
RWKV_A_RANK = 64
RWKV_G_RANK = 128
RWKV_COLS = 3 * MIX_WIDTH + RWKV_W_RANK + RWKV_A_RANK + RWKV_G_RANK
HGRN_COLS = 4 * MIX_WIDTH
RET_COLS = 4 * MIX_WIDTH
MLSTM_QK = 2 * MIX_WIDTH
MLSTM_COLS = 4 * MIX_WIDTH + 2 * N_HEADS
GATE_COLS = N_MIX * D_MODEL
IN_COLS = RWKV_COLS + HGRN_COLS + RET_COLS + MLSTM_COLS + GATE_COLS
MLSTM_CONV = 4
CHUNK = 64
RET_THETA = 10000.0
PEER_HEADS = 8
PEER_N_KEYS = 128
PEER_N_EXPERTS = PEER_N_KEYS * PEER_N_KEYS
PEER_TOPK = 16
PEER_QDIM = 128
PEER_QHALF = PEER_QDIM // 2
PEER_BLOCK = 256
NORM_EPS = 1e-6
RWKV_GN_EPS = 64e-5
HEAD_NORM_EPS = 1e-5
MASK_NEG = -1e30
LB_FLOOR = 1e-30

kernel_name = 'hybrid_rwkv7_hgrn2_retnet_mlstm_peer_step'


def rms_norm(x, g):
    xf = x.astype(jnp.float32)
    y = xf * lax.rsqrt(jnp.mean(xf * xf, axis=-1, keepdims=True) + NORM_EPS)
    return (y * g.astype(jnp.float32)).astype(x.dtype)


def head_norm(y, g=None, b=None, eps=HEAD_NORM_EPS, center=False):
    if center:
        y = y - jnp.mean(y, axis=-1, keepdims=True)
    y = y * lax.rsqrt(jnp.mean(y * y, axis=-1, keepdims=True) + eps)
    if g is not None:
        y = y * g.astype(jnp.float32).reshape(N_HEADS, HEAD_DIM)
    if b is not None:
        y = y + b.astype(jnp.float32).reshape(N_HEADS, HEAD_DIM)
    return y


def heads(z):
    return z.reshape(z.shape[:-1] + (N_HEADS, -1))


def to_chunks(z, L):
    B, T, H = z.shape[:3]
    z = z.reshape((B, T // L, L, H) + z.shape[3:])
    return jnp.transpose(z, (1, 0, 3, 2) + tuple(range(4, z.ndim)))


def from_chunks(z):
    nC, B, H, L, D = z.shape
    return jnp.transpose(z, (1, 0, 3, 2, 4)).reshape(B, nC * L, H, D)


def causal_mask(L):
    idx = jnp.arange(L)
    return idx[:, None] >= idx[None, :]


def rotary(z, pos):
    half = HEAD_DIM // 2
    inv = RET_THETA ** (-jnp.arange(half, dtype=jnp.float32) / half)
    ang = pos.astype(jnp.float32)[:, None] * inv[None, :]
    cos = jnp.cos(ang)[None, :, None, :]
    sin = jnp.sin(ang)[None, :, None, :]
    z1, z2 = z[..., :half], z[..., half:]
    return jnp.concatenate([z1 * cos - z2 * sin, z1 * sin + z2 * cos], axis=-1)


def rwkv7_mixer(p, shift_prev, S0, mu, w0, w_up, a0, a_up, g_up, k_k, k_a, r_k, ln_g, ln_b):
    B, T, _ = p.shape
    W = MIX_WIDTH
    p_prev = jnp.concatenate([shift_prev[:, None], p[:, :-1]], axis=1)
    xs = p + mu * (p_prev - p)
    r, k, v = xs[..., :W], xs[..., W:2 * W], xs[..., 2 * W:3 * W]
    o = 3 * W
    xw = xs[..., o:o + RWKV_W_RANK]
    o += RWKV_W_RANK
    xa = xs[..., o:o + RWKV_A_RANK]
    o += RWKV_A_RANK
    xg = xs[..., o:o + RWKV_G_RANK]
    w = w0 + jnp.tanh(xw) @ w_up
    log_decay = -jnp.exp(-jax.nn.softplus(-w) - 0.5)
    a = jax.nn.sigmoid(a0 + xa @ a_up)
    g = jax.nn.sigmoid(xg) @ g_up
    kk = heads(k * k_k)
    kk = kk / jnp.maximum(jnp.sqrt(jnp.sum(kk * kk, axis=-1, keepdims=True)), 1e-12)
    k = k * (1.0 + (a - 1.0) * k_a)
    r_h, k_h, v_h, a_h = heads(r), heads(k), heads(v), heads(a)
    wd = jnp.exp(heads(log_decay))

    def step(S, inp):
        r_t, w_t, k_t, v_t, kk_t, a_t = inp
        sa = jnp.einsum('bhvk,bhk->bhv', S, -kk_t)
        S = (S * w_t[:, :, None, :]
             + jnp.einsum('bhv,bhk->bhvk', sa, kk_t * a_t)
             + jnp.einsum('bhv,bhk->bhvk', v_t, k_t))
        return S, jnp.einsum('bhvk,bhk->bhv', S, r_t)

    tm = lambda z: jnp.moveaxis(z, 1, 0)
    S_T, y = lax.scan(step, S0, (tm(r_h), tm(wd), tm(k_h), tm(v_h), tm(kk), tm(a_h)))
    y = jnp.moveaxis(y, 0, 1)
    y = head_norm(y, ln_g, ln_b, eps=RWKV_GN_EPS, center=True)
    y = y + jnp.sum(r_h * k_h * r_k.reshape(N_HEADS, HEAD_DIM), axis=-1, keepdims=True) * v_h
    return y.reshape(B, T, W) * g, p[:, -1], S_T


def gla_chunked(q, k, v, log_f, S0):
    T = q.shape[1]
    L = math.gcd(T, CHUNK)
    causal = causal_mask(L)

    def step(S, inp):
        qc, kc, vc, gc = inp
        b = jnp.cumsum(gc, axis=2)
        o = jnp.einsum('bhtd,bhdv->bhtv', qc * jnp.exp(b), S)
        diff = jnp.where(causal[:, :, None], b[:, :, :, None, :] - b[:, :, None, :, :], MASK_NEG)
        A = jnp.einsum('bhtsd,bhsd->bhts', qc[:, :, :, None, :] * jnp.exp(diff), kc)
        o = o + jnp.einsum('bhts,bhsv->bhtv', A, vc)
        bL = b[:, :, -1:]
        S = S * jnp.exp(bL)[:, :, 0, :, None] + jnp.einsum('bhsd,bhsv->bhdv', kc * jnp.exp(bL - b), vc)
        return S, o

    S, o = lax.scan(step, S0, (to_chunks(q, L), to_chunks(k, L), to_chunks(v, L), to_chunks(log_f, L)))
    return from_chunks(o), S


def hgrn2_mixer(p, S0, lb, norm_g):
    B, T, _ = p.shape
    W = MIX_WIDTH
    q = jax.nn.silu(p[..., :W])
    f_pre = p[..., W:2 * W]
    i_in = p[..., 2 * W:3 * W]
    g = p[..., 3 * W:]
    log_f = jnp.logaddexp(jnp.log(jnp.maximum(lb, LB_FLOOR)),
                          jnp.log1p(-lb) + jax.nn.log_sigmoid(f_pre))
    k = -jnp.expm1(log_f)
    o, S = gla_chunked(heads(q), heads(k), heads(i_in), heads(log_f), S0)
    o = head_norm(o, norm_g).reshape(B, T, W) * jax.nn.silu(g)
    return o, S


def retention_chunked(q, k, v, S0):
    T = q.shape[1]
    L = math.gcd(T, CHUNK)
    lg = jnp.log(1.0 - 2.0 ** (-5.0 - jnp.arange(N_HEADS, dtype=jnp.float32)))
    idx = jnp.arange(L, dtype=jnp.float32)
    rel = idx[:, None] - idx[None, :]
    dmat = jnp.where(rel >= 0, jnp.exp(lg[:, None, None] * jnp.maximum(rel, 0.0)), 0.0)
    d_in = jnp.exp(lg[:, None] * (idx + 1.0))[..., None]
    d_st = jnp.exp(lg[:, None] * (L - 1.0 - idx))[..., None]
    g_L = jnp.exp(lg * L)[:, None, None]

    def step(S, inp):
        qc, kc, vc = inp
        att = jnp.einsum('bhtd,bhsd->bhts', qc, kc) * dmat
        o = jnp.einsum('bhts,bhsv->bhtv', att, vc) + jnp.einsum('bhtd,bhdv->bhtv', qc, S) * d_in
        S = S * g_L + jnp.einsum('bhsd,bhsv->bhdv', kc * d_st, vc)
        return S, o

    S, o = lax.scan(step, S0, (to_chunks(q, L), to_chunks(k, L), to_chunks(v, L)))
    return from_chunks(o), S


def retention_mixer(p, S0, pos0):
    B, T, _ = p.shape
    W = MIX_WIDTH
    pos = pos0 + jnp.arange(T)
    q = rotary(heads(p[..., :W]), pos)
    k = rotary(heads(p[..., W:2 * W]), pos) * HEAD_DIM ** -0.5
    v = heads(p[..., 2 * W:3 * W])
    g = p[..., 3 * W:]
    o, S = retention_chunked(q, k, v, S0)
    o = head_norm(o).reshape(B, T, W) * jax.nn.silu(g)
    return o, S


def mlstm_chunked(q, k, v, ig, lf, C0, n0, m0):
    T = q.shape[1]
    L = math.gcd(T, CHUNK)
    causal = causal_mask(L)

    def step(carry, inp):
        C, n, m = carry
        qc, kc, vc, igc, lfc = inp
        b = jnp.cumsum(lfc, axis=-1)
        log_d = jnp.where(causal, b[..., :, None] - b[..., None, :] + igc[..., None, :], MASK_NEG)
        m_inter = b + m[..., None]
        m_t = jnp.maximum(m_inter, jnp.max(log_d, axis=-1))
        sc = jnp.einsum('bhtd,bhsd->bhts', qc, kc) * jnp.exp(log_d - m_t[..., None])
        w_inter = jnp.exp(m_inter - m_t)
        num = jnp.einsum('bhts,bhsv->bhtv', sc, vc) + w_inter[..., None] * jnp.einsum('bhtd,bhdv->bhtv', qc, C)
        den = jnp.sum(sc, axis=-1) + w_inter * jnp.einsum('bhtd,bhd->bht', qc, n)
        h = num / jnp.maximum(jnp.abs(den), jnp.exp(-m_t))[..., None]
        m_L = m_t[..., -1]
        a_prev = jnp.exp(b[..., -1] + m - m_L)
        a_s = jnp.exp(igc + b[..., -1:] - b - m_L[..., None])
        C = C * a_prev[..., None, None] + jnp.einsum('bhsd,bhsv->bhdv', kc * a_s[..., None], vc)
        n = n * a_prev[..., None] + jnp.einsum('bhs,bhsd->bhd', a_s, kc)
        return (C, n, m_L), h

    (C, n, m), h = lax.scan(step, (C0, n0, m0),
                            (to_chunks(q, L), to_chunks(k, L), to_chunks(v, L),
                             to_chunks(ig, L), to_chunks(lf, L)))
    return from_chunks(h), C, n, m


def mlstm_mixer(p, conv_prev, C0, n0, m0, conv_w, conv_b, ig_b, fg_b, norm_g):
    B, T, _ = p.shape
    W = MIX_WIDTH
    H = N_HEADS
    qk_raw = p[..., :MLSTM_QK]
    full = jnp.concatenate([conv_prev, qk_raw], axis=1)
    qk = lax.conv_general_dilated(full, conv_w.astype(full.dtype)[:, None, :], window_strides=(1,),
                                  padding='VALID', dimension_numbers=('NWC', 'WIO', 'NWC'),
                                  feature_group_count=MLSTM_QK)
    qk = jax.nn.silu(qk + conv_b)
    q = heads(qk[..., :W])
    k = heads(qk[..., W:]) * HEAD_DIM ** -0.5
    v = heads(p[..., 2 * W:3 * W])
    ig = p[..., 3 * W:3 * W + H] + ig_b
    lf = jax.nn.log_sigmoid(p[..., 3 * W + H:3 * W + 2 * H] + fg_b)
    og = p[..., 3 * W + 2 * H:]
    h, C, n, m = mlstm_chunked(q, k, v, ig, lf, C0, n0, m0)
    h = jax.nn.sigmoid(og) * head_norm(h, norm_g, center=True).reshape(B, T, W)
    return h, full[:, -(MLSTM_CONV - 1):], C, n, m


def peer_ffn(xn, w_q, keys, U, V):
    B, T, D = xn.shape
    n = B * T
    xf = xn.reshape(n, D)
    q = (xf @ w_q).astype(jnp.float32).reshape(n, PEER_HEADS, 2, PEER_QHALF)
    s = jnp.einsum('nhcd,hckd->nhck', q, keys.astype(jnp.float32))
    s1, i1 = lax.top_k(s[:, :, 0], PEER_TOPK)
    s2, i2 = lax.top_k(s[:, :, 1], PEER_TOPK)
    cand = (s1[..., :, None] + s2[..., None, :]).reshape(n, PEER_HEADS, PEER_TOPK * PEER_TOPK)
    cidx = (i1[..., :, None] * PEER_N_KEYS + i2[..., None, :]).reshape(n, PEER_HEADS, PEER_TOPK * PEER_TOPK)
    top, sel = lax.top_k(cand, PEER_TOPK)
    eidx = jnp.take_along_axis(cidx, sel, axis=-1)
    gate = jax.nn.softmax(top, axis=-1)
    blk = math.gcd(n, PEER_BLOCK)
    nb = n // blk
    kk = PEER_HEADS * PEER_TOPK

    def block(args):
        xb, eb, gb = args
        hid = jax.nn.gelu(jnp.einsum('nkd,nd->nk', U[eb], xb).astype(jnp.float32), approximate=False)
        return jnp.einsum('nk,nkd->nd', (hid * gb).astype(xb.dtype), V[eb])

    out = lax.map(block, (xf.reshape(nb, blk, D), eidx.reshape(nb, blk, kk), gate.reshape(nb, blk, kk)))
    return out.reshape(B, T, D).astype(xn.dtype)


def hybrid_layer(x, st, pos0, P, l):
    B, T, _ = x.shape
    xn = rms_norm(x, P['ln1_g'][l])
    proj = (xn @ P['w_in'][l]).astype(jnp.float32)
    o1 = RWKV_COLS
    o2 = o1 + HGRN_COLS
    o3 = o2 + RET_COLS
    o4 = o3 + MLSTM_COLS
    y_a, shift_new, S_a = rwkv7_mixer(proj[..., :o1], st[0], st[1], P['rwkv_mu'][l], P['rwkv_w0'][l],
                                      P['rwkv_w_up'][l], P['rwkv_a0'][l], P['rwkv_a_up'][l],
                                      P['rwkv_g_up'][l], P['rwkv_k_k'][l], P['rwkv_k_a'][l],
                                      P['rwkv_r_k'][l], P['rwkv_ln_g'][l], P['rwkv_ln_b'][l])
    y_b, S_b = hgrn2_mixer(proj[..., o1:o2], st[2], P['hgrn_lb'][l], P['hgrn_norm_g'][l])
    y_c, S_c = retention_mixer(proj[..., o2:o3], st[3], pos0)
    y_d, conv_new, C_d, n_d, m_d = mlstm_mixer(proj[..., o3:o4], st[4], st[5], st[6], st[7],
                                               P['mlstm_conv_w'][l], P['mlstm_conv_b'][l],
                                               P['mlstm_ig_b'][l], P['mlstm_fg_b'][l],
                                               P['mlstm_norm_g'][l])
    gates = jax.nn.sigmoid(proj[..., o4:]).reshape(B, T, N_MIX, D_MODEL)
    ys = jnp.stack([y_a, y_b, y_c, y_d], axis=2)
    branch = jnp.einsum('btmw,mwd->btmd', ys, P['w_branch'][l].astype(jnp.float32))
    merged = jnp.sum(gates * branch, axis=2).astype(x.dtype)
    x = x + merged @ P['w_out'][l]
    x = x + peer_ffn(rms_norm(x, P['ln2_g'][l]), P['peer_w_q'][l], P['peer_keys'][l],
                     P['peer_u'][l], P['peer_v'][l])
    return x, (shift_new, S_a, S_b, S_c, conv_new, C_d, n_d, m_d)


def run_trunk(x, states, pos0, P):
    new = tuple([] for _ in states)
    for l in range(DEPTH):
        st = tuple(s[l].astype(jnp.float32) for s in states)
        x, st_new = hybrid_layer(x, st, pos0, P, l)
        for acc, s in zip(new, st_new):
            acc.append(s)
    return rms_norm(x, P['lnf_g']), tuple(jnp.stack(acc, axis=0) for acc in new)


def setup_inputs(seed: int = 0) -> dict:
    key = jax.random.key(seed)
    ks = iter(jax.random.split(key, 64))
    nrm = lambda shape, scale: scale * jax.random.normal(next(ks), shape, jnp.float32)
    gain = lambda shape: 1.0 + 0.02 * jax.random.normal(next(ks), shape, jnp.float32)
    Lr, W, H, Dh = DEPTH, MIX_WIDTH, N_HEADS, HEAD_DIM
    return {
        'x_prompt': nrm((BATCH, SEQ, D_MODEL), 1.0),
        'x_sample': nrm((DEC_BATCH, DEC_SEQ, D_MODEL), 1.0),
        'state_rwkv_shift': nrm((Lr, DEC_BATCH, RWKV_COLS), 1.0),
        'state_rwkv_S': nrm((Lr, DEC_BATCH, H, Dh, Dh), 0.3),
        'state_hgrn_S': nrm((Lr, DEC_BATCH, H, Dh, Dh), 0.3),
        'state_ret_S': nrm((Lr, DEC_BATCH, H, Dh, Dh), 1.0),
        'state_mlstm_conv': nrm((Lr, DEC_BATCH, MLSTM_CONV - 1, MLSTM_QK), 1.0),
        'state_mlstm_C': nrm((Lr, DEC_BATCH, H, Dh, Dh), 0.3),
        'state_mlstm_n': nrm((Lr, DEC_BATCH, H, Dh), 1.0),
        'state_mlstm_m': nrm((Lr, DEC_BATCH, H), 1.0),
        'ln1_g': gain((Lr, D_MODEL)),
        'w_in': nrm((Lr, D_MODEL, IN_COLS), D_MODEL ** -0.5),
        'rwkv_mu': jax.random.uniform(next(ks), (Lr, RWKV_COLS), jnp.float32, 0.0, 1.0),
        'rwkv_w0': jax.random.uniform(next(ks), (Lr, W), jnp.float32, -6.0, -1.0),
        'rwkv_w_up': nrm((Lr, RWKV_W_RANK, W), 0.1 * RWKV_W_RANK ** -0.5),
        'rwkv_a0': nrm((Lr, W), 0.1),
        'rwkv_a_up': nrm((Lr, RWKV_A_RANK, W), 0.5 * RWKV_A_RANK ** -0.5),
        'rwkv_g_up': nrm((Lr, RWKV_G_RANK, W), RWKV_G_RANK ** -0.5),
        'rwkv_k_k': 0.85 + nrm((Lr, W), 0.05),
        'rwkv_k_a': 1.0 + nrm((Lr, W), 0.05),
        'rwkv_r_k': nrm((Lr, W), 0.1),
        'rwkv_ln_g': gain((Lr, W)),
        'rwkv_ln_b': nrm((Lr, W), 0.02),
        'hgrn_lb_logits': nrm((Lr, W), 0.1),
        'hgrn_norm_g': gain((Lr, W)),
        'mlstm_conv_w': nrm((Lr, MLSTM_CONV, MLSTM_QK), MLSTM_CONV ** -0.5),
        'mlstm_conv_b': nrm((Lr, MLSTM_QK), 0.02),
        'mlstm_ig_b': nrm((Lr, H), 0.1),
        'mlstm_fg_b': jnp.linspace(3.0, 6.0, H, dtype=jnp.float32)[None, :] + nrm((Lr, H), 0.1),
        'mlstm_norm_g': gain((Lr, W)),
        'w_branch': nrm((Lr, N_MIX, W, D_MODEL), W ** -0.5),
        'w_out': nrm((Lr, D_MODEL, D_MODEL), D_MODEL ** -0.5),
        'ln2_g': gain((Lr, D_MODEL)),
        'peer_w_q': nrm((Lr, D_MODEL, PEER_HEADS * PEER_QDIM), D_MODEL ** -0.5),
        'peer_keys': nrm((Lr, PEER_HEADS, 2, PEER_N_KEYS, PEER_QHALF), PEER_QHALF ** -0.5),
        'peer_u': nrm((Lr, PEER_N_EXPERTS, D_MODEL), D_MODEL ** -0.5),
        'peer_v': nrm((Lr, PEER_N_EXPERTS, D_MODEL), PEER_HEADS ** -0.5),
        'lnf_g': gain((D_MODEL,)),
    }


def reference(x_prompt, x_sample, state_rwkv_shift, state_rwkv_S, state_hgrn_S, state_ret_S,
              state_mlstm_conv, state_mlstm_C, state_mlstm_n, state_mlstm_m,
              ln1_g, w_in, rwkv_mu, rwkv_w0, rwkv_w_up, rwkv_a0, rwkv_a_up, rwkv_g_up,
              rwkv_k_k, rwkv_k_a, rwkv_r_k, rwkv_ln_g, rwkv_ln_b, hgrn_lb_logits, hgrn_norm_g,
              mlstm_conv_w, mlstm_conv_b, mlstm_ig_b, mlstm_fg_b, mlstm_norm_g,
              w_branch, w_out, ln2_g, peer_w_q, peer_keys, peer_u, peer_v, lnf_g):
    lb_all = jax.nn.softmax(hgrn_lb_logits.astype(jnp.float32), axis=0)
    hgrn_lb = jnp.cumsum(lb_all, axis=0) - lb_all[0:1]
    P = dict(ln1_g=ln1_g, w_in=w_in, rwkv_mu=rwkv_mu, rwkv_w0=rwkv_w0, rwkv_w_up=rwkv_w_up,
             rwkv_a0=rwkv_a0, rwkv_a_up=rwkv_a_up, rwkv_g_up=rwkv_g_up, rwkv_k_k=rwkv_k_k,
             rwkv_k_a=rwkv_k_a, rwkv_r_k=rwkv_r_k, rwkv_ln_g=rwkv_ln_g, rwkv_ln_b=rwkv_ln_b,
             hgrn_lb=hgrn_lb, hgrn_norm_g=hgrn_norm_g, mlstm_conv_w=mlstm_conv_w,
             mlstm_conv_b=mlstm_conv_b, mlstm_ig_b=mlstm_ig_b, mlstm_fg_b=mlstm_fg_b,
             mlstm_norm_g=mlstm_norm_g, w_branch=w_branch, w_out=w_out, ln2_g=ln2_g,
             peer_w_q=peer_w_q, peer_keys=peer_keys, peer_u=peer_u, peer_v=peer_v, lnf_g=lnf_g)

    Bp = x_prompt.shape[0]
    z = lambda *shape: jnp.zeros((DEPTH, Bp) + shape, jnp.float32)
    prompt_init = (z(RWKV_COLS), z(N_HEADS, HEAD_DIM, HEAD_DIM), z(N_HEADS, HEAD_DIM, HEAD_DIM),
                   z(N_HEADS, HEAD_DIM, HEAD_DIM), z(MLSTM_CONV - 1, MLSTM_QK),
                   z(N_HEADS, HEAD_DIM, HEAD_DIM), z(N_HEADS, HEAD_DIM), z(N_HEADS))
    y_prompt, (p_rwkv_shift, p_rwkv_S, p_hgrn_S, p_ret_S, p_mlstm_conv, p_mlstm_C, p_mlstm_n,
               p_mlstm_m) = run_trunk(x_prompt, prompt_init, 0, P)

    sample_init = (state_rwkv_shift, state_rwkv_S, state_hgrn_S, state_ret_S, state_mlstm_conv,
                   state_mlstm_C, state_mlstm_n, state_mlstm_m)
    y_sample, (s_rwkv_shift, s_rwkv_S, s_hgrn_S, s_ret_S, s_mlstm_conv, s_mlstm_C, s_mlstm_n,
               s_mlstm_m) = run_trunk(x_sample, sample_init, PAST_LEN, P)

    return (y_prompt, y_sample,
            p_rwkv_shift, p_rwkv_S, p_hgrn_S, p_ret_S, p_mlstm_conv, p_mlstm_C, p_mlstm_n, p_mlstm_m,
            s_rwkv_shift, s_rwkv_S, s_hgrn_S, s_ret_S, s_mlstm_conv, s_mlstm_C, s_mlstm_n, s_mlstm_m)
```

```python
import functools
import math

import numpy as np
import jax
import jax.numpy as jnp
from jax import lax
from jax.experimental import pallas as pl
from jax.experimental.pallas import tpu as pltpu

F32 = jnp.float32
BF16 = jnp.bfloat16
HI = lax.Precision.HIGHEST

D_MODEL = 1024
N_HEADS = 4
HEAD_DIM = 64
HALF_DIM = HEAD_DIM // 2
MIX_W = N_HEADS * HEAD_DIM
RWKV_W_RANK = 64
RWKV_A_RANK = 64
RWKV_G_RANK = 128
RWKV_COLS = 3 * MIX_W + RWKV_W_RANK + RWKV_A_RANK + RWKV_G_RANK
MLSTM_COLS = 4 * MIX_W + 2 * N_HEADS
GATE_COLS = 4 * D_MODEL
MLSTM_CONV = 4
CHUNK = 64
SUB = 16
RET_THETA = 10000.0
PEER_HEADS = 8
PEER_N_KEYS = 128
PEER_TOPK = 16
PEER_QHALF = 64
NORM_EPS = 1e-6
RWKV_GN_EPS = 64e-5
HEAD_NORM_EPS = 1e-5
MASK_NEG = -1e30
LB_FLOOR = 1e-30
NEG_BIG = -1e30
INV_SQRT2 = 1.0 / math.sqrt(2.0)
RET_LOG_GAMMA = tuple(math.log(1.0 - 2.0 ** (-5.0 - h)) for h in range(N_HEADS))
LANES = 128
VMEM_LIMIT = 56 * 1024 * 1024


def _dot(a, b, prec=None):
    return jnp.dot(a, b, precision=prec, preferred_element_type=F32)


def _dot_nt(a, b, prec=None):
    return lax.dot_general(a, b, (((1,), (1,)), ((), ())), precision=prec, preferred_element_type=F32)


def _dot_tn(a, b, prec=None):
    return lax.dot_general(a, b, (((0,), (0,)), ((), ())), precision=prec, preferred_element_type=F32)


def _params(*sem):
    return pltpu.CompilerParams(dimension_semantics=sem, vmem_limit_bytes=VMEM_LIMIT)


def _full(shape):
    nd = len(shape)
    return pl.BlockSpec(shape, lambda *_: (0,) * nd)


def _iota(shape, axis):
    return lax.broadcasted_iota(jnp.int32, shape, axis)


def _shift_rows(cur, prev, j):
    row = _iota(cur.shape, 0)
    return jnp.where(row >= j, pltpu.roll(cur, j, 0), pltpu.roll(prev, j, 0))


def _head_norm_rows(y, eps, center):
    if center:
        y = y - jnp.mean(y, axis=1, keepdims=True)
    return y * lax.rsqrt(jnp.mean(y * y, axis=1, keepdims=True) + eps)


def _head_norm_cols(y, eps, center):
    if center:
        y = y - jnp.mean(y, axis=0, keepdims=True)
    return y * lax.rsqrt(jnp.mean(y * y, axis=0, keepdims=True) + eps)


def _proj_kernel(x_ref, g_ref, w_ref, wg_ref, xn_ref, pr_ref, ph_ref, pt_ref, pm_ref, pg_ref):
    x = x_ref[...]
    xn = x * lax.rsqrt(jnp.mean(x * x, axis=-1, keepdims=True) + NORM_EPS) * g_ref[...]
    xb = xn.astype(BF16)
    xn_ref[...] = xb
    p = _dot(xb, w_ref[...])
    pr_ref[...] = p[:, 0:1024]
    ph_ref[...] = p[:, 1024:2048]
    pt_ref[...] = p[:, 2048:3072]
    pm_ref[...] = p[:, 3072:4096]
    pg_ref[...] = _dot(xn, wg_ref[...], HI)


def _proj_call(x2d, g, w_main, w_mlg, tm):
    n = x2d.shape[0]
    row = lambda i: (i, 0)
    out = lambda c, dt: jax.ShapeDtypeStruct((n, c), dt)
    return pl.pallas_call(
        _proj_kernel,
        grid=(n // tm,),
        in_specs=[pl.BlockSpec((tm, D_MODEL), row), _full((1, D_MODEL)),
                  _full((D_MODEL, 4096)), _full((D_MODEL, LANES))],
        out_specs=[pl.BlockSpec((tm, D_MODEL), row)] * 5 + [pl.BlockSpec((tm, LANES), row)],
        out_shape=[out(D_MODEL, BF16)] + [out(D_MODEL, F32)] * 4 + [out(LANES, F32)],
        compiler_params=_params("parallel"),
        name="norm_proj",
    )(x2d, g.reshape(1, D_MODEL), w_main, w_mlg)


def _neumann_inverse(nmat, eye):
    t = eye - nmat
    pw = nmat
    for _ in range(5):
        pw = _dot(pw, pw, HI)
        t = t + _dot(t, pw, HI)
    return t


def _rwkv_prompt_kernel(p_ref, mu_ref, w0_ref, wup_ref, a0_ref, aup_ref, gup_ref, kk_ref, ka_ref, rk_ref,
                        lng_ref, lnb_ref, y_ref, shift_ref, s_ref, prow_ref):
    c = pl.program_id(1)

    @pl.when(c == 0)
    def _():
        s_ref[...] = jnp.zeros_like(s_ref)
        prow_ref[...] = jnp.zeros_like(prow_ref)

    p = p_ref[...]
    L = p.shape[0]
    row = _iota(p.shape, 0)
    prev = jnp.where(row >= 1, pltpu.roll(p, 1, 0), jnp.broadcast_to(prow_ref[...], p.shape))
    last = p[L - 1:L, :]
    prow_ref[...] = last
    shift_ref[...] = last
    xs = p + mu_ref[...] * (prev - p)
    W = MIX_W
    r_all, k_all, v_all = xs[:, 0:W], xs[:, W:2 * W], xs[:, 2 * W:3 * W]
    o = 3 * W
    xw = xs[:, o:o + RWKV_W_RANK]
    o += RWKV_W_RANK
    xa = xs[:, o:o + RWKV_A_RANK]
    o += RWKV_A_RANK
    xg = xs[:, o:o + RWKV_G_RANK]
    w = w0_ref[...] + _dot(jnp.tanh(xw), wup_ref[...], HI)
    logw_all = -jnp.exp(-jax.nn.softplus(-w) - 0.5)
    a_all = jax.nn.sigmoid(a0_ref[...] + _dot(xa, aup_ref[...], HI))
    g_all = _dot(jax.nn.sigmoid(xg), gup_ref[...], HI)
    kk_all = k_all * kk_ref[...]
    kq_all = k_all * (1.0 + (a_all - 1.0) * ka_ref[...])

    rr = _iota((L, L), 0)
    cc = _iota((L, L), 1)
    tri = (rr >= cc).astype(F32)
    strict = rr > cc
    incl = rr >= cc
    eye = (rr == cc).astype(F32)
    outs = []
    for h in range(N_HEADS):
        sl = slice(h * HEAD_DIM, (h + 1) * HEAD_DIM)
        rh, kh, vh, ah, logw = r_all[:, sl], kq_all[:, sl], v_all[:, sl], a_all[:, sl], logw_all[:, sl]
        kkh = kk_all[:, sl]
        kkh = kkh / jnp.maximum(jnp.sqrt(jnp.sum(kkh * kkh, axis=1, keepdims=True)), 1e-12)
        bh = kkh * ah
        cum = _dot(tri, logw, HI)
        cum_l = cum[L - 1:L, :]
        e_neg = jnp.exp(-cum)
        e_tail = jnp.exp(cum_l - cum)
        kkd = kkh * jnp.exp(cum - logw)
        rd = rh * jnp.exp(cum)
        s0 = s_ref[h]
        lhs = jnp.concatenate([kkd, rd], axis=0)
        rhs = jnp.concatenate([kh * e_neg, bh * e_neg], axis=0)
        gm = _dot_nt(lhs, rhs, HI)
        qs = _dot_nt(lhs, s0, HI)
        n_k = jnp.where(strict, gm[0:L, 0:L], 0.0)
        n_b = jnp.where(strict, gm[0:L, L:2 * L], 0.0)
        m_k = jnp.where(incl, gm[L:2 * L, 0:L], 0.0)
        m_b = jnp.where(incl, gm[L:2 * L, L:2 * L], 0.0)
        tinv = _neumann_inverse(n_b, eye)
        u = _dot(tinv, qs[0:L] + _dot(n_k, vh, HI), HI)
        y = qs[L:2 * L] + _dot(m_k, vh, HI) - _dot(m_b, u, HI)
        s_ref[h] = (s0 * jnp.exp(cum_l)
                    + _dot_tn(vh, kh * e_tail, HI) - _dot_tn(u, bh * e_tail, HI))
        y = _head_norm_rows(y, RWKV_GN_EPS, True) * lng_ref[:, sl] + lnb_ref[:, sl]
        y = y + jnp.sum(rh * kh * rk_ref[:, sl], axis=1, keepdims=True) * vh
        outs.append(y * g_all[:, sl])
    y_ref[...] = jnp.concatenate(outs, axis=1)


def _chunk_specs(nc, width):
    return pl.BlockSpec((CHUNK, width), lambda b, c: (b * nc + c, 0))


def _state_spec():
    return pl.BlockSpec((None, N_HEADS, HEAD_DIM, HEAD_DIM), lambda b, c: (b, 0, 0, 0))


def _rwkv_prompt_call(p, bsz, prm):
    n = p.shape[0]
    nc = n // bsz // CHUNK
    small = [prm[k] for k in ("mu", "w0", "wup", "a0", "aup", "gup", "kk", "ka", "rk", "lng", "lnb")]
    return pl.pallas_call(
        _rwkv_prompt_kernel,
        grid=(bsz, nc),
        in_specs=[_chunk_specs(nc, RWKV_COLS)] + [_full(a.shape) for a in small],
        out_specs=[_chunk_specs(nc, MIX_W),
                   pl.BlockSpec((None, 1, RWKV_COLS), lambda b, c: (b, 0, 0)),
                   _state_spec()],
        out_shape=[jax.ShapeDtypeStruct((n, MIX_W), F32),
                   jax.ShapeDtypeStruct((bsz, 1, RWKV_COLS), F32),
                   jax.ShapeDtypeStruct((bsz, N_HEADS, HEAD_DIM, HEAD_DIM), F32)],
        scratch_shapes=[pltpu.VMEM((1, RWKV_COLS), F32)],
        compiler_params=_params("parallel", "arbitrary"),
        name="rwkv_prompt",
    )(p, *small)


def _hgrn_lower_bound(logits, layer):
    l0, l1 = logits[0:1, :], logits[1:2, :]
    m = jnp.maximum(l0, l1)
    e0, e1 = jnp.exp(l0 - m), jnp.exp(l1 - m)
    z = e0 + e1
    lb0, lb1 = e0 / z, e1 / z
    return (lb0 - lb0) if layer == 0 else ((lb0 + lb1) - lb0)


def _hgrn_log_f(z, lb):
    return jnp.logaddexp(jnp.log(jnp.maximum(lb, LB_FLOOR)), jnp.log1p(-lb) + jax.nn.log_sigmoid(z))


def _hgrn_prompt_kernel(layer, p_ref, lg_ref, ng_ref, y_ref, s_ref, st_ref):
    c = pl.program_id(1)

    @pl.when(c == 0)
    def _():
        st_ref[...] = jnp.zeros_like(st_ref)

    p = p_ref[...]
    L = p.shape[0]
    W = MIX_W
    q = jax.nn.silu(p[:, 0:W])
    lb = _hgrn_lower_bound(lg_ref[...], layer)
    log_f = _hgrn_log_f(p[:, W:2 * W], lb)
    k = 1.0 - jnp.exp(log_f)
    v = p[:, 2 * W:3 * W]
    g = p[:, 3 * W:4 * W]
    rr = _iota((L, L), 0)
    cc = _iota((L, L), 1)
    tri_blk = jnp.logical_and(rr >= cc, (rr // SUB) == (cc // SUB)).astype(F32)
    b = _dot(tri_blk, log_f, HI)
    qe = q * jnp.exp(b)
    row = _iota((L, W), 0) % SUB
    intra = [jnp.zeros((L, HEAD_DIM), F32) for _ in range(N_HEADS)]
    for lag in range(SUB):
        valid = row >= lag
        kr = k if lag == 0 else pltpu.roll(k, lag, 0)
        br = b if lag == 0 else pltpu.roll(b, lag, 0)
        vr = v if lag == 0 else pltpu.roll(v, lag, 0)
        term = jnp.where(valid, q * kr * jnp.exp(jnp.where(valid, b - br, 0.0)), 0.0)
        for h in range(N_HEADS):
            sl = slice(h * HEAD_DIM, (h + 1) * HEAD_DIM)
            intra[h] = intra[h] + jnp.sum(term[:, sl], axis=1, keepdims=True) * vr[:, sl]
    outs = []
    for h in range(N_HEADS):
        sl = slice(h * HEAD_DIM, (h + 1) * HEAD_DIM)
        st = st_ref[h]
        pieces = []
        for j in range(L // SUB):
            rs = slice(j * SUB, (j + 1) * SUB)
            pieces.append(_dot_nt(qe[rs, sl], st, HI))
            b_l = b[(j + 1) * SUB - 1:(j + 1) * SUB, sl]
            kd = k[rs, sl] * jnp.exp(b_l - b[rs, sl])
            st = st * jnp.exp(b_l) + _dot_tn(v[rs, sl], kd, HI)
        st_ref[h] = st
        s_ref[h] = st.T
        o = jnp.concatenate(pieces, axis=0) + intra[h]
        outs.append(_head_norm_rows(o, HEAD_NORM_EPS, False) * ng_ref[:, sl] * jax.nn.silu(g[:, sl]))
    y_ref[...] = jnp.concatenate(outs, axis=1)


def _hgrn_prompt_call(p, bsz, logits, norm_g, layer):
    n = p.shape[0]
    nc = n // bsz // CHUNK
    return pl.pallas_call(
        functools.partial(_hgrn_prompt_kernel, layer),
        grid=(bsz, nc),
        in_specs=[_chunk_specs(nc, 4 * MIX_W), _full(logits.shape), _full(norm_g.shape)],
        out_specs=[_chunk_specs(nc, MIX_W), _state_spec()],
        out_shape=[jax.ShapeDtypeStruct((n, MIX_W), F32),
                   jax.ShapeDtypeStruct((bsz, N_HEADS, HEAD_DIM, HEAD_DIM), F32)],
        scratch_shapes=[pltpu.VMEM((N_HEADS, HEAD_DIM, HEAD_DIM), F32)],
        compiler_params=_params("parallel", "arbitrary"),
        name="hgrn_prompt",
    )(p, logits, norm_g)


def _ret_prompt_kernel(pos0, p_ref, inv_ref, y_ref, s_ref):
    c = pl.program_id(1)

    @pl.when(c == 0)
    def _():
        s_ref[...] = jnp.zeros_like(s_ref)

    p = p_ref[...]
    L = p.shape[0]
    W = MIX_W
    pos = (pos0 + c * L + _iota((L, W), 0)).astype(F32)
    ang = pos * inv_ref[...]
    cosv, sinv = jnp.cos(ang), jnp.sin(ang)
    first = (_iota((L, W), 1) % HEAD_DIM) < HALF_DIM

    def rot(z):
        partner = jnp.where(first, -pltpu.roll(z, W - HALF_DIM, 1), pltpu.roll(z, HALF_DIM, 1))
        return z * cosv + partner * sinv

    q = rot(p[:, 0:W])
    k = rot(p[:, W:2 * W]) * HEAD_DIM ** -0.5
    v = p[:, 2 * W:3 * W]
    g = p[:, 3 * W:4 * W]
    rel = (_iota((L, L), 0) - _iota((L, L), 1)).astype(F32)
    t_col = _iota((L, 1), 0).astype(F32)
    outs = []
    for h in range(N_HEADS):
        sl = slice(h * HEAD_DIM, (h + 1) * HEAD_DIM)
        lg = RET_LOG_GAMMA[h]
        dmat = jnp.where(rel >= 0, jnp.exp(lg * jnp.maximum(rel, 0.0)), 0.0)
        d_in = jnp.exp(lg * (t_col + 1.0))
        d_st = jnp.exp(lg * (L - 1.0 - t_col))
        s0 = s_ref[h]
        att = _dot_nt(q[:, sl], k[:, sl], HI) * dmat
        o = _dot(att, v[:, sl], HI) + _dot(q[:, sl], s0, HI) * d_in
        s_ref[h] = s0 * math.exp(lg * L) + _dot_tn(k[:, sl] * d_st, v[:, sl], HI)
        outs.append(_head_norm_rows(o, HEAD_NORM_EPS, False) * jax.nn.silu(g[:, sl]))
    y_ref[...] = jnp.concatenate(outs, axis=1)


def _rotary_inv_row():
    inv = RET_THETA ** (-(np.arange(MIX_W) % HALF_DIM).astype(np.float64) / HALF_DIM)
    return jnp.asarray(inv.astype(np.float32).reshape(1, MIX_W))


def _ret_prompt_call(p, bsz, pos0):
    n = p.shape[0]
    nc = n // bsz // CHUNK
    inv = _rotary_inv_row()
    return pl.pallas_call(
        functools.partial(_ret_prompt_kernel, pos0),
        grid=(bsz, nc),
        in_specs=[_chunk_specs(nc, 4 * MIX_W), _full(inv.shape)],
        out_specs=[_chunk_specs(nc, MIX_W), _state_spec()],
        out_shape=[jax.ShapeDtypeStruct((n, MIX_W), F32),
                   jax.ShapeDtypeStruct((bsz, N_HEADS, HEAD_DIM, HEAD_DIM), F32)],
        compiler_params=_params("parallel", "arbitrary"),
        name="ret_prompt",
    )(p, inv)


def _mlstm_prompt_kernel(p_ref, pg_ref, cw_ref, cb_ref, gb_ref, ng_ref,
                         y_ref, conv_ref, c_ref, n_ref, m_ref, prev_ref):
    c = pl.program_id(1)

    @pl.when(c == 0)
    def _():
        c_ref[...] = jnp.zeros_like(c_ref)
        n_ref[...] = jnp.zeros_like(n_ref)
        m_ref[...] = jnp.zeros_like(m_ref)
        prev_ref[...] = jnp.zeros_like(prev_ref)

    p = p_ref[...]
    L = p.shape[0]
    W = MIX_W
    raw = p[:, 0:2 * W]
    prev = prev_ref[...]
    acc = cw_ref[MLSTM_CONV - 1:MLSTM_CONV, :] * raw
    for j in range(1, MLSTM_CONV):
        acc = acc + cw_ref[MLSTM_CONV - 1 - j:MLSTM_CONV - j, :] * _shift_rows(raw, prev, j)
    prev_ref[...] = raw
    conv_ref[...] = raw[L - 8:L, :]
    qk = jax.nn.silu(acc + cb_ref[...])
    q_all = qk[:, 0:W]
    k_all = qk[:, W:2 * W] * HEAD_DIM ** -0.5
    v_all = p[:, 2 * W:3 * W]
    og = p[:, 3 * W:4 * W]

    gb = pg_ref[...] + gb_ref[...]
    rr = _iota((L, L), 0)
    cc = _iota((L, L), 1)
    causal = rr >= cc
    bcum = _dot(causal.astype(F32), jax.nn.log_sigmoid(gb), HI)
    gb_t = gb.T
    bcum_t = bcum.T
    m_all = m_ref[...]
    mr = _iota(m_all.shape, 0)
    ml = _iota(m_all.shape, 1)
    outs = []
    for h in range(N_HEADS):
        sl = slice(h * HEAD_DIM, (h + 1) * HEAD_DIM)
        qh, kh, vh = q_all[:, sl], k_all[:, sl], v_all[:, sl]
        ig_row = gb_t[h:h + 1, :]
        ig_col = gb[:, h:h + 1]
        b_col = bcum[:, N_HEADS + h:N_HEADS + h + 1]
        b_row = bcum_t[N_HEADS + h:N_HEADS + h + 1, :]
        m_prev = m_all[0:1, h:h + 1]
        log_d = jnp.where(causal, b_col - b_row + ig_row, MASK_NEG)
        m_inter = b_col + m_prev
        m_t = jnp.maximum(m_inter, jnp.max(log_d, axis=1, keepdims=True))
        sc = _dot_nt(qh, kh, HI) * jnp.exp(log_d - m_t)
        w_inter = jnp.exp(m_inter - m_t)
        c0 = c_ref[h]
        n0 = n_ref[h:h + 1, :]
        num = _dot(sc, vh, HI) + w_inter * _dot(qh, c0, HI)
        den = jnp.sum(sc, axis=1, keepdims=True) + w_inter * jnp.sum(qh * n0, axis=1, keepdims=True)
        hh = num / jnp.maximum(jnp.abs(den), jnp.exp(-m_t))
        m_l = m_t[L - 1:L, :]
        b_l = b_col[L - 1:L, :]
        a_prev = jnp.exp(b_l + m_prev - m_l)
        a_s = jnp.exp(ig_col + b_l - b_col - m_l)
        ks = kh * a_s
        c_ref[h] = c0 * a_prev + _dot_tn(ks, vh, HI)
        n_ref[h:h + 1, :] = n0 * a_prev + jnp.sum(ks, axis=0, keepdims=True)
        m_all = jnp.where(jnp.logical_and(mr == 0, ml == h), m_l, m_all)
        hn = _head_norm_rows(hh, HEAD_NORM_EPS, True) * ng_ref[:, sl]
        outs.append(jax.nn.sigmoid(og[:, sl]) * hn)
    m_ref[...] = m_all
    y_ref[...] = jnp.concatenate(outs, axis=1)


def _mlstm_prompt_call(p, pg, bsz, conv_w, conv_b, gate_b, norm_g):
    n = p.shape[0]
    nc = n // bsz // CHUNK
    bmap = lambda b, c: (b, 0, 0)
    return pl.pallas_call(
        _mlstm_prompt_kernel,
        grid=(bsz, nc),
        in_specs=[_chunk_specs(nc, 4 * MIX_W), _chunk_specs(nc, LANES), _full(conv_w.shape),
                  _full(conv_b.shape), _full(gate_b.shape), _full(norm_g.shape)],
        out_specs=[_chunk_specs(nc, MIX_W),
                   pl.BlockSpec((None, 8, 2 * MIX_W), bmap),
                   _state_spec(),
                   pl.BlockSpec((None, N_HEADS, HEAD_DIM), bmap),
                   pl.BlockSpec((None, 8, LANES), bmap)],
        out_shape=[jax.ShapeDtypeStruct((n, MIX_W), F32),
                   jax.ShapeDtypeStruct((bsz, 8, 2 * MIX_W), F32),
                   jax.ShapeDtypeStruct((bsz, N_HEADS, HEAD_DIM, HEAD_DIM), F32),
                   jax.ShapeDtypeStruct((bsz, N_HEADS, HEAD_DIM), F32),
                   jax.ShapeDtypeStruct((bsz, 8, LANES), F32)],
        scratch_shapes=[pltpu.VMEM((CHUNK, 2 * MIX_W), F32)],
        compiler_params=_params("parallel", "arbitrary"),
        name="mlstm_prompt",
    )(p, pg, conv_w, conv_b, gate_b, norm_g)


def _head_rows(ref, base, h):
    return ref[pl.ds(pl.multiple_of(base + h * HEAD_DIM, HEAD_DIM), HEAD_DIM), :]


def _rwkv_sample_kernel(p_ref, sh_ref, s_ref, mu_ref, w0_ref, wup_ref, a0_ref, aup_ref, gup_ref, kk_ref, ka_ref,
                        rk_ref, lng_ref, lnb_ref, y_ref, so_ref, v_scr, y_scr):
    h = pl.program_id(0)
    W = MIX_W

    def xs_rows(lo, size, dyn):
        idx = pl.ds(pl.multiple_of(lo + h * HEAD_DIM, HEAD_DIM), size) if dyn else pl.ds(lo, size)
        pv = p_ref[idx, :]
        return pv + mu_ref[idx, :] * (sh_ref[idx, :] - pv)

    r = xs_rows(0, HEAD_DIM, True)
    k = xs_rows(W, HEAD_DIM, True)
    v = xs_rows(2 * W, HEAD_DIM, True)
    xw = xs_rows(3 * W, RWKV_W_RANK, False)
    xa = xs_rows(3 * W + RWKV_W_RANK, RWKV_A_RANK, False)
    xg = xs_rows(3 * W + RWKV_W_RANK + RWKV_A_RANK, RWKV_G_RANK, False)
    w = w0_ref[...] + _dot(wup_ref[...], jnp.tanh(xw), HI)
    wd = jnp.exp(-jnp.exp(-jax.nn.softplus(-w) - 0.5))
    a = jax.nn.sigmoid(a0_ref[...] + _dot(aup_ref[...], xa, HI))
    g = _dot(gup_ref[...], jax.nn.sigmoid(xg), HI)
    kk = k * kk_ref[...]
    kk = kk / jnp.maximum(jnp.sqrt(jnp.sum(kk * kk, axis=0, keepdims=True)), 1e-12)
    kq = k * (1.0 + (a - 1.0) * ka_ref[...])
    bb = kk * a
    v_scr[...] = v

    def body(i, carry):
        sv = s_ref[i]
        sa = -jnp.sum(sv * kk, axis=0, keepdims=True)
        snew = sv * wd + sa * bb + v_scr[pl.ds(i, 1), :] * kq
        so_ref[i] = snew
        y_scr[pl.ds(i, 1), :] = jnp.sum(snew * r, axis=0, keepdims=True)
        return carry

    lax.fori_loop(0, HEAD_DIM, body, 0)
    y = _head_norm_cols(y_scr[...], RWKV_GN_EPS, True) * lng_ref[...] + lnb_ref[...]
    y = y + jnp.sum(r * kq * rk_ref[...], axis=0, keepdims=True) * v
    y_ref[...] = y * g


def _head_block(cols):
    return pl.BlockSpec((HEAD_DIM, cols), lambda h: (h, 0))


def _state_block(bsz):
    return pl.BlockSpec((None, HEAD_DIM, HEAD_DIM, bsz), lambda h: (h, 0, 0, 0))


def _rwkv_sample_call(p_t, shift_t, s_t, prm):
    bsz = p_t.shape[1]
    return pl.pallas_call(
        _rwkv_sample_kernel,
        grid=(N_HEADS,),
        in_specs=[_full(p_t.shape), _full(shift_t.shape), _state_block(bsz), _full((RWKV_COLS, 1)),
                  _head_block(1), _head_block(RWKV_W_RANK), _head_block(1), _head_block(RWKV_A_RANK),
                  _head_block(RWKV_G_RANK), _head_block(1), _head_block(1), _head_block(1), _head_block(1),
                  _head_block(1)],
        out_specs=[_head_block(bsz), _state_block(bsz)],
        out_shape=[jax.ShapeDtypeStruct((MIX_W, bsz), F32), jax.ShapeDtypeStruct(s_t.shape, F32)],
        scratch_shapes=[pltpu.VMEM((HEAD_DIM, bsz), F32), pltpu.VMEM((HEAD_DIM, bsz), F32)],
        compiler_params=_params("parallel"),
        name="rwkv_sample",
    )(p_t, shift_t, s_t, prm["mu_c"], prm["w0_c"], prm["wup_t"], prm["a0_c"], prm["aup_t"], prm["gup_t"],
      prm["kk_c"], prm["ka_c"], prm["rk_c"], prm["lng_c"], prm["lnb_c"])


def _kv_state_step(s_ref, so_ref, q_scr, k_scr, f_scr, v, decay):
    def body(i, o):
        dec = f_scr[pl.ds(i, 1), :] if decay is None else decay
        snew = s_ref[i] * dec + k_scr[pl.ds(i, 1), :] * v
        so_ref[i] = snew
        return o + q_scr[pl.ds(i, 1), :] * snew

    return lax.fori_loop(0, HEAD_DIM, body, jnp.zeros_like(v))


def _hgrn_sample_kernel(layer, p_ref, s_ref, lg_ref, ng_ref, y_ref, so_ref, q_scr, k_scr, f_scr):
    h = pl.program_id(0)
    W = MIX_W
    lg = lg_ref[...]
    l0, l1 = lg[:, 0:1], lg[:, 1:2]
    m = jnp.maximum(l0, l1)
    e0, e1 = jnp.exp(l0 - m), jnp.exp(l1 - m)
    z = e0 + e1
    lb0, lb1 = e0 / z, e1 / z
    lb = (lb0 - lb0) if layer == 0 else ((lb0 + lb1) - lb0)
    log_f = _hgrn_log_f(_head_rows(p_ref, W, h), lb)
    f = jnp.exp(log_f)
    q_scr[...] = jax.nn.silu(_head_rows(p_ref, 0, h))
    k_scr[...] = 1.0 - f
    f_scr[...] = f
    v = _head_rows(p_ref, 2 * W, h)
    g = _head_rows(p_ref, 3 * W, h)
    o = _kv_state_step(s_ref, so_ref, q_scr, k_scr, f_scr, v, None)
    y_ref[...] = _head_norm_cols(o, HEAD_NORM_EPS, False) * ng_ref[...] * jax.nn.silu(g)


def _hgrn_sample_call(p_t, s_t, logits_c, norm_g_c, layer):
    bsz = p_t.shape[1]
    return pl.pallas_call(
        functools.partial(_hgrn_sample_kernel, layer),
        grid=(N_HEADS,),
        in_specs=[_full(p_t.shape), _state_block(bsz), _head_block(2), _head_block(1)],
        out_specs=[_head_block(bsz), _state_block(bsz)],
        out_shape=[jax.ShapeDtypeStruct((MIX_W, bsz), F32), jax.ShapeDtypeStruct(s_t.shape, F32)],
        scratch_shapes=[pltpu.VMEM((HEAD_DIM, bsz), F32)] * 3,
        compiler_params=_params("parallel"),
        name="hgrn_sample",
    )(p_t, s_t, logits_c, norm_g_c)


def _ret_sample_kernel(pos0, p_ref, s_ref, inv_ref, y_ref, so_ref, q_scr, k_scr):
    h = pl.program_id(0)
    W = MIX_W
    ang = jnp.float32(pos0) * inv_ref[...]
    cosv, sinv = jnp.cos(ang), jnp.sin(ang)

    def rot(z):
        z1, z2 = z[0:HALF_DIM, :], z[HALF_DIM:HEAD_DIM, :]
        return jnp.concatenate([z1 * cosv - z2 * sinv, z1 * sinv + z2 * cosv], axis=0)

    q_scr[...] = rot(_head_rows(p_ref, 0, h))
    k_scr[...] = rot(_head_rows(p_ref, W, h)) * HEAD_DIM ** -0.5
    v = _head_rows(p_ref, 2 * W, h)
    g = _head_rows(p_ref, 3 * W, h)
    gamma = jnp.float32(math.exp(RET_LOG_GAMMA[N_HEADS - 1]))
    for hh in range(N_HEADS - 2, -1, -1):
        gamma = jnp.where(h == hh, jnp.float32(math.exp(RET_LOG_GAMMA[hh])), gamma)
    o = _kv_state_step(s_ref, so_ref, q_scr, k_scr, None, v, gamma)
    y_ref[...] = _head_norm_cols(o, HEAD_NORM_EPS, False) * jax.nn.silu(g)


def _ret_sample_call(p_t, s_t, pos0):
    bsz = p_t.shape[1]
    inv = RET_THETA ** (-np.arange(HALF_DIM).astype(np.float64) / HALF_DIM)
    inv = jnp.asarray(inv.astype(np.float32).reshape(HALF_DIM, 1))
    return pl.pallas_call(
        functools.partial(_ret_sample_kernel, pos0),
        grid=(N_HEADS,),
        in_specs=[_full(p_t.shape), _state_block(bsz), _full(inv.shape)],
        out_specs=[_head_block(bsz), _state_block(bsz)],
        out_shape=[jax.ShapeDtypeStruct((MIX_W, bsz), F32), jax.ShapeDtypeStruct(s_t.shape, F32)],
        scratch_shapes=[pltpu.VMEM((HEAD_DIM, bsz), F32)] * 2,
        compiler_params=_params("parallel"),
        name="ret_sample",
    )(p_t, s_t, inv)


def _mlstm_sample_kernel(p_ref, pg_ref, conv_ref, c_ref, n_ref, m_ref, cw_ref, cb_ref, igb_ref, fgb_ref, ng_ref,
                         y_ref, co_ref, no_ref, mo_ref, q_scr, k_scr):
    h = pl.program_id(0)
    W = MIX_W

    def conv_rows(base):
        idx = pl.ds(pl.multiple_of(base + h * HEAD_DIM, HEAD_DIM), HEAD_DIM)
        cw = cw_ref[idx, :]
        acc = cw[:, MLSTM_CONV - 1:MLSTM_CONV] * p_ref[idx, :]
        for j in range(MLSTM_CONV - 1):
            acc = acc + cw[:, j:j + 1] * conv_ref[j, idx, :]
        return jax.nn.silu(acc + cb_ref[idx, :])

    q = conv_rows(0)
    k = conv_rows(W) * HEAD_DIM ** -0.5
    v = _head_rows(p_ref, 2 * W, h)
    og = _head_rows(p_ref, 3 * W, h)
    ig = pg_ref[pl.ds(h, 1), :] + igb_ref[pl.ds(h, 1), :]
    lf = jax.nn.log_sigmoid(pg_ref[pl.ds(N_HEADS + h, 1), :] + fgb_ref[pl.ds(h, 1), :])
    m_prev = m_ref[pl.ds(h, 1), :]
    n0 = n_ref[...]
    m_inter = lf + m_prev
    m_t = jnp.maximum(m_inter, ig)
    sc = jnp.sum(q * k, axis=0, keepdims=True) * jnp.exp(ig - m_t)
    w_inter = jnp.exp(m_inter - m_t)
    a_prev = jnp.exp(lf + m_prev - m_t)
    a_s = jnp.exp(ig + lf - lf - m_t)
    q_scr[...] = q
    k_scr[...] = k * a_s

    def body(i, qc):
        c0 = c_ref[i]
        co_ref[i] = c0 * a_prev + k_scr[pl.ds(i, 1), :] * v
        return qc + q_scr[pl.ds(i, 1), :] * c0

    qc = lax.fori_loop(0, HEAD_DIM, body, jnp.zeros_like(v))
    num = sc * v + w_inter * qc
    den = sc + w_inter * jnp.sum(q * n0, axis=0, keepdims=True)
    hh = num / jnp.maximum(jnp.abs(den), jnp.exp(-m_t))
    no_ref[...] = n0 * a_prev + k * a_s
    mo_ref[pl.ds(h, 1), :] = m_t
    y_ref[...] = jax.nn.sigmoid(og) * (_head_norm_cols(hh, HEAD_NORM_EPS, True) * ng_ref[...])


def _mlstm_sample_call(p_t, pg_t, conv_t, c_t, n_t, m_t, prm):
    bsz = p_t.shape[1]
    nblk = pl.BlockSpec((None, HEAD_DIM, bsz), lambda h: (h, 0, 0))
    return pl.pallas_call(
        _mlstm_sample_kernel,
        grid=(N_HEADS,),
        in_specs=[_full(p_t.shape), _full(pg_t.shape), _full(conv_t.shape), _state_block(bsz), nblk,
                  _full(m_t.shape), _full((2 * MIX_W, MLSTM_CONV)), _full((2 * MIX_W, 1)),
                  _full((N_HEADS, 1)), _full((N_HEADS, 1)), _head_block(1)],
        out_specs=[_head_block(bsz), _state_block(bsz), nblk, _full(m_t.shape)],
        out_shape=[jax.ShapeDtypeStruct((MIX_W, bsz), F32), jax.ShapeDtypeStruct(c_t.shape, F32),
                   jax.ShapeDtypeStruct(n_t.shape, F32), jax.ShapeDtypeStruct(m_t.shape, F32)],
        scratch_shapes=[pltpu.VMEM((HEAD_DIM, bsz), F32)] * 2,
        compiler_params=_params("arbitrary"),
        name="mlstm_sample",
    )(p_t, pg_t, conv_t, c_t, n_t, m_t, prm["cw_t"], prm["cb_c"], prm["igb_c"], prm["fgb_c"], prm["ng_c"])


def _merge_kernel(x_ref, xn_ref, ya_ref, yb_ref, yc_ref, yd_ref, wg_ref, wb_ref, wo_ref, g2_ref, x1_ref, xn2_ref):
    gates = jax.nn.sigmoid(_dot(xn_ref[...], wg_ref[...]))
    merged = None
    for m, y_ref in enumerate((ya_ref, yb_ref, yc_ref, yd_ref)):
        br = _dot(y_ref[...].astype(BF16), wb_ref[m])
        term = gates[:, m * D_MODEL:(m + 1) * D_MODEL] * br
        merged = term if merged is None else merged + term
    x1 = x_ref[...] + _dot(merged.astype(BF16), wo_ref[...])
    x1_ref[...] = x1
    xn2 = x1 * lax.rsqrt(jnp.mean(x1 * x1, axis=-1, keepdims=True) + NORM_EPS) * g2_ref[...]
    xn2_ref[...] = xn2.astype(BF16)


def _merge_call(x2d, xn, ys, w_gate, w_branch, w_out, g2, tm):
    n = x2d.shape[0]
    row = lambda i: (i, 0)
    return pl.pallas_call(
        _merge_kernel,
        grid=(n // tm,),
        in_specs=[pl.BlockSpec((tm, D_MODEL), row)] * 2 + [pl.BlockSpec((tm, MIX_W), row)] * 4
        + [_full(w_gate.shape), _full(w_branch.shape), _full(w_out.shape), _full((1, D_MODEL))],
        out_specs=[pl.BlockSpec((tm, D_MODEL), row)] * 2,
        out_shape=[jax.ShapeDtypeStruct((n, D_MODEL), F32), jax.ShapeDtypeStruct((n, D_MODEL), BF16)],
        compiler_params=_params("parallel"),
        name="merge_out",
    )(x2d, xn, *ys, w_gate, w_branch, w_out, g2.reshape(1, D_MODEL))


def _top_values(s, count):
    vals = []
    for _ in range(count):
        mx = jnp.max(s, axis=0, keepdims=True)
        vals.append(mx)
        s = jnp.where(s == mx, NEG_BIG, s)
    return vals


def _peer_score_kernel(xn_ref, wq_ref, keys_ref, s1_ref, cf_ref, s2_ref, e2_ref, tau_ref):
    q_t = _dot_nt(wq_ref[...], xn_ref[...])
    tm = q_t.shape[1]
    taus = []
    for h in range(PEER_HEADS):
        base = h * 2 * PEER_QHALF
        s1 = _dot(keys_ref[h, 0], q_t[base:base + PEER_QHALF, :], HI)
        s2 = _dot(keys_ref[h, 1], q_t[base + PEER_QHALF:base + 2 * PEER_QHALF, :], HI)
        v1 = _top_values(s1, PEER_TOPK)
        v2 = _top_values(s2, PEER_TOPK)
        rows = [v1[i] + v2[j] for i in range(PEER_TOPK) for j in range(PEER_TOPK // (i + 1))]
        pad = (-len(rows)) % 8
        cand = jnp.concatenate(rows + [jnp.full((pad, tm), NEG_BIG, F32)], axis=0)
        tau = _top_values(cand, PEER_TOPK)[-1]
        top = v1[0] + v2[0]
        z = jnp.sum(jnp.where(cand >= tau, jnp.exp(cand - top), 0.0), axis=0, keepdims=True)
        s1_ref[h] = s1
        s2_ref[h] = s2
        e2_ref[h] = jnp.exp(s2 - v2[0])
        cf_ref[h] = jnp.exp(s1 - v1[0]) / z
        taus.append(tau)
    tau_ref[...] = jnp.concatenate(taus, axis=0)


def _peer_score_call(xn2, wq_t, keys, tm):
    n = xn2.shape[0]
    blk = pl.BlockSpec((PEER_HEADS, PEER_N_KEYS, tm), lambda i: (0, 0, i))
    big = jax.ShapeDtypeStruct((PEER_HEADS, PEER_N_KEYS, n), F32)
    return pl.pallas_call(
        _peer_score_kernel,
        grid=(n // tm,),
        in_specs=[pl.BlockSpec((tm, D_MODEL), lambda i: (i, 0)), _full(wq_t.shape), _full(keys.shape)],
        out_specs=[blk] * 4 + [pl.BlockSpec((PEER_HEADS, tm), lambda i: (0, i))],
        out_shape=[big] * 4 + [jax.ShapeDtypeStruct((PEER_HEADS, n), F32)],
        compiler_params=_params("parallel"),
        name="peer_scores",
    )(xn2, wq_t, keys)


def _peer_expert_kernel(xn_ref, x1_ref, u_ref, vt_ref, s1_ref, cf_ref, s2_ref, e2_ref, tau_ref, o_ref, acc_ref):
    j = pl.program_id(1)

    @pl.when(j == 0)
    def _():
        acc_ref[...] = jnp.zeros_like(acc_ref)

    hid = _dot_nt(u_ref[...], xn_ref[...])
    act = 0.5 * hid * (1.0 + lax.erf(hid * INV_SQRT2))
    te = hid.shape[0]
    per_tile = te // PEER_N_KEYS
    pieces = []
    for al in range(per_tile):
        a = j * per_tile + al
        gate = jnp.zeros((PEER_N_KEYS, hid.shape[1]), F32)
        for h in range(PEER_HEADS):
            t = s1_ref[h, pl.ds(a, 1), :] + s2_ref[h]
            gate = gate + jnp.where(t >= tau_ref[h:h + 1, :], e2_ref[h] * cf_ref[h, pl.ds(a, 1), :], 0.0)
        pieces.append(act[al * PEER_N_KEYS:(al + 1) * PEER_N_KEYS, :] * gate)
    wmat = jnp.concatenate(pieces, axis=0).astype(BF16)
    acc_ref[...] += _dot(vt_ref[...], wmat)

    @pl.when(j == pl.num_programs(1) - 1)
    def _():
        o_ref[...] = x1_ref[...] + acc_ref[...].T


def _peer_expert_call(xn2, x1, u, v_t, stats, tm, te):
    n = xn2.shape[0]
    n_exp = u.shape[0]
    s1, cf, s2, e2, tau = stats
    tok = pl.BlockSpec((tm, D_MODEL), lambda i, j: (i, 0))
    blk = pl.BlockSpec((PEER_HEADS, PEER_N_KEYS, tm), lambda i, j: (0, 0, i))
    return pl.pallas_call(
        _peer_expert_kernel,
        grid=(n // tm, n_exp // te),
        in_specs=[tok, tok,
                  pl.BlockSpec((te, D_MODEL), lambda i, j: (j, 0)),
                  pl.BlockSpec((D_MODEL, te), lambda i, j: (0, j)),
                  blk, blk, blk, blk,
                  pl.BlockSpec((PEER_HEADS, tm), lambda i, j: (0, i))],
        out_specs=tok,
        out_shape=jax.ShapeDtypeStruct((n, D_MODEL), F32),
        scratch_shapes=[pltpu.VMEM((D_MODEL, tm), F32)],
        compiler_params=_params("parallel", "arbitrary"),
        name="peer_experts",
    )(xn2, x1, u, v_t, s1, cf, s2, e2, tau)


def _final_norm_kernel(x_ref, g_ref, o_ref):
    x = x_ref[...]
    o_ref[...] = x * lax.rsqrt(jnp.mean(x * x, axis=-1, keepdims=True) + NORM_EPS) * g_ref[...]


def _final_norm_call(x2d, g, tm):
    n = x2d.shape[0]
    row = lambda i: (i, 0)
    return pl.pallas_call(
        _final_norm_kernel,
        grid=(n // tm,),
        in_specs=[pl.BlockSpec((tm, D_MODEL), row), _full((1, D_MODEL))],
        out_specs=pl.BlockSpec((tm, D_MODEL), row),
        out_shape=jax.ShapeDtypeStruct((n, D_MODEL), F32),
        compiler_params=_params("parallel"),
        name="final_norm",
    )(x2d, g.reshape(1, D_MODEL))


def _layer_weights(l, w_in, rw, hg, ml, w_branch, w_out, ln1_g, ln2_g, peer_w_q, peer_keys, peer_u, peer_v):
    o3 = RWKV_COLS + 2 * 4 * MIX_W
    o4 = o3 + MLSTM_COLS
    gcol = o3 + 3 * MIX_W
    w = w_in[l]
    w_main = jnp.concatenate([w[:, 0:gcol], w[:, gcol + 2 * N_HEADS:o4]], axis=1).astype(BF16)
    w_mlg = jnp.pad(w[:, gcol:gcol + 2 * N_HEADS], ((0, 0), (0, LANES - 2 * N_HEADS)))
    row = lambda a: a[l].reshape(1, -1)
    col = lambda a: a[l].reshape(-1, 1)
    rwkv = dict(mu=row(rw["mu"]), w0=row(rw["w0"]), wup=rw["w_up"][l], a0=row(rw["a0"]), aup=rw["a_up"][l],
                gup=rw["g_up"][l], kk=row(rw["k_k"]), ka=row(rw["k_a"]), rk=row(rw["r_k"]),
                lng=row(rw["ln_g"]), lnb=row(rw["ln_b"]),
                mu_c=col(rw["mu"]), w0_c=col(rw["w0"]), wup_t=rw["w_up"][l].T, a0_c=col(rw["a0"]),
                aup_t=rw["a_up"][l].T, gup_t=rw["g_up"][l].T, kk_c=col(rw["k_k"]), ka_c=col(rw["k_a"]),
                rk_c=col(rw["r_k"]), lng_c=col(rw["ln_g"]), lnb_c=col(rw["ln_b"]))
    gate_b = jnp.pad(jnp.concatenate([ml["ig_b"][l], ml["fg_b"][l]]), (0, LANES - 2 * N_HEADS)).reshape(1, LANES)
    mlstm = dict(cw=ml["conv_w"][l], cb=row(ml["conv_b"]), gate_b=gate_b, ng=row(ml["norm_g"]),
                 cw_t=ml["conv_w"][l].T, cb_c=col(ml["conv_b"]), igb_c=col(ml["ig_b"]), fgb_c=col(ml["fg_b"]),
                 ng_c=col(ml["norm_g"]))
    return dict(
        ln1=ln1_g[l], ln2=ln2_g[l], w_main=w_main, w_mlg=w_mlg, w_gate=w[:, o4:].astype(BF16),
        w_branch=w_branch[l].astype(BF16), w_out=w_out[l].astype(BF16),
        rwkv=rwkv, hgrn_logits=hg["lb_logits"], hgrn_logits_c=hg["lb_logits"].T,
        hgrn_ng=row(hg["norm_g"]), hgrn_ng_c=col(hg["norm_g"]), mlstm=mlstm,
        wq_t=peer_w_q[l].T.astype(BF16), keys=peer_keys[l], u=peer_u[l].astype(BF16),
        v_t=peer_v[l].T.astype(BF16))


def _lanes_last(a):
    return jnp.moveaxis(a, 0, -1)


def _prompt_mixers(proj, bsz, lw, layer):
    _, pr, ph, pt, pm, pg = proj
    ya, shift, s_a = _rwkv_prompt_call(pr, bsz, lw["rwkv"])
    yb, s_b = _hgrn_prompt_call(ph, bsz, lw["hgrn_logits"], lw["hgrn_ng"], layer)
    yc, s_c = _ret_prompt_call(pt, bsz, 0)
    m = lw["mlstm"]
    yd, conv, c_d, n_d, m_d = _mlstm_prompt_call(pm, pg, bsz, m["cw"], m["cb"], m["gate_b"], m["ng"])
    states = (shift[:, 0, :], s_a, s_b, s_c, conv[:, 8 - (MLSTM_CONV - 1):, :], c_d, n_d, m_d[:, 0, :N_HEADS])
    return (ya, yb, yc, yd), states


def _sample_mixers(proj, st, lw, layer, pos0):
    _, pr, ph, pt, pm, pg = proj
    shift, s_a, s_b, s_c, conv, c_d, n_d, m_d = st
    to_t = lambda a: jnp.moveaxis(a, 0, -1)
    from_t = lambda a: jnp.moveaxis(a, -1, 0)
    ya, s_a2 = _rwkv_sample_call(pr.T, shift.T, to_t(s_a), lw["rwkv"])
    yb, s_b2 = _hgrn_sample_call(ph.T, to_t(s_b), lw["hgrn_logits_c"], lw["hgrn_ng_c"], layer)
    yc, s_c2 = _ret_sample_call(pt.T, to_t(s_c), pos0)
    yd, c_d2, n_d2, m_d2 = _mlstm_sample_call(pm.T, pg.T, to_t(conv), to_t(c_d), to_t(n_d), m_d.T, lw["mlstm"])
    conv2 = jnp.concatenate([conv[:, 1:, :], pm[:, None, 0:2 * MIX_W]], axis=1)
    states = (pr, from_t(s_a2), from_t(s_b2), from_t(s_c2), conv2, from_t(c_d2), from_t(n_d2), m_d2.T)
    return (ya.T, yb.T, yc.T, yd.T), states


def _run_trunk(x, states, pos0, weights, lnf_g, prompt):
    bsz, t, _ = x.shape
    n = bsz * t
    x2d = x.reshape(n, D_MODEL)
    tm = 256 if n % 256 == 0 else n
    tm_peer = 512 if n % 512 == 0 else n
    new_states = []
    for layer, lw in enumerate(weights):
        proj = _proj_call(x2d, lw["ln1"], lw["w_main"], lw["w_mlg"], tm)
        if prompt:
            ys, st_new = _prompt_mixers(proj, bsz, lw, layer)
        else:
            ys, st_new = _sample_mixers(proj, tuple(s[layer] for s in states), lw, layer, pos0)
        new_states.append(st_new)
        x1, xn2 = _merge_call(x2d, proj[0], ys, lw["w_gate"], lw["w_branch"], lw["w_out"], lw["ln2"], tm)
        stats = _peer_score_call(xn2, lw["wq_t"], lw["keys"], min(tm, 128) if n < 256 else 256)
        x2d = _peer_expert_call(xn2, x1, lw["u"], lw["v_t"], stats, tm_peer, 512)
    y = _final_norm_call(x2d, lnf_g, tm).reshape(bsz, t, D_MODEL)
    stacked = tuple(jnp.stack([s[i] for s in new_states], axis=0) for i in range(8))
    return y, stacked


def kernel(x_prompt, x_sample, state_rwkv_shift, state_rwkv_S, state_hgrn_S, state_ret_S, state_mlstm_conv, state_mlstm_C, state_mlstm_n, state_mlstm_m, ln1_g, w_in, rwkv_mu, rwkv_w0, rwkv_w_up, rwkv_a0, rwkv_a_up, rwkv_g_up, rwkv_k_k, rwkv_k_a, rwkv_r_k, rwkv_ln_g, rwkv_ln_b, hgrn_lb_logits, hgrn_norm_g, mlstm_conv_w, mlstm_conv_b, mlstm_ig_b, mlstm_fg_b, mlstm_norm_g, w_branch, w_out, ln2_g, peer_w_q, peer_keys, peer_u, peer_v, lnf_g):
    depth = w_in.shape[0]
    rw = dict(mu=rwkv_mu, w0=rwkv_w0, w_up=rwkv_w_up, a0=rwkv_a0, a_up=rwkv_a_up, g_up=rwkv_g_up,
              k_k=rwkv_k_k, k_a=rwkv_k_a, r_k=rwkv_r_k, ln_g=rwkv_ln_g, ln_b=rwkv_ln_b)
    hg = dict(lb_logits=hgrn_lb_logits, norm_g=hgrn_norm_g)
    ml = dict(conv_w=mlstm_conv_w, conv_b=mlstm_conv_b, ig_b=mlstm_ig_b, fg_b=mlstm_fg_b, norm_g=mlstm_norm_g)
    weights = [_layer_weights(l, w_in, rw, hg, ml, w_branch, w_out, ln1_g, ln2_g, peer_w_q, peer_keys,
                              peer_u, peer_v) for l in range(depth)]
    past_len = 16384
    y_prompt, p_states = _run_trunk(x_prompt, None, 0, weights, lnf_g, True)
    sample_init = (state_rwkv_shift, state_rwkv_S, state_hgrn_S, state_ret_S, state_mlstm_conv,
                   state_mlstm_C, state_mlstm_n, state_mlstm_m)
    y_sample, s_states = _run_trunk(x_sample, sample_init, past_len, weights, lnf_g, False)
    return (y_prompt, y_sample) + p_states + s_states
```

```python
import functools
import math

import numpy as np
import jax
import jax.numpy as jnp
from jax import lax
from jax.experimental import pallas as pl
from jax.experimental.pallas import tpu as pltpu

F32 = jnp.float32
BF16 = jnp.bfloat16
HI = lax.Precision.HIGHEST

D_MODEL = 1024
N_HEADS = 4
HEAD_DIM = 64
HALF_DIM = HEAD_DIM // 2
MIX_W = N_HEADS * HEAD_DIM
RWKV_W_RANK = 64
RWKV_A_RANK = 64
RWKV_G_RANK = 128
RWKV_COLS = 3 * MIX_W + RWKV_W_RANK + RWKV_A_RANK + RWKV_G_RANK
MLSTM_COLS = 4 * MIX_W + 2 * N_HEADS
GATE_COLS = 4 * D_MODEL
MLSTM_CONV = 4
CHUNK = 64
SUB = 16
RET_THETA = 10000.0
PEER_HEADS = 8
PEER_N_KEYS = 128
PEER_TOPK = 16
PEER_QHALF = 64
NORM_EPS = 1e-6
RWKV_GN_EPS = 64e-5
HEAD_NORM_EPS = 1e-5
MASK_NEG = -1e30
LB_FLOOR = 1e-30
NEG_BIG = -1e30
INV_SQRT2 = 1.0 / math.sqrt(2.0)
RET_LOG_GAMMA = tuple(math.log(1.0 - 2.0 ** (-5.0 - h)) for h in range(N_HEADS))
LANES = 128
VMEM_LIMIT = 56 * 1024 * 1024


X3 = "bf16x3"


def _dot_dims(a, b, dims, prec):
    if prec != X3:
        return lax.dot_general(a, b, dims, precision=prec, preferred_element_type=F32)
    a_hi, b_hi = a.astype(BF16), b.astype(BF16)
    a_lo = (a - a_hi.astype(F32)).astype(BF16)
    b_lo = (b - b_hi.astype(F32)).astype(BF16)
    mm = lambda x, y: lax.dot_general(x, y, dims, preferred_element_type=F32)
    return mm(a_hi, b_hi) + (mm(a_hi, b_lo) + mm(a_lo, b_hi))


def _dot(a, b, prec=None):
    return _dot_dims(a, b, (((1,), (0,)), ((), ())), prec)


def _dot_nt(a, b, prec=None):
    return _dot_dims(a, b, (((1,), (1,)), ((), ())), prec)


def _dot_tn(a, b, prec=None):
    return _dot_dims(a, b, (((0,), (0,)), ((), ())), prec)


def _params(*sem):
    return pltpu.CompilerParams(dimension_semantics=sem, vmem_limit_bytes=VMEM_LIMIT)


def _full(shape):
    nd = len(shape)
    return pl.BlockSpec(shape, lambda *_: (0,) * nd)


def _iota(shape, axis):
    return lax.broadcasted_iota(jnp.int32, shape, axis)


def _shift_rows(cur, prev, j):
    row = _iota(cur.shape, 0)
    return jnp.where(row >= j, pltpu.roll(cur, j, 0), pltpu.roll(prev, j, 0))


def _head_norm_rows(y, eps, center):
    if center:
        y = y - jnp.mean(y, axis=1, keepdims=True)
    return y * lax.rsqrt(jnp.mean(y * y, axis=1, keepdims=True) + eps)


def _same_head_mask(width):
    return (_iota((width, width), 0) // HEAD_DIM) == (_iota((width, width), 1) // HEAD_DIM)


def _per_head_row(values, width):
    head = _iota((1, width), 1) // HEAD_DIM
    row = jnp.full((1, width), values[-1], F32)
    for h in range(len(values) - 2, -1, -1):
        row = jnp.where(head == h, values[h], row)
    return row


def _block_diag(x, same_head):
    reps = same_head.shape[0] // x.shape[0]
    return jnp.where(same_head, jnp.concatenate([x] * reps, axis=0), 0.0)


def _head_mean_lanes(y, same_head):
    e = jnp.where(same_head, 1.0 / HEAD_DIM, 0.0).astype(BF16)
    hi = y.astype(BF16)
    lo = (y - hi.astype(F32)).astype(BF16)
    return _dot(hi, e) + _dot(lo, e)


def _head_norm_lanes(y, same_head, eps, center):
    if center:
        y = y - _head_mean_lanes(y, same_head)
    return y * lax.rsqrt(_head_mean_lanes(y * y, same_head) + eps)


def _head_norm_cols(y, eps, center):
    if center:
        y = y - jnp.mean(y, axis=0, keepdims=True)
    return y * lax.rsqrt(jnp.mean(y * y, axis=0, keepdims=True) + eps)


def _proj_kernel(x_ref, g_ref, w_ref, wg_ref, xn_ref, pr_ref, ph_ref, pt_ref, pm_ref, pg_ref):
    x = x_ref[...]
    xn = x * lax.rsqrt(jnp.mean(x * x, axis=-1, keepdims=True) + NORM_EPS) * g_ref[...]
    xb = xn.astype(BF16)
    xn_ref[...] = xb
    p = _dot(xb, w_ref[...])
    pr_ref[...] = p[:, 0:1024]
    ph_ref[...] = p[:, 1024:2048]
    pt_ref[...] = p[:, 2048:3072]
    pm_ref[...] = p[:, 3072:4096]
    pg_ref[...] = _dot(xn, wg_ref[...], HI)


def _proj_call(x2d, g, w_main, w_mlg, tm):
    n = x2d.shape[0]
    row = lambda i: (i, 0)
    out = lambda c, dt: jax.ShapeDtypeStruct((n, c), dt)
    return pl.pallas_call(
        _proj_kernel,
        grid=(n // tm,),
        in_specs=[pl.BlockSpec((tm, D_MODEL), row), _full((1, D_MODEL)),
                  _full((D_MODEL, 4096)), _full((D_MODEL, LANES))],
        out_specs=[pl.BlockSpec((tm, D_MODEL), row)] * 5 + [pl.BlockSpec((tm, LANES), row)],
        out_shape=[out(D_MODEL, BF16)] + [out(D_MODEL, F32)] * 4 + [out(LANES, F32)],
        compiler_params=_params("parallel"),
        name="norm_proj",
    )(x2d, g.reshape(1, D_MODEL), w_main, w_mlg)


def _neumann_inverse(nmat, eye, same_head):
    t = eye - nmat
    pw = nmat
    for _ in range(5):
        pw = _dot(pw, _block_diag(pw, same_head), X3)
        t = t + _dot(t, _block_diag(pw, same_head), X3)
    return t


def _cumsum_rows(x):
    row = _iota(x.shape, 0)
    shift = 1
    while shift < x.shape[0]:
        x = x + jnp.where(row >= shift, pltpu.roll(x, shift, 0), 0.0)
        shift *= 2
    return x


def _rwkv_prompt_kernel(p_ref, mu_ref, w0_ref, wup_ref, a0_ref, aup_ref, gup_ref, kk_ref, ka_ref, rk_ref,
                        lng_ref, lnb_ref, y_ref, shift_ref, s_ref, prow_ref, sbd_ref):
    p = p_ref[...]
    L = p.shape[0]
    row = _iota(p.shape, 0)
    prev = jnp.where(row >= 1, pltpu.roll(p, 1, 0), jnp.broadcast_to(prow_ref[...], p.shape))
    last = p[L - 1:L, :]
    prow_ref[...] = last
    shift_ref[...] = last
    xs = p + mu_ref[...] * (prev - p)
    W = MIX_W
    r, k, v = xs[:, 0:W], xs[:, W:2 * W], xs[:, 2 * W:3 * W]
    o = 3 * W
    xw = xs[:, o:o + RWKV_W_RANK]
    o += RWKV_W_RANK
    xa = xs[:, o:o + RWKV_A_RANK]
    o += RWKV_A_RANK
    xg = xs[:, o:o + RWKV_G_RANK]
    w = w0_ref[...] + _dot(jnp.tanh(xw), wup_ref[...], X3)
    logw = -jnp.exp(-jax.nn.softplus(-w) - 0.5)
    a = jax.nn.sigmoid(a0_ref[...] + _dot(xa, aup_ref[...], X3))
    g = _dot(jax.nn.sigmoid(xg), gup_ref[...], X3)
    same_head = _same_head_mask(W)
    kk = k * kk_ref[...]
    kk = kk / jnp.maximum(jnp.sqrt(HEAD_DIM * _head_mean_lanes(kk * kk, same_head)), 1e-12)
    kq = k * (1.0 + (a - 1.0) * ka_ref[...])
    bb = kk * a

    cum = _cumsum_rows(logw)
    cum_l = cum[L - 1:L, :]
    e_neg = jnp.exp(-cum)
    e_tail = jnp.exp(cum_l - cum)
    lhs = jnp.concatenate([kk * jnp.exp(cum - logw), r * jnp.exp(cum)], axis=0)
    s0 = sbd_ref[...]
    gk = _dot_nt(lhs, _block_diag(kq * e_neg, same_head), X3)
    gb = _dot_nt(lhs, _block_diag(bb * e_neg, same_head), X3)
    qs = _dot_nt(lhs, s0, X3)
    t_idx = _iota((L, W), 0)
    s_idx = _iota((L, W), 1) % HEAD_DIM
    strict = t_idx > s_idx
    incl = t_idx >= s_idx
    n_k = jnp.where(strict, gk[0:L], 0.0)
    n_b = jnp.where(strict, gb[0:L], 0.0)
    m_k = jnp.where(incl, gk[L:2 * L], 0.0)
    m_b = jnp.where(incl, gb[L:2 * L], 0.0)
    tinv = _neumann_inverse(n_b, (t_idx == s_idx).astype(F32), same_head)
    kv = _dot(jnp.concatenate([n_k, m_k], axis=0), _block_diag(v, same_head), X3)
    u = _dot(tinv, _block_diag(qs[0:L] + kv[0:L], same_head), X3)
    y = qs[L:2 * L] + kv[L:2 * L] - _dot(m_b, _block_diag(u, same_head), X3)
    upd = _dot_tn(jnp.concatenate([v, -u], axis=0),
                  jnp.concatenate([kq * e_tail, bb * e_tail], axis=0), X3)
    s_new = s0 * jnp.exp(cum_l) + jnp.where(same_head, upd, 0.0)
    sbd_ref[...] = s_new
    for h in range(N_HEADS):
        sl = slice(h * HEAD_DIM, (h + 1) * HEAD_DIM)
        s_ref[h] = s_new[sl, sl]
    y = _head_norm_lanes(y, same_head, RWKV_GN_EPS, True) * lng_ref[...] + lnb_ref[...]
    y = y + HEAD_DIM * _head_mean_lanes(r * kq * rk_ref[...], same_head) * v
    y_ref[...] = y * g


def _per_sequence(body, batched, carried):
    def kernel(*refs):
        @pl.when(pl.program_id(1) == 0)
        def _():
            for i in carried:
                refs[i][...] = jnp.zeros_like(refs[i])

        group = next(r.shape[0] for r, b in zip(refs, batched) if b)
        for gi in range(group):
            body(*[r.at[gi] if b else r for r, b in zip(refs, batched)])

    return kernel


def _seq_group(bsz):
    return 2 if bsz % 2 == 0 else 1


def _chunk_specs(grp, width):
    return pl.BlockSpec((grp, CHUNK, width), lambda b, c: (b, c, 0))


def _per_seq_spec(grp, *dims):
    return pl.BlockSpec((grp,) + dims, lambda b, c: (b,) + (0,) * len(dims))


def _state_spec(grp):
    return _per_seq_spec(grp, N_HEADS, HEAD_DIM, HEAD_DIM)


def _rwkv_prompt_call(p, bsz, prm):
    n = p.shape[0]
    t = n // bsz
    grp = _seq_group(bsz)
    small = [prm[k] for k in ("mu", "w0", "wup", "a0", "aup", "gup", "kk", "ka", "rk", "lng", "lnb")]
    y, shift, s = pl.pallas_call(
        _per_sequence(_rwkv_prompt_kernel, (True,) + (False,) * len(small) + (True,) * 5,
                      carried=(len(small) + 4, len(small) + 5)),
        grid=(bsz // grp, t // CHUNK),
        in_specs=[_chunk_specs(grp, RWKV_COLS)] + [_full(a.shape) for a in small],
        out_specs=[_chunk_specs(grp, MIX_W), _per_seq_spec(grp, 1, RWKV_COLS), _state_spec(grp)],
        out_shape=[jax.ShapeDtypeStruct((bsz, t, MIX_W), F32),
                   jax.ShapeDtypeStruct((bsz, 1, RWKV_COLS), F32),
                   jax.ShapeDtypeStruct((bsz, N_HEADS, HEAD_DIM, HEAD_DIM), F32)],
        scratch_shapes=[pltpu.VMEM((grp, 1, RWKV_COLS), F32), pltpu.VMEM((grp, MIX_W, MIX_W), F32)],
        compiler_params=_params("parallel", "arbitrary"),
        name="rwkv_prompt",
    )(p.reshape(bsz, t, RWKV_COLS), *small)
    return y.reshape(n, MIX_W), shift, s


def _hgrn_lower_bound(logits, layer):
    l0, l1 = logits[0:1, :], logits[1:2, :]
    m = jnp.maximum(l0, l1)
    e0, e1 = jnp.exp(l0 - m), jnp.exp(l1 - m)
    z = e0 + e1
    lb0, lb1 = e0 / z, e1 / z
    return (lb0 - lb0) if layer == 0 else ((lb0 + lb1) - lb0)


def _hgrn_log_f(z, lb):
    return jnp.logaddexp(jnp.log(jnp.maximum(lb, LB_FLOOR)), jnp.log1p(-lb) + jax.nn.log_sigmoid(z))


def _hgrn_prompt_kernel(layer, p_ref, lg_ref, ng_ref, y_ref, s_ref, st_ref):
    p = p_ref[...]
    L = p.shape[0]
    W = MIX_W
    q = jax.nn.silu(p[:, 0:W])
    lb = _hgrn_lower_bound(lg_ref[...], layer)
    log_f = _hgrn_log_f(p[:, W:2 * W], lb)
    k = 1.0 - jnp.exp(log_f)
    v = p[:, 2 * W:3 * W]
    g = p[:, 3 * W:4 * W]
    same_head = _same_head_mask(W)
    row = _iota((L, W), 0) % SUB
    b = log_f
    shift = 1
    while shift < SUB:
        b = b + jnp.where(row >= shift, pltpu.roll(b, shift, 0), 0.0)
        shift *= 2
    qe = q * jnp.exp(b)
    intra = jnp.zeros((L, W), F32)
    for lag in range(SUB):
        valid = row >= lag
        kr = k if lag == 0 else pltpu.roll(k, lag, 0)
        br = b if lag == 0 else pltpu.roll(b, lag, 0)
        vr = v if lag == 0 else pltpu.roll(v, lag, 0)
        term = jnp.where(valid, q * kr * jnp.exp(jnp.where(valid, b - br, 0.0)), 0.0)
        intra = intra + HEAD_DIM * _head_mean_lanes(term, same_head) * vr
    st = st_ref[...]
    pieces = []
    for j in range(L // SUB):
        rs = slice(j * SUB, (j + 1) * SUB)
        pieces.append(_dot_nt(qe[rs], st, X3))
        b_l = b[(j + 1) * SUB - 1:(j + 1) * SUB]
        kd = k[rs] * jnp.exp(b_l - b[rs])
        st = st * jnp.exp(b_l) + jnp.where(same_head, _dot_tn(v[rs], kd, X3), 0.0)
    st_ref[...] = st
    for h in range(N_HEADS):
        sl = slice(h * HEAD_DIM, (h + 1) * HEAD_DIM)
        s_ref[h] = st[sl, sl].T
    o = jnp.concatenate(pieces, axis=0) + intra
    y_ref[...] = _head_norm_lanes(o, same_head, HEAD_NORM_EPS, False) * ng_ref[...] * jax.nn.silu(g)


def _hgrn_prompt_call(p, bsz, logits, norm_g, layer):
    n = p.shape[0]
    t = n // bsz
    grp = _seq_group(bsz)
    y, s = pl.pallas_call(
        _per_sequence(functools.partial(_hgrn_prompt_kernel, layer), (True, False, False, True, True, True),
                      carried=(5,)),
        grid=(bsz // grp, t // CHUNK),
        in_specs=[_chunk_specs(grp, 4 * MIX_W), _full(logits.shape), _full(norm_g.shape)],
        out_specs=[_chunk_specs(grp, MIX_W), _state_spec(grp)],
        out_shape=[jax.ShapeDtypeStruct((bsz, t, MIX_W), F32),
                   jax.ShapeDtypeStruct((bsz, N_HEADS, HEAD_DIM, HEAD_DIM), F32)],
        scratch_shapes=[pltpu.VMEM((grp, MIX_W, MIX_W), F32)],
        compiler_params=_params("parallel", "arbitrary"),
        name="hgrn_prompt",
    )(p.reshape(bsz, t, 4 * MIX_W), logits, norm_g)
    return y.reshape(n, MIX_W), s


def _ret_prompt_kernel(pos0, p_ref, inv_ref, y_ref, s_ref, sbd_ref):
    c = pl.program_id(1)
    p = p_ref[...]
    L = p.shape[0]
    W = MIX_W
    pos = (pos0 + c * L + _iota((L, W), 0)).astype(F32)
    ang = pos * inv_ref[...]
    cosv, sinv = jnp.cos(ang), jnp.sin(ang)
    first = (_iota((L, W), 1) % HEAD_DIM) < HALF_DIM

    def rot(z):
        partner = jnp.where(first, -pltpu.roll(z, W - HALF_DIM, 1), pltpu.roll(z, HALF_DIM, 1))
        return z * cosv + partner * sinv

    q = rot(p[:, 0:W])
    k = rot(p[:, W:2 * W]) * HEAD_DIM ** -0.5
    v = p[:, 2 * W:3 * W]
    g = p[:, 3 * W:4 * W]
    same_head = _same_head_mask(W)
    lg = _per_head_row(RET_LOG_GAMMA, W)
    t_row = _iota((L, W), 0).astype(F32)
    rel = t_row - (_iota((L, W), 1) % HEAD_DIM).astype(F32)
    dmat = jnp.where(rel >= 0, jnp.exp(lg * jnp.maximum(rel, 0.0)), 0.0)
    d_in = jnp.exp(lg * (t_row + 1.0))
    d_st = jnp.exp(lg * (L - 1.0 - t_row))
    s0 = sbd_ref[...]
    att = _dot_nt(q, _block_diag(k, same_head), X3) * dmat
    o = _dot(att, _block_diag(v, same_head), X3) + _dot(q, s0, X3) * d_in
    s_new = s0 * jnp.exp(lg * float(L)) + jnp.where(same_head, _dot_tn(k * d_st, v, X3), 0.0)
    sbd_ref[...] = s_new
    for h in range(N_HEADS):
        sl = slice(h * HEAD_DIM, (h + 1) * HEAD_DIM)
        s_ref[h] = s_new[sl, sl]
    y_ref[...] = _head_norm_lanes(o, same_head, HEAD_NORM_EPS, False) * jax.nn.silu(g)


def _rotary_inv_row():
    inv = RET_THETA ** (-(np.arange(MIX_W) % HALF_DIM).astype(np.float64) / HALF_DIM)
    return jnp.asarray(inv.astype(np.float32).reshape(1, MIX_W))


def _ret_prompt_call(p, bsz, pos0):
    n = p.shape[0]
    t = n // bsz
    grp = _seq_group(bsz)
    inv = _rotary_inv_row()
    y, s = pl.pallas_call(
        _per_sequence(functools.partial(_ret_prompt_kernel, pos0), (True, False, True, True, True), carried=(4,)),
        grid=(bsz // grp, t // CHUNK),
        in_specs=[_chunk_specs(grp, 4 * MIX_W), _full(inv.shape)],
        out_specs=[_chunk_specs(grp, MIX_W), _state_spec(grp)],
        out_shape=[jax.ShapeDtypeStruct((bsz, t, MIX_W), F32),
                   jax.ShapeDtypeStruct((bsz, N_HEADS, HEAD_DIM, HEAD_DIM), F32)],
        scratch_shapes=[pltpu.VMEM((grp, MIX_W, MIX_W), F32)],
        compiler_params=_params("parallel", "arbitrary"),
        name="ret_prompt",
    )(p.reshape(bsz, t, 4 * MIX_W), inv)
    return y.reshape(n, MIX_W), s


def _mlstm_prompt_kernel(p_ref, pg_ref, cw_ref, cb_ref, gb_ref, ng_ref,
                         y_ref, conv_ref, c_ref, n_ref, m_ref, prev_ref, cbd_ref, nrow_ref, mrow_ref):
    p = p_ref[...]
    L = p.shape[0]
    W = MIX_W
    raw = p[:, 0:2 * W]
    prev = prev_ref[...]
    acc = cw_ref[MLSTM_CONV - 1:MLSTM_CONV, :] * raw
    for j in range(1, MLSTM_CONV):
        acc = acc + cw_ref[MLSTM_CONV - 1 - j:MLSTM_CONV - j, :] * _shift_rows(raw, prev, j)
    prev_ref[...] = raw
    conv_ref[...] = raw[L - 8:L, :]
    qk = jax.nn.silu(acc + cb_ref[...])
    q_all = qk[:, 0:W]
    k_all = qk[:, W:2 * W] * HEAD_DIM ** -0.5
    v_all = p[:, 2 * W:3 * W]
    og = p[:, 3 * W:4 * W]

    gb = pg_ref[...] + gb_ref[...]
    bcum = _cumsum_rows(jax.nn.log_sigmoid(gb))
    gb_t = gb.T
    bcum_t = bcum.T
    spread = lambda a, base: jnp.concatenate(
        [jnp.broadcast_to(a[:, base + h:base + h + 1], (L, HEAD_DIM)) for h in range(N_HEADS)], axis=1)
    rows = lambda a_t, base: jnp.concatenate([a_t[base + h:base + h + 1, :] for h in range(N_HEADS)], axis=1)
    ig_col, b_col = spread(gb, 0), spread(bcum, N_HEADS)
    ig_row, b_row = rows(gb_t, 0), rows(bcum_t, N_HEADS)
    same_head = _same_head_mask(W)
    causal = _iota((L, W), 0) >= (_iota((L, W), 1) % HEAD_DIM)
    m_prev = mrow_ref[...]
    log_d = jnp.where(causal, b_col - b_row + ig_row, MASK_NEG)
    m_inter = b_col + m_prev
    row_max = jnp.concatenate(
        [jnp.broadcast_to(jnp.max(log_d[:, h * HEAD_DIM:(h + 1) * HEAD_DIM], axis=1, keepdims=True), (L, HEAD_DIM))
         for h in range(N_HEADS)], axis=1)
    m_t = jnp.maximum(m_inter, row_max)
    sc = _dot_nt(q_all, _block_diag(k_all, same_head), X3) * jnp.exp(log_d - m_t)
    w_inter = jnp.exp(m_inter - m_t)
    c0 = cbd_ref[...]
    n0 = nrow_ref[...]
    num = _dot(sc, _block_diag(v_all, same_head), X3) + w_inter * _dot(q_all, c0, X3)
    den = HEAD_DIM * (_head_mean_lanes(sc, same_head) + w_inter * _head_mean_lanes(q_all * n0, same_head))
    hh = num / jnp.maximum(jnp.abs(den), jnp.exp(-m_t))
    m_l = m_t[L - 1:L, :]
    b_l = b_col[L - 1:L, :]
    a_prev = jnp.exp(b_l + m_prev - m_l)
    ks = k_all * jnp.exp(ig_col + b_l - b_col - m_l)
    c_new = c0 * a_prev + jnp.where(same_head, _dot_tn(ks, v_all, X3), 0.0)
    n_new = n0 * a_prev + jnp.sum(ks, axis=0, keepdims=True)
    cbd_ref[...] = c_new
    nrow_ref[...] = n_new
    mrow_ref[...] = m_l
    m_out = jnp.zeros(m_ref.shape, F32)
    m_first = jnp.logical_and(_iota(m_ref.shape, 0) == 0, _iota(m_ref.shape, 1) < N_HEADS)
    for h in range(N_HEADS):
        sl = slice(h * HEAD_DIM, (h + 1) * HEAD_DIM)
        c_ref[h] = c_new[sl, sl]
        n_ref[h:h + 1, :] = n_new[:, sl]
        m_out = jnp.where(jnp.logical_and(m_first, _iota(m_ref.shape, 1) == h), m_l[:, h * HEAD_DIM:h * HEAD_DIM + 1], m_out)
    m_ref[...] = m_out
    hn = _head_norm_lanes(hh, same_head, HEAD_NORM_EPS, True) * ng_ref[...]
    y_ref[...] = jax.nn.sigmoid(og) * hn


def _mlstm_prompt_call(p, pg, bsz, conv_w, conv_b, gate_b, norm_g):
    n = p.shape[0]
    t = n // bsz
    grp = _seq_group(bsz)
    outs = pl.pallas_call(
        _per_sequence(_mlstm_prompt_kernel, (True, True) + (False,) * 4 + (True,) * 9, carried=(11, 12, 13, 14)),
        grid=(bsz // grp, t // CHUNK),
        in_specs=[_chunk_specs(grp, 4 * MIX_W), _chunk_specs(grp, LANES), _full(conv_w.shape),
                  _full(conv_b.shape), _full(gate_b.shape), _full(norm_g.shape)],
        out_specs=[_chunk_specs(grp, MIX_W),
                   _per_seq_spec(grp, 8, 2 * MIX_W),
                   _state_spec(grp),
                   _per_seq_spec(grp, N_HEADS, HEAD_DIM),
                   _per_seq_spec(grp, 8, LANES)],
        out_shape=[jax.ShapeDtypeStruct((bsz, t, MIX_W), F32),
                   jax.ShapeDtypeStruct((bsz, 8, 2 * MIX_W), F32),
                   jax.ShapeDtypeStruct((bsz, N_HEADS, HEAD_DIM, HEAD_DIM), F32),
                   jax.ShapeDtypeStruct((bsz, N_HEADS, HEAD_DIM), F32),
                   jax.ShapeDtypeStruct((bsz, 8, LANES), F32)],
        scratch_shapes=[pltpu.VMEM((grp, CHUNK, 2 * MIX_W), F32), pltpu.VMEM((grp, MIX_W, MIX_W), F32),
                        pltpu.VMEM((grp, 1, MIX_W), F32), pltpu.VMEM((grp, 1, MIX_W), F32)],
        compiler_params=_params("parallel", "arbitrary"),
        name="mlstm_prompt",
    )(p.reshape(bsz, t, 4 * MIX_W), pg.reshape(bsz, t, LANES), conv_w, conv_b, gate_b, norm_g)
    return (outs[0].reshape(n, MIX_W),) + tuple(outs[1:])


def _head_rows(ref, base, h):
    return ref[pl.ds(pl.multiple_of(base + h * HEAD_DIM, HEAD_DIM), HEAD_DIM), :]


def _rwkv_sample_kernel(p_ref, sh_ref, s_ref, mu_ref, w0_ref, wup_ref, a0_ref, aup_ref, gup_ref, kk_ref, ka_ref,
                        rk_ref, lng_ref, lnb_ref, y_ref, so_ref, v_scr, y_scr):
    h = pl.program_id(0)
    W = MIX_W

    def xs_rows(lo, size, dyn):
        idx = pl.ds(pl.multiple_of(lo + h * HEAD_DIM, HEAD_DIM), size) if dyn else pl.ds(lo, size)
        pv = p_ref[idx, :]
        return pv + mu_ref[idx, :] * (sh_ref[idx, :] - pv)

    r = xs_rows(0, HEAD_DIM, True)
    k = xs_rows(W, HEAD_DIM, True)
    v = xs_rows(2 * W, HEAD_DIM, True)
    xw = xs_rows(3 * W, RWKV_W_RANK, False)
    xa = xs_rows(3 * W + RWKV_W_RANK, RWKV_A_RANK, False)
    xg = xs_rows(3 * W + RWKV_W_RANK + RWKV_A_RANK, RWKV_G_RANK, False)
    w = w0_ref[...] + _dot(wup_ref[...], jnp.tanh(xw), HI)
    wd = jnp.exp(-jnp.exp(-jax.nn.softplus(-w) - 0.5))
    a = jax.nn.sigmoid(a0_ref[...] + _dot(aup_ref[...], xa, HI))
    g = _dot(gup_ref[...], jax.nn.sigmoid(xg), HI)
    kk = k * kk_ref[...]
    kk = kk / jnp.maximum(jnp.sqrt(jnp.sum(kk * kk, axis=0, keepdims=True)), 1e-12)
    kq = k * (1.0 + (a - 1.0) * ka_ref[...])
    bb = kk * a
    v_scr[...] = v

    def body(i, carry):
        sv = s_ref[i]
        sa = -jnp.sum(sv * kk, axis=0, keepdims=True)
        snew = sv * wd + sa * bb + v_scr[pl.ds(i, 1), :] * kq
        so_ref[i] = snew
        y_scr[pl.ds(i, 1), :] = jnp.sum(snew * r, axis=0, keepdims=True)
        return carry

    lax.fori_loop(0, HEAD_DIM, body, 0)
    y = _head_norm_cols(y_scr[...], RWKV_GN_EPS, True) * lng_ref[...] + lnb_ref[...]
    y = y + jnp.sum(r * kq * rk_ref[...], axis=0, keepdims=True) * v
    y_ref[...] = y * g


def _head_block(cols):
    return pl.BlockSpec((HEAD_DIM, cols), lambda h: (h, 0))


def _state_block(bsz):
    return pl.BlockSpec((None, HEAD_DIM, HEAD_DIM, bsz), lambda h: (h, 0, 0, 0))


def _rwkv_sample_call(p_t, shift_t, s_t, prm):
    bsz = p_t.shape[1]
    return pl.pallas_call(
        _rwkv_sample_kernel,
        grid=(N_HEADS,),
        in_specs=[_full(p_t.shape), _full(shift_t.shape), _state_block(bsz), _full((RWKV_COLS, 1)),
                  _head_block(1), _head_block(RWKV_W_RANK), _head_block(1), _head_block(RWKV_A_RANK),
                  _head_block(RWKV_G_RANK), _head_block(1), _head_block(1), _head_block(1), _head_block(1),
                  _head_block(1)],
        out_specs=[_head_block(bsz), _state_block(bsz)],
        out_shape=[jax.ShapeDtypeStruct((MIX_W, bsz), F32), jax.ShapeDtypeStruct(s_t.shape, F32)],
        scratch_shapes=[pltpu.VMEM((HEAD_DIM, bsz), F32), pltpu.VMEM((HEAD_DIM, bsz), F32)],
        compiler_params=_params("parallel"),
        name="rwkv_sample",
    )(p_t, shift_t, s_t, prm["mu_c"], prm["w0_c"], prm["wup_t"], prm["a0_c"], prm["aup_t"], prm["gup_t"],
      prm["kk_c"], prm["ka_c"], prm["rk_c"], prm["lng_c"], prm["lnb_c"])


def _kv_state_step(s_ref, so_ref, q_scr, k_scr, f_scr, v, decay):
    def body(i, o):
        dec = f_scr[pl.ds(i, 1), :] if decay is None else decay
        snew = s_ref[i] * dec + k_scr[pl.ds(i, 1), :] * v
        so_ref[i] = snew
        return o + q_scr[pl.ds(i, 1), :] * snew

    return lax.fori_loop(0, HEAD_DIM, body, jnp.zeros_like(v))


def _hgrn_sample_kernel(layer, p_ref, s_ref, lg_ref, ng_ref, y_ref, so_ref, q_scr, k_scr, f_scr):
    h = pl.program_id(0)
    W = MIX_W
    lg = lg_ref[...]
    l0, l1 = lg[:, 0:1], lg[:, 1:2]
    m = jnp.maximum(l0, l1)
    e0, e1 = jnp.exp(l0 - m), jnp.exp(l1 - m)
    z = e0 + e1
    lb0, lb1 = e0 / z, e1 / z
    lb = (lb0 - lb0) if layer == 0 else ((lb0 + lb1) - lb0)
    log_f = _hgrn_log_f(_head_rows(p_ref, W, h), lb)
    f = jnp.exp(log_f)
    q_scr[...] = jax.nn.silu(_head_rows(p_ref, 0, h))
    k_scr[...] = 1.0 - f
    f_scr[...] = f
    v = _head_rows(p_ref, 2 * W, h)
    g = _head_rows(p_ref, 3 * W, h)
    o = _kv_state_step(s_ref, so_ref, q_scr, k_scr, f_scr, v, None)
    y_ref[...] = _head_norm_cols(o, HEAD_NORM_EPS, False) * ng_ref[...] * jax.nn.silu(g)


def _hgrn_sample_call(p_t, s_t, logits_c, norm_g_c, layer):
    bsz = p_t.shape[1]
    return pl.pallas_call(
        functools.partial(_hgrn_sample_kernel, layer),
        grid=(N_HEADS,),
        in_specs=[_full(p_t.shape), _state_block(bsz), _head_block(2), _head_block(1)],
        out_specs=[_head_block(bsz), _state_block(bsz)],
        out_shape=[jax.ShapeDtypeStruct((MIX_W, bsz), F32), jax.ShapeDtypeStruct(s_t.shape, F32)],
        scratch_shapes=[pltpu.VMEM((HEAD_DIM, bsz), F32)] * 3,
        compiler_params=_params("parallel"),
        name="hgrn_sample",
    )(p_t, s_t, logits_c, norm_g_c)


def _ret_sample_kernel(pos0, p_ref, s_ref, inv_ref, y_ref, so_ref, q_scr, k_scr):
    h = pl.program_id(0)
    W = MIX_W
    ang = jnp.float32(pos0) * inv_ref[...]
    cosv, sinv = jnp.cos(ang), jnp.sin(ang)

    def rot(z):
        z1, z2 = z[0:HALF_DIM, :], z[HALF_DIM:HEAD_DIM, :]
        return jnp.concatenate([z1 * cosv - z2 * sinv, z1 * sinv + z2 * cosv], axis=0)

    q_scr[...] = rot(_head_rows(p_ref, 0, h))
    k_scr[...] = rot(_head_rows(p_ref, W, h)) * HEAD_DIM ** -0.5
    v = _head_rows(p_ref, 2 * W, h)
    g = _head_rows(p_ref, 3 * W, h)
    gamma = jnp.float32(math.exp(RET_LOG_GAMMA[N_HEADS - 1]))
    for hh in range(N_HEADS - 2, -1, -1):
        gamma = jnp.where(h == hh, jnp.float32(math.exp(RET_LOG_GAMMA[hh])), gamma)
    o = _kv_state_step(s_ref, so_ref, q_scr, k_scr, None, v, gamma)
    y_ref[...] = _head_norm_cols(o, HEAD_NORM_EPS, False) * jax.nn.silu(g)


def _ret_sample_call(p_t, s_t, pos0):
    bsz = p_t.shape[1]
    inv = RET_THETA ** (-np.arange(HALF_DIM).astype(np.float64) / HALF_DIM)
    inv = jnp.asarray(inv.astype(np.float32).reshape(HALF_DIM, 1))
    return pl.pallas_call(
        functools.partial(_ret_sample_kernel, pos0),
        grid=(N_HEADS,),
        in_specs=[_full(p_t.shape), _state_block(bsz), _full(inv.shape)],
        out_specs=[_head_block(bsz), _state_block(bsz)],
        out_shape=[jax.ShapeDtypeStruct((MIX_W, bsz), F32), jax.ShapeDtypeStruct(s_t.shape, F32)],
        scratch_shapes=[pltpu.VMEM((HEAD_DIM, bsz), F32)] * 2,
        compiler_params=_params("parallel"),
        name="ret_sample",
    )(p_t, s_t, inv)


def _mlstm_sample_kernel(p_ref, pg_ref, conv_ref, c_ref, n_ref, m_ref, cw_ref, cb_ref, igb_ref, fgb_ref, ng_ref,
                         y_ref, co_ref, no_ref, mo_ref, q_scr, k_scr):
    h = pl.program_id(0)
    W = MIX_W

    def conv_rows(base):
        idx = pl.ds(pl.multiple_of(base + h * HEAD_DIM, HEAD_DIM), HEAD_DIM)
        cw = cw_ref[idx, :]
        acc = cw[:, MLSTM_CONV - 1:MLSTM_CONV] * p_ref[idx, :]
        for j in range(MLSTM_CONV - 1):
            acc = acc + cw[:, j:j + 1] * conv_ref[j, idx, :]
        return jax.nn.silu(acc + cb_ref[idx, :])

    q = conv_rows(0)
    k = conv_rows(W) * HEAD_DIM ** -0.5
    v = _head_rows(p_ref, 2 * W, h)
    og = _head_rows(p_ref, 3 * W, h)
    ig = pg_ref[pl.ds(h, 1), :] + igb_ref[pl.ds(h, 1), :]
    lf = jax.nn.log_sigmoid(pg_ref[pl.ds(N_HEADS + h, 1), :] + fgb_ref[pl.ds(h, 1), :])
    m_prev = m_ref[pl.ds(h, 1), :]
    n0 = n_ref[...]
    m_inter = lf + m_prev
    m_t = jnp.maximum(m_inter, ig)
    sc = jnp.sum(q * k, axis=0, keepdims=True) * jnp.exp(ig - m_t)
    w_inter = jnp.exp(m_inter - m_t)
    a_prev = jnp.exp(lf + m_prev - m_t)
    a_s = jnp.exp(ig + lf - lf - m_t)
    q_scr[...] = q
    k_scr[...] = k * a_s

    def body(i, qc):
        c0 = c_ref[i]
        co_ref[i] = c0 * a_prev + k_scr[pl.ds(i, 1), :] * v
        return qc + q_scr[pl.ds(i, 1), :] * c0

    qc = lax.fori_loop(0, HEAD_DIM, body, jnp.zeros_like(v))
    num = sc * v + w_inter * qc
    den = sc + w_inter * jnp.sum(q * n0, axis=0, keepdims=True)
    hh = num / jnp.maximum(jnp.abs(den), jnp.exp(-m_t))
    no_ref[...] = n0 * a_prev + k * a_s
    mo_ref[pl.ds(h, 1), :] = m_t
    y_ref[...] = jax.nn.sigmoid(og) * (_head_norm_cols(hh, HEAD_NORM_EPS, True) * ng_ref[...])


def _mlstm_sample_call(p_t, pg_t, conv_t, c_t, n_t, m_t, prm):
    bsz = p_t.shape[1]
    nblk = pl.BlockSpec((None, HEAD_DIM, bsz), lambda h: (h, 0, 0))
    return pl.pallas_call(
        _mlstm_sample_kernel,
        grid=(N_HEADS,),
        in_specs=[_full(p_t.shape), _full(pg_t.shape), _full(conv_t.shape), _state_block(bsz), nblk,
                  _full(m_t.shape), _full((2 * MIX_W, MLSTM_CONV)), _full((2 * MIX_W, 1)),
                  _full((N_HEADS, 1)), _full((N_HEADS, 1)), _head_block(1)],
        out_specs=[_head_block(bsz), _state_block(bsz), nblk, _full(m_t.shape)],
        out_shape=[jax.ShapeDtypeStruct((MIX_W, bsz), F32), jax.ShapeDtypeStruct(c_t.shape, F32),
                   jax.ShapeDtypeStruct(n_t.shape, F32), jax.ShapeDtypeStruct(m_t.shape, F32)],
        scratch_shapes=[pltpu.VMEM((HEAD_DIM, bsz), F32)] * 2,
        compiler_params=_params("arbitrary"),
        name="mlstm_sample",
    )(p_t, pg_t, conv_t, c_t, n_t, m_t, prm["cw_t"], prm["cb_c"], prm["igb_c"], prm["fgb_c"], prm["ng_c"])


def _merge_kernel(x_ref, xn_ref, ya_ref, yb_ref, yc_ref, yd_ref, wg_ref, wb_ref, wo_ref, g2_ref, x1_ref, xn2_ref):
    gates = jax.nn.sigmoid(_dot(xn_ref[...], wg_ref[...]))
    merged = None
    for m, y_ref in enumerate((ya_ref, yb_ref, yc_ref, yd_ref)):
        br = _dot(y_ref[...].astype(BF16), wb_ref[m])
        term = gates[:, m * D_MODEL:(m + 1) * D_MODEL] * br
        merged = term if merged is None else merged + term
    x1 = x_ref[...] + _dot(merged.astype(BF16), wo_ref[...])
    x1_ref[...] = x1
    xn2 = x1 * lax.rsqrt(jnp.mean(x1 * x1, axis=-1, keepdims=True) + NORM_EPS) * g2_ref[...]
    xn2_ref[...] = xn2.astype(BF16)


def _merge_call(x2d, xn, ys, w_gate, w_branch, w_out, g2, tm):
    n = x2d.shape[0]
    row = lambda i: (i, 0)
    return pl.pallas_call(
        _merge_kernel,
        grid=(n // tm,),
        in_specs=[pl.BlockSpec((tm, D_MODEL), row)] * 2 + [pl.BlockSpec((tm, MIX_W), row)] * 4
        + [_full(w_gate.shape), _full(w_branch.shape), _full(w_out.shape), _full((1, D_MODEL))],
        out_specs=[pl.BlockSpec((tm, D_MODEL), row)] * 2,
        out_shape=[jax.ShapeDtypeStruct((n, D_MODEL), F32), jax.ShapeDtypeStruct((n, D_MODEL), BF16)],
        compiler_params=_params("parallel"),
        name="merge_out",
    )(x2d, xn, *ys, w_gate, w_branch, w_out, g2.reshape(1, D_MODEL))


def _top_ranked(s, count, break_ties):
    idx = _iota(s.shape, 0)
    rank = jnp.full(s.shape, float(count), F32)
    vals = []
    for r in range(count):
        mx = jnp.max(s, axis=0, keepdims=True)
        sel = s == mx
        if break_ties:
            sel = idx == jnp.min(jnp.where(sel, idx, s.shape[0]), axis=0, keepdims=True)
        rank = jnp.where(sel, float(r), rank)
        s = jnp.where(sel, NEG_BIG, s)
        vals.append(mx)
    ranked = jnp.sum((rank < float(count)).astype(F32), axis=0, keepdims=True)
    return vals, rank, ranked


_LATTICE_WIDTHS = tuple(PEER_TOPK // (i + 1) for i in range(PEER_TOPK))


def _peer_head_stats(s1, s2, break_ties):
    v1, rank1, n1 = _top_ranked(s1, PEER_TOPK, break_ties)
    v2, rank2, n2 = _top_ranked(s2, PEER_TOPK, break_ties)
    rows = [v1[i] + v2[j] for i in range(PEER_TOPK) for j in range(_LATTICE_WIDTHS[i])]
    pad = (-len(rows)) % 8
    cand = jnp.concatenate(rows + [jnp.full((pad, s1.shape[1]), NEG_BIG, F32)], axis=0)
    _, crank, nc = _top_ranked(cand, PEER_TOPK, break_ties)
    chosen = crank < float(PEER_TOPK)
    z = jnp.sum(jnp.where(chosen, jnp.exp(cand - rows[0]), 0.0), axis=0, keepdims=True)
    taken = chosen.astype(F32)
    lim = jnp.zeros_like(s1)
    off = 0
    for i in range(PEER_TOPK):
        count_i = jnp.sum(taken[off:off + _LATTICE_WIDTHS[i], :], axis=0, keepdims=True)
        lim = jnp.where(rank1 == float(i), count_i, lim)
        off += _LATTICE_WIDTHS[i]
    tied = jnp.maximum(jnp.maximum(n1, n2), nc) - float(PEER_TOPK)
    stats = (lim, jnp.exp(s1 - v1[0]) / z, rank2.astype(BF16), jnp.exp(s2 - v2[0]).astype(BF16))
    return stats, tied


def _peer_score_kernel(xn_ref, wq_ref, keys_ref, lim_ref, cf_ref, r2_ref, e2_ref):
    q_t = _dot_nt(wq_ref[...], xn_ref[...])
    out_refs = (lim_ref, cf_ref, r2_ref, e2_ref)

    def all_heads(break_ties):
        tied = jnp.zeros((1, q_t.shape[1]), F32)
        for h in range(PEER_HEADS):
            base = h * 2 * PEER_QHALF
            s1 = _dot(keys_ref[h, 0], q_t[base:base + PEER_QHALF, :], HI)
            s2 = _dot(keys_ref[h, 1], q_t[base + PEER_QHALF:base + 2 * PEER_QHALF, :], HI)
            stats, tied_h = _peer_head_stats(s1, s2, break_ties)
            for ref, val in zip(out_refs, stats):
                ref[h] = val
            tied = jnp.maximum(tied, tied_h)
        return tied

    tied = all_heads(False)

    @pl.when(jnp.max(tied) > 0.0)
    def _():
        all_heads(True)


def _peer_score_call(xn2, wq_t, keys, tm):
    n = xn2.shape[0]
    blk = pl.BlockSpec((PEER_HEADS, PEER_N_KEYS, tm), lambda i: (0, 0, i))
    big = lambda dt: jax.ShapeDtypeStruct((PEER_HEADS, PEER_N_KEYS, n), dt)
    return pl.pallas_call(
        _peer_score_kernel,
        grid=(n // tm,),
        in_specs=[pl.BlockSpec((tm, D_MODEL), lambda i: (i, 0)), _full(wq_t.shape), _full(keys.shape)],
        out_specs=[blk] * 4,
        out_shape=[big(F32), big(F32), big(BF16), big(BF16)],
        compiler_params=_params("parallel"),
        name="peer_scores",
    )(xn2, wq_t, keys)


def _peer_expert_kernel(xn_ref, x1_ref, u_ref, vt_ref, lim_ref, cf_ref, r2_ref, e2_ref, o_ref, acc_ref):
    j = pl.program_id(1)

    @pl.when(j == 0)
    def _():
        acc_ref[...] = jnp.zeros_like(acc_ref)

    hid = _dot_nt(u_ref[...], xn_ref[...])
    act = 0.5 * hid * (1.0 + lax.erf(hid * INV_SQRT2))
    te, tm = hid.shape
    per_tile = te // PEER_N_KEYS
    pieces = []
    for al in range(per_tile):
        a = j * per_tile + al
        gate = jnp.zeros((PEER_N_KEYS, tm), BF16)
        for h in range(PEER_HEADS):
            lim = lim_ref[h, pl.ds(a, 1), :].astype(BF16)
            cf = cf_ref[h, pl.ds(a, 1), :].astype(BF16)
            gate = gate + jnp.where(r2_ref[h] < lim, e2_ref[h] * cf, jnp.zeros((), BF16))
        pieces.append((act[al * PEER_N_KEYS:(al + 1) * PEER_N_KEYS, :] * gate.astype(F32)).astype(BF16))
    acc_ref[...] += _dot(vt_ref[...], jnp.concatenate(pieces, axis=0))

    @pl.when(j == pl.num_programs(1) - 1)
    def _():
        o_ref[...] = x1_ref[...] + acc_ref[...].T


def _peer_expert_call(xn2, x1, u, v_t, stats, tm, te):
    n = xn2.shape[0]
    n_exp = u.shape[0]
    tok = pl.BlockSpec((tm, D_MODEL), lambda i, j: (i, 0))
    blk = pl.BlockSpec((PEER_HEADS, PEER_N_KEYS, tm), lambda i, j: (0, 0, i))
    return pl.pallas_call(
        _peer_expert_kernel,
        grid=(n // tm, n_exp // te),
        in_specs=[tok, tok,
                  pl.BlockSpec((te, D_MODEL), lambda i, j: (j, 0)),
                  pl.BlockSpec((D_MODEL, te), lambda i, j: (0, j)),
                  blk, blk, blk, blk],
        out_specs=tok,
        out_shape=jax.ShapeDtypeStruct((n, D_MODEL), F32),
        scratch_shapes=[pltpu.VMEM((D_MODEL, tm), F32)],
        compiler_params=_params("parallel", "arbitrary"),
        name="peer_experts",
    )(xn2, x1, u, v_t, *stats)


def _final_norm_kernel(x_ref, g_ref, o_ref):
    x = x_ref[...]
    o_ref[...] = x * lax.rsqrt(jnp.mean(x * x, axis=-1, keepdims=True) + NORM_EPS) * g_ref[...]


def _final_norm_call(x2d, g, tm):
    n = x2d.shape[0]
    row = lambda i: (i, 0)
    return pl.pallas_call(
        _final_norm_kernel,
        grid=(n // tm,),
        in_specs=[pl.BlockSpec((tm, D_MODEL), row), _full((1, D_MODEL))],
        out_specs=pl.BlockSpec((tm, D_MODEL), row),
        out_shape=jax.ShapeDtypeStruct((n, D_MODEL), F32),
        compiler_params=_params("parallel"),
        name="final_norm",
    )(x2d, g.reshape(1, D_MODEL))


def _layer_weights(l, w_in, rw, hg, ml, w_branch, w_out, ln1_g, ln2_g, peer_w_q, peer_keys, peer_u, peer_v):
    o3 = RWKV_COLS + 2 * 4 * MIX_W
    o4 = o3 + MLSTM_COLS
    gcol = o3 + 3 * MIX_W
    w = w_in[l]
    w_main = jnp.concatenate([w[:, 0:gcol], w[:, gcol + 2 * N_HEADS:o4]], axis=1).astype(BF16)
    w_mlg = jnp.pad(w[:, gcol:gcol + 2 * N_HEADS], ((0, 0), (0, LANES - 2 * N_HEADS)))
    row = lambda a: a[l].reshape(1, -1)
    col = lambda a: a[l].reshape(-1, 1)
    rwkv = dict(mu=row(rw["mu"]), w0=row(rw["w0"]), wup=rw["w_up"][l], a0=row(rw["a0"]), aup=rw["a_up"][l],
                gup=rw["g_up"][l], kk=row(rw["k_k"]), ka=row(rw["k_a"]), rk=row(rw["r_k"]),
                lng=row(rw["ln_g"]), lnb=row(rw["ln_b"]),
                mu_c=col(rw["mu"]), w0_c=col(rw["w0"]), wup_t=rw["w_up"][l].T, a0_c=col(rw["a0"]),
                aup_t=rw["a_up"][l].T, gup_t=rw["g_up"][l].T, kk_c=col(rw["k_k"]), ka_c=col(rw["k_a"]),
                rk_c=col(rw["r_k"]), lng_c=col(rw["ln_g"]), lnb_c=col(rw["ln_b"]))
    gate_b = jnp.pad(jnp.concatenate([ml["ig_b"][l], ml["fg_b"][l]]), (0, LANES - 2 * N_HEADS)).reshape(1, LANES)
    mlstm = dict(cw=ml["conv_w"][l], cb=row(ml["conv_b"]), gate_b=gate_b, ng=row(ml["norm_g"]),
                 cw_t=ml["conv_w"][l].T, cb_c=col(ml["conv_b"]), igb_c=col(ml["ig_b"]), fgb_c=col(ml["fg_b"]),
                 ng_c=col(ml["norm_g"]))
    return dict(
        ln1=ln1_g[l], ln2=ln2_g[l], w_main=w_main, w_mlg=w_mlg, w_gate=w[:, o4:].astype(BF16),
        w_branch=w_branch[l].astype(BF16), w_out=w_out[l].astype(BF16),
        rwkv=rwkv, hgrn_logits=hg["lb_logits"], hgrn_logits_c=hg["lb_logits"].T,
        hgrn_ng=row(hg["norm_g"]), hgrn_ng_c=col(hg["norm_g"]), mlstm=mlstm,
        wq_t=peer_w_q[l].T.astype(BF16), keys=peer_keys[l], u=peer_u[l].astype(BF16),
        v_t=peer_v[l].T.astype(BF16))


def _lanes_last(a):
    return jnp.moveaxis(a, 0, -1)


def _prompt_mixers(proj, bsz, lw, layer):
    _, pr, ph, pt, pm, pg = proj
    ya, shift, s_a = _rwkv_prompt_call(pr, bsz, lw["rwkv"])
    yb, s_b = _hgrn_prompt_call(ph, bsz, lw["hgrn_logits"], lw["hgrn_ng"], layer)
    yc, s_c = _ret_prompt_call(pt, bsz, 0)
    m = lw["mlstm"]
    yd, conv, c_d, n_d, m_d = _mlstm_prompt_call(pm, pg, bsz, m["cw"], m["cb"], m["gate_b"], m["ng"])
    states = (shift[:, 0, :], s_a, s_b, s_c, conv[:, 8 - (MLSTM_CONV - 1):, :], c_d, n_d, m_d[:, 0, :N_HEADS])
    return (ya, yb, yc, yd), states


def _sample_mixers(proj, st, lw, layer, pos0):
    _, pr, ph, pt, pm, pg = proj
    shift, s_a, s_b, s_c, conv, c_d, n_d, m_d = st
    to_t = lambda a: jnp.moveaxis(a, 0, -1)
    from_t = lambda a: jnp.moveaxis(a, -1, 0)
    ya, s_a2 = _rwkv_sample_call(pr.T, shift.T, to_t(s_a), lw["rwkv"])
    yb, s_b2 = _hgrn_sample_call(ph.T, to_t(s_b), lw["hgrn_logits_c"], lw["hgrn_ng_c"], layer)
    yc, s_c2 = _ret_sample_call(pt.T, to_t(s_c), pos0)
    yd, c_d2, n_d2, m_d2 = _mlstm_sample_call(pm.T, pg.T, to_t(conv), to_t(c_d), to_t(n_d), m_d.T, lw["mlstm"])
    conv2 = jnp.concatenate([conv[:, 1:, :], pm[:, None, 0:2 * MIX_W]], axis=1)
    states = (pr, from_t(s_a2), from_t(s_b2), from_t(s_c2), conv2, from_t(c_d2), from_t(n_d2), m_d2.T)
    return (ya.T, yb.T, yc.T, yd.T), states


def _run_trunk(x, states, pos0, weights, lnf_g, prompt):
    bsz, t, _ = x.shape
    n = bsz * t
    x2d = x.reshape(n, D_MODEL)
    tm = 256 if n % 256 == 0 else n
    tm_peer = 512 if n % 512 == 0 else n
    new_states = []
    for layer, lw in enumerate(weights):
        proj = _proj_call(x2d, lw["ln1"], lw["w_main"], lw["w_mlg"], tm)
        if prompt:
            ys, st_new = _prompt_mixers(proj, bsz, lw, layer)
        else:
            ys, st_new = _sample_mixers(proj, tuple(s[layer] for s in states), lw, layer, pos0)
        new_states.append(st_new)
        x1, xn2 = _merge_call(x2d, proj[0], ys, lw["w_gate"], lw["w_branch"], lw["w_out"], lw["ln2"], tm)
        stats = _peer_score_call(xn2, lw["wq_t"], lw["keys"], min(tm, 128) if n < 256 else 256)
        x2d = _peer_expert_call(xn2, x1, lw["u"], lw["v_t"], stats, tm_peer, 512)
    y = _final_norm_call(x2d, lnf_g, tm).reshape(bsz, t, D_MODEL)
    stacked = tuple(jnp.stack([s[i] for s in new_states], axis=0) for i in range(8))
    return y, stacked


def kernel(x_prompt, x_sample, state_rwkv_shift, state_rwkv_S, state_hgrn_S, state_ret_S, state_mlstm_conv, state_mlstm_C, state_mlstm_n, state_mlstm_m, ln1_g, w_in, rwkv_mu, rwkv_w0, rwkv_w_up, rwkv_a0, rwkv_a_up, rwkv_g_up, rwkv_k_k, rwkv_k_a, rwkv_r_k, rwkv_ln_g, rwkv_ln_b, hgrn_lb_logits, hgrn_norm_g, mlstm_conv_w, mlstm_conv_b, mlstm_ig_b, mlstm_fg_b, mlstm_norm_g, w_branch, w_out, ln2_g, peer_w_q, peer_keys, peer_u, peer_v, lnf_g):
    depth = w_in.shape[0]
    rw = dict(mu=rwkv_mu, w0=rwkv_w0, w_up=rwkv_w_up, a0=rwkv_a0, a_up=rwkv_a_up, g_up=rwkv_g_up,
              k_k=rwkv_k_k, k_a=rwkv_k_a, r_k=rwkv_r_k, ln_g=rwkv_ln_g, ln_b=rwkv_ln_b)
    hg = dict(lb_logits=hgrn_lb_logits, norm_g=hgrn_norm_g)
    ml = dict(conv_w=mlstm_conv_w, conv_b=mlstm_conv_b, ig_b=mlstm_ig_b, fg_b=mlstm_fg_b, norm_g=mlstm_norm_g)
    weights = [_layer_weights(l, w_in, rw, hg, ml, w_branch, w_out, ln1_g, ln2_g, peer_w_q, peer_keys,
                              peer_u, peer_v) for l in range(depth)]
    past_len = 16384
    y_prompt, p_states = _run_trunk(x_prompt, None, 0, weights, lnf_g, True)
    sample_init = (state_rwkv_shift, state_rwkv_S, state_hgrn_S, state_ret_S, state_mlstm_conv,
                   state_mlstm_C, state_mlstm_n, state_mlstm_m)
    y_sample, s_states = _run_trunk(x_sample, sample_init, past_len, weights, lnf_g, False)
    return (y_prompt, y_sample) + p_states + s_states
```

```python
import functools
import math

import numpy as np
import jax
import jax.numpy as jnp
from jax import lax
from jax.experimental import pallas as pl
from jax.experimental.pallas import tpu as pltpu

F32 = jnp.float32
BF16 = jnp.bfloat16
HI = lax.Precision.HIGHEST

D_MODEL = 1024
N_HEADS = 4
HEAD_DIM = 64
HALF_DIM = HEAD_DIM // 2
MIX_W = N_HEADS * HEAD_DIM
RWKV_W_RANK = 64
RWKV_A_RANK = 64
RWKV_G_RANK = 128
RWKV_COLS = 3 * MIX_W + RWKV_W_RANK + RWKV_A_RANK + RWKV_G_RANK
MLSTM_COLS = 4 * MIX_W + 2 * N_HEADS
GATE_COLS = 4 * D_MODEL
MLSTM_CONV = 4
CHUNK = 64
SUB = 16
RET_THETA = 10000.0
PEER_HEADS = 8
PEER_N_KEYS = 128
PEER_TOPK = 16
PEER_QHALF = 64
NORM_EPS = 1e-6
RWKV_GN_EPS = 64e-5
HEAD_NORM_EPS = 1e-5
MASK_NEG = -1e30
LB_FLOOR = 1e-30
NEG_BIG = -1e30
INV_SQRT2 = 1.0 / math.sqrt(2.0)
RET_LOG_GAMMA = tuple(math.log(1.0 - 2.0 ** (-5.0 - h)) for h in range(N_HEADS))
LANES = 128
BF16_ROWS = 16
VMEM_LIMIT = 56 * 1024 * 1024


X1 = "bf16"


def _dot_dims(a, b, dims, prec):
    if prec == X1:
        return lax.dot_general(a.astype(BF16), b.astype(BF16), dims, preferred_element_type=F32)
    return lax.dot_general(a, b, dims, precision=prec, preferred_element_type=F32)


def _dot(a, b, prec=None):
    return _dot_dims(a, b, (((1,), (0,)), ((), ())), prec)


def _dot_nt(a, b, prec=None):
    return _dot_dims(a, b, (((1,), (1,)), ((), ())), prec)


def _dot_tn(a, b, prec=None):
    return _dot_dims(a, b, (((0,), (0,)), ((), ())), prec)


def _params(*sem):
    return pltpu.CompilerParams(dimension_semantics=sem, vmem_limit_bytes=VMEM_LIMIT)


def _full(shape):
    nd = len(shape)
    return pl.BlockSpec(shape, lambda *_: (0,) * nd)


def _iota(shape, axis):
    return lax.broadcasted_iota(jnp.int32, shape, axis)


def _shift_rows(cur, prev, j):
    row = _iota(cur.shape, 0)
    return jnp.where(row >= j, pltpu.roll(cur, j, 0), pltpu.roll(prev, j, 0))


def _head_norm_rows(y, eps, center):
    if center:
        y = y - jnp.mean(y, axis=1, keepdims=True)
    return y * lax.rsqrt(jnp.mean(y * y, axis=1, keepdims=True) + eps)


def _same_head_mask(width):
    return (_iota((width, width), 0) // HEAD_DIM) == (_iota((width, width), 1) // HEAD_DIM)


def _per_head_row(values, width):
    head = _iota((1, width), 1) // HEAD_DIM
    row = jnp.full((1, width), values[-1], F32)
    for h in range(len(values) - 2, -1, -1):
        row = jnp.where(head == h, values[h], row)
    return row


def _block_diag(x, same_head):
    reps = same_head.shape[0] // x.shape[0]
    return jnp.where(same_head, jnp.concatenate([x] * reps, axis=0), 0.0)


def _head_mean_lanes(y, same_head, split=True):
    e = jnp.where(same_head, 1.0 / HEAD_DIM, 0.0).astype(BF16)
    hi = y.astype(BF16)
    if not split:
        return _dot(hi, e)
    lo = (y - hi.astype(F32)).astype(BF16)
    return _dot(jnp.concatenate([hi, lo], axis=1), jnp.concatenate([e, e], axis=0))


def _head_norm_lanes(y, same_head, eps, center):
    if center:
        y = y - _head_mean_lanes(y, same_head)
    return y * lax.rsqrt(_head_mean_lanes(y * y, same_head) + eps)


def _head_norm_cols(y, eps, center):
    if center:
        y = y - jnp.mean(y, axis=0, keepdims=True)
    return y * lax.rsqrt(jnp.mean(y * y, axis=0, keepdims=True) + eps)


def _proj_kernel(x_ref, g_ref, w_ref, wg_ref, xn_ref, pr_ref, ph_ref, pt_ref, pm_ref, pg_ref):
    x = x_ref[...]
    xn = x * lax.rsqrt(jnp.mean(x * x, axis=-1, keepdims=True) + NORM_EPS) * g_ref[...]
    xb = xn.astype(BF16)
    xn_ref[...] = xb
    p = _dot(xb, w_ref[...])
    pr_ref[...] = p[:, 0:1024]
    ph_ref[...] = p[:, 1024:2048]
    pt_ref[...] = p[:, 2048:3072]
    pm_ref[...] = p[:, 3072:4096]
    pg_ref[...] = _dot(xn, wg_ref[...], HI)


def _proj_call(x2d, g, w_main, w_mlg, tm):
    n = x2d.shape[0]
    row = lambda i: (i, 0)
    out = lambda c, dt: jax.ShapeDtypeStruct((n, c), dt)
    return pl.pallas_call(
        _proj_kernel,
        grid=(n // tm,),
        in_specs=[pl.BlockSpec((tm, D_MODEL), row), _full((1, D_MODEL)),
                  _full((D_MODEL, 4096)), _full((D_MODEL, LANES))],
        out_specs=[pl.BlockSpec((tm, D_MODEL), row)] * 5 + [pl.BlockSpec((tm, LANES), row)],
        out_shape=[out(D_MODEL, BF16)] + [out(D_MODEL, F32)] * 4 + [out(LANES, F32)],
        compiler_params=_params("parallel"),
        name="norm_proj",
    )(x2d, g.reshape(1, D_MODEL), w_main, w_mlg)


def _cumsum_rows(x):
    row = _iota(x.shape, 0)
    shift = 1
    while shift < x.shape[0]:
        x = x + jnp.where(row >= shift, pltpu.roll(x, shift, 0), 0.0)
        shift *= 2
    return x


def _rwkv_prompt_kernel(p_ref, mu_ref, w0_ref, wup_ref, a0_ref, aup_ref, gup_ref, kk_ref, ka_ref, rk_ref,
                        lng_ref, lnb_ref, y_ref, shift_ref, s_ref, prow_ref, sbd_ref):
    p = p_ref[...]
    L = p.shape[0]
    row = _iota(p.shape, 0)
    prev = jnp.where(row >= 1, pltpu.roll(p, 1, 0), jnp.broadcast_to(prow_ref[...], p.shape))
    last = p[L - 1:L, :]
    prow_ref[...] = last
    shift_ref[...] = last
    xs = p + mu_ref[...] * (prev - p)
    W = MIX_W
    r, k, v = xs[:, 0:W], xs[:, W:2 * W], xs[:, 2 * W:3 * W]
    o = 3 * W
    xw = xs[:, o:o + RWKV_W_RANK]
    o += RWKV_W_RANK
    xa = xs[:, o:o + RWKV_A_RANK]
    o += RWKV_A_RANK
    xg = xs[:, o:o + RWKV_G_RANK]
    w = w0_ref[...] + _dot(jnp.tanh(xw), wup_ref[...], X1)
    logw = -jnp.exp(-jax.nn.softplus(-w) - 0.5)
    a = jax.nn.sigmoid(a0_ref[...] + _dot(xa, aup_ref[...], X1))
    g = _dot(jax.nn.sigmoid(xg), gup_ref[...], X1)
    same_head = _same_head_mask(W)
    kk = k * kk_ref[...]
    kk = kk / jnp.maximum(jnp.sqrt(HEAD_DIM * _head_mean_lanes(kk * kk, same_head)), 1e-12)
    kq = k * (1.0 + (a - 1.0) * ka_ref[...])
    bb = kk * a

    cum = _cumsum_rows(logw)
    cum_l = cum[L - 1:L, :]
    e_neg = jnp.exp(-cum)
    e_tail = jnp.exp(cum_l - cum)
    lhs = jnp.concatenate([kk * jnp.exp(cum - logw), r * jnp.exp(cum)], axis=0)
    s0 = sbd_ref[...]
    yield
    gb = _dot_nt(lhs, _block_diag(bb * e_neg, same_head), X1)
    yield
    gk = _dot_nt(lhs, _block_diag(kq * e_neg, same_head), X1)
    qs = _dot_nt(lhs, s0, X1)
    t_idx = _iota((L, W), 0)
    s_idx = _iota((L, W), 1) % HEAD_DIM
    strict = t_idx > s_idx
    incl = t_idx >= s_idx
    n_k = jnp.where(strict, gk[0:L], 0.0)
    n_b = jnp.where(strict, gb[0:L], 0.0)
    m_k = jnp.where(incl, gk[L:2 * L], 0.0)
    m_b = jnp.where(incl, gb[L:2 * L], 0.0)
    tinv = (t_idx == s_idx).astype(F32) - n_b
    pw = n_b
    for _ in range(5):
        yield
        pw = _dot(pw, _block_diag(pw, same_head), X1)
        yield
        tinv = tinv + _dot(tinv, _block_diag(pw, same_head), X1)
    kv = _dot(jnp.concatenate([n_k, m_k], axis=0), _block_diag(v, same_head), X1)
    yield
    u = _dot(tinv, _block_diag(qs[0:L] + kv[0:L], same_head), X1)
    yield
    y = qs[L:2 * L] + kv[L:2 * L] - _dot(m_b, _block_diag(u, same_head), X1)
    upd = _dot_tn(jnp.concatenate([v, -u], axis=0),
                  jnp.concatenate([kq * e_tail, bb * e_tail], axis=0), X1)
    yield
    s_new = s0 * jnp.exp(cum_l) + jnp.where(same_head, upd, 0.0)
    sbd_ref[...] = s_new
    for h in range(N_HEADS):
        sl = slice(h * HEAD_DIM, (h + 1) * HEAD_DIM)
        s_ref[h] = s_new[sl, sl]
    y = _head_norm_lanes(y, same_head, RWKV_GN_EPS, True) * lng_ref[...] + lnb_ref[...]
    y = y + HEAD_DIM * _head_mean_lanes(r * kq * rk_ref[...], same_head) * v
    y_ref[...] = y * g


def _per_sequence(body, batched, carried):
    def kernel(*refs):
        @pl.when(pl.program_id(1) == 0)
        def _():
            for i in carried:
                refs[i][...] = jnp.zeros_like(refs[i])

        group = next(r.shape[0] for r, b in zip(refs, batched) if b)
        runs = [body(*[r.at[gi] if b else r for r, b in zip(refs, batched)]) for gi in range(group)]
        runs = [r for r in runs if r is not None]
        while runs:
            for r in list(runs):
                if next(r, _DONE) is _DONE:
                    runs.remove(r)

    return kernel


_DONE = object()


def _seq_group(bsz):
    return next(g for g in (4, 2, 1) if bsz % g == 0)


def _chunk_specs(grp, width):
    return pl.BlockSpec((grp, CHUNK, width), lambda b, c: (b, c, 0))


def _per_seq_spec(grp, *dims):
    return pl.BlockSpec((grp,) + dims, lambda b, c: (b,) + (0,) * len(dims))


def _state_spec(grp):
    return _per_seq_spec(grp, N_HEADS, HEAD_DIM, HEAD_DIM)


def _rwkv_prompt_call(p, bsz, prm):
    n = p.shape[0]
    t = n // bsz
    grp = _seq_group(bsz)
    small = [prm[k] for k in ("mu", "w0", "wup", "a0", "aup", "gup", "kk", "ka", "rk", "lng", "lnb")]
    y, shift, s = pl.pallas_call(
        _per_sequence(_rwkv_prompt_kernel, (True,) + (False,) * len(small) + (True,) * 5,
                      carried=(len(small) + 4, len(small) + 5)),
        grid=(bsz // grp, t // CHUNK),
        in_specs=[_chunk_specs(grp, RWKV_COLS)] + [_full(a.shape) for a in small],
        out_specs=[_chunk_specs(grp, MIX_W), _per_seq_spec(grp, 1, RWKV_COLS), _state_spec(grp)],
        out_shape=[jax.ShapeDtypeStruct((bsz, t, MIX_W), F32),
                   jax.ShapeDtypeStruct((bsz, 1, RWKV_COLS), F32),
                   jax.ShapeDtypeStruct((bsz, N_HEADS, HEAD_DIM, HEAD_DIM), F32)],
        scratch_shapes=[pltpu.VMEM((grp, 1, RWKV_COLS), F32), pltpu.VMEM((grp, MIX_W, MIX_W), F32)],
        compiler_params=_params("parallel", "arbitrary"),
        name="rwkv_prompt",
    )(p.reshape(bsz, t, RWKV_COLS), *small)
    return y.reshape(n, MIX_W), shift, s


def _hgrn_lower_bound(logits, layer):
    l0, l1 = logits[0:1, :], logits[1:2, :]
    m = jnp.maximum(l0, l1)
    e0, e1 = jnp.exp(l0 - m), jnp.exp(l1 - m)
    z = e0 + e1
    lb0, lb1 = e0 / z, e1 / z
    return (lb0 - lb0) if layer == 0 else ((lb0 + lb1) - lb0)


def _hgrn_log_f(z, lb):
    return jnp.logaddexp(jnp.log(jnp.maximum(lb, LB_FLOOR)), jnp.log1p(-lb) + jax.nn.log_sigmoid(z))


def _hgrn_prompt_kernel(layer, p_ref, lg_ref, ng_ref, y_ref, s_ref, st_ref):
    p = p_ref[...]
    L = p.shape[0]
    W = MIX_W
    q = jax.nn.silu(p[:, 0:W])
    lb = _hgrn_lower_bound(lg_ref[...], layer)
    log_f = _hgrn_log_f(p[:, W:2 * W], lb)
    k = 1.0 - jnp.exp(log_f)
    v = p[:, 2 * W:3 * W]
    g = p[:, 3 * W:4 * W]
    same_head = _same_head_mask(W)
    row = _iota((L, W), 0) % SUB
    b = log_f
    shift = 1
    while shift < SUB:
        b = b + jnp.where(row >= shift, pltpu.roll(b, shift, 0), 0.0)
        shift *= 2
    qe = q * jnp.exp(b)
    intra = jnp.zeros((L, W), F32)
    for lag in range(SUB):
        valid = row >= lag
        kr = k if lag == 0 else pltpu.roll(k, lag, 0)
        br = b if lag == 0 else pltpu.roll(b, lag, 0)
        vr = v if lag == 0 else pltpu.roll(v, lag, 0)
        term = jnp.where(valid, q * kr * jnp.exp(jnp.where(valid, b - br, 0.0)), 0.0)
        intra = intra + HEAD_DIM * _head_mean_lanes(term, same_head, split=False) * vr
        if lag % 4 == 3:
            yield
    st = st_ref[...]
    pieces = []
    for j in range(L // SUB):
        rs = slice(j * SUB, (j + 1) * SUB)
        pieces.append(_dot_nt(qe[rs], st, X1))
        b_l = b[(j + 1) * SUB - 1:(j + 1) * SUB]
        kd = k[rs] * jnp.exp(b_l - b[rs])
        st = st * jnp.exp(b_l) + jnp.where(same_head, _dot_tn(v[rs], kd, X1), 0.0)
        yield
    st_ref[...] = st
    for h in range(N_HEADS):
        sl = slice(h * HEAD_DIM, (h + 1) * HEAD_DIM)
        s_ref[h] = st[sl, sl].T
    o = jnp.concatenate(pieces, axis=0) + intra
    y_ref[...] = _head_norm_lanes(o, same_head, HEAD_NORM_EPS, False) * ng_ref[...] * jax.nn.silu(g)


def _hgrn_prompt_call(p, bsz, logits, norm_g, layer):
    n = p.shape[0]
    t = n // bsz
    grp = _seq_group(bsz)
    y, s = pl.pallas_call(
        _per_sequence(functools.partial(_hgrn_prompt_kernel, layer), (True, False, False, True, True, True),
                      carried=(5,)),
        grid=(bsz // grp, t // CHUNK),
        in_specs=[_chunk_specs(grp, 4 * MIX_W), _full(logits.shape), _full(norm_g.shape)],
        out_specs=[_chunk_specs(grp, MIX_W), _state_spec(grp)],
        out_shape=[jax.ShapeDtypeStruct((bsz, t, MIX_W), F32),
                   jax.ShapeDtypeStruct((bsz, N_HEADS, HEAD_DIM, HEAD_DIM), F32)],
        scratch_shapes=[pltpu.VMEM((grp, MIX_W, MIX_W), F32)],
        compiler_params=_params("parallel", "arbitrary"),
        name="hgrn_prompt",
    )(p.reshape(bsz, t, 4 * MIX_W), logits, norm_g)
    return y.reshape(n, MIX_W), s


def _ret_prompt_kernel(pos0, p_ref, inv_ref, y_ref, s_ref, sbd_ref):
    c = pl.program_id(1)
    p = p_ref[...]
    L = p.shape[0]
    W = MIX_W
    pos = (pos0 + c * L + _iota((L, W), 0)).astype(F32)
    ang = pos * inv_ref[...]
    cosv, sinv = jnp.cos(ang), jnp.sin(ang)
    first = (_iota((L, W), 1) % HEAD_DIM) < HALF_DIM

    def rot(z):
        partner = jnp.where(first, -pltpu.roll(z, W - HALF_DIM, 1), pltpu.roll(z, HALF_DIM, 1))
        return z * cosv + partner * sinv

    q = rot(p[:, 0:W])
    k = rot(p[:, W:2 * W]) * HEAD_DIM ** -0.5
    v = p[:, 2 * W:3 * W]
    g = p[:, 3 * W:4 * W]
    same_head = _same_head_mask(W)
    lg = _per_head_row(RET_LOG_GAMMA, W)
    t_row = _iota((L, W), 0).astype(F32)
    rel = t_row - (_iota((L, W), 1) % HEAD_DIM).astype(F32)
    dmat = jnp.where(rel >= 0, jnp.exp(lg * jnp.maximum(rel, 0.0)), 0.0)
    d_in = jnp.exp(lg * (t_row + 1.0))
    d_st = jnp.exp(lg * (L - 1.0 - t_row))
    s0 = sbd_ref[...]
    yield
    att = _dot_nt(q, _block_diag(k, same_head), X1) * dmat
    inter = _dot(q, s0, X1) * d_in
    yield
    o = _dot(att, _block_diag(v, same_head), X1) + inter
    s_new = s0 * jnp.exp(lg * float(L)) + jnp.where(same_head, _dot_tn(k * d_st, v, X1), 0.0)
    yield
    sbd_ref[...] = s_new
    for h in range(N_HEADS):
        sl = slice(h * HEAD_DIM, (h + 1) * HEAD_DIM)
        s_ref[h] = s_new[sl, sl]
    y_ref[...] = _head_norm_lanes(o, same_head, HEAD_NORM_EPS, False) * jax.nn.silu(g)


def _rotary_inv_row():
    inv = RET_THETA ** (-(np.arange(MIX_W) % HALF_DIM).astype(np.float64) / HALF_DIM)
    return jnp.asarray(inv.astype(np.float32).reshape(1, MIX_W))


def _ret_prompt_call(p, bsz, pos0):
    n = p.shape[0]
    t = n // bsz
    grp = _seq_group(bsz)
    inv = _rotary_inv_row()
    y, s = pl.pallas_call(
        _per_sequence(functools.partial(_ret_prompt_kernel, pos0), (True, False, True, True, True), carried=(4,)),
        grid=(bsz // grp, t // CHUNK),
        in_specs=[_chunk_specs(grp, 4 * MIX_W), _full(inv.shape)],
        out_specs=[_chunk_specs(grp, MIX_W), _state_spec(grp)],
        out_shape=[jax.ShapeDtypeStruct((bsz, t, MIX_W), F32),
                   jax.ShapeDtypeStruct((bsz, N_HEADS, HEAD_DIM, HEAD_DIM), F32)],
        scratch_shapes=[pltpu.VMEM((grp, MIX_W, MIX_W), F32)],
        compiler_params=_params("parallel", "arbitrary"),
        name="ret_prompt",
    )(p.reshape(bsz, t, 4 * MIX_W), inv)
    return y.reshape(n, MIX_W), s


def _mlstm_prompt_kernel(p_ref, pg_ref, cw_ref, cb_ref, gb_ref, ng_ref,
                         y_ref, conv_ref, c_ref, n_ref, m_ref, prev_ref, cbd_ref, nrow_ref, mrow_ref):
    p = p_ref[...]
    L = p.shape[0]
    W = MIX_W
    raw = p[:, 0:2 * W]
    prev = prev_ref[...]
    acc = cw_ref[MLSTM_CONV - 1:MLSTM_CONV, :] * raw
    for j in range(1, MLSTM_CONV):
        acc = acc + cw_ref[MLSTM_CONV - 1 - j:MLSTM_CONV - j, :] * _shift_rows(raw, prev, j)
    prev_ref[...] = raw
    conv_ref[...] = raw[L - 8:L, :]
    qk = jax.nn.silu(acc + cb_ref[...])
    q_all = qk[:, 0:W]
    k_all = qk[:, W:2 * W] * HEAD_DIM ** -0.5
    v_all = p[:, 2 * W:3 * W]
    og = p[:, 3 * W:4 * W]

    gb = pg_ref[...] + gb_ref[...]
    bcum = _cumsum_rows(jax.nn.log_sigmoid(gb))
    gb_t = gb.T
    bcum_t = bcum.T
    spread = lambda a, base: jnp.concatenate(
        [jnp.broadcast_to(a[:, base + h:base + h + 1], (L, HEAD_DIM)) for h in range(N_HEADS)], axis=1)
    rows = lambda a_t, base: jnp.concatenate([a_t[base + h:base + h + 1, :] for h in range(N_HEADS)], axis=1)
    ig_col, b_col = spread(gb, 0), spread(bcum, N_HEADS)
    ig_row, b_row = rows(gb_t, 0), rows(bcum_t, N_HEADS)
    same_head = _same_head_mask(W)
    causal = _iota((L, W), 0) >= (_iota((L, W), 1) % HEAD_DIM)
    m_prev = mrow_ref[...]
    log_d = jnp.where(causal, b_col - b_row + ig_row, MASK_NEG)
    m_inter = b_col + m_prev
    row_max = jnp.concatenate(
        [jnp.broadcast_to(jnp.max(log_d[:, h * HEAD_DIM:(h + 1) * HEAD_DIM], axis=1, keepdims=True), (L, HEAD_DIM))
         for h in range(N_HEADS)], axis=1)
    m_t = jnp.maximum(m_inter, row_max)
    c0 = cbd_ref[...]
    n0 = nrow_ref[...]
    yield
    sc = _dot_nt(q_all, _block_diag(k_all, same_head), X1) * jnp.exp(log_d - m_t)
    w_inter = jnp.exp(m_inter - m_t)
    inter = w_inter * _dot(q_all, c0, X1)
    m_l = m_t[L - 1:L, :]
    b_l = b_col[L - 1:L, :]
    a_prev = jnp.exp(b_l + m_prev - m_l)
    ks = k_all * jnp.exp(ig_col + b_l - b_col - m_l)
    yield
    num = _dot(sc, _block_diag(v_all, same_head), X1) + inter
    c_new = c0 * a_prev + jnp.where(same_head, _dot_tn(ks, v_all, X1), 0.0)
    yield
    den = HEAD_DIM * (_head_mean_lanes(sc, same_head) + w_inter * _head_mean_lanes(q_all * n0, same_head))
    hh = num / jnp.maximum(jnp.abs(den), jnp.exp(-m_t))
    n_new = n0 * a_prev + jnp.sum(ks, axis=0, keepdims=True)
    yield
    cbd_ref[...] = c_new
    nrow_ref[...] = n_new
    mrow_ref[...] = m_l
    m_out = jnp.zeros(m_ref.shape, F32)
    m_first = jnp.logical_and(_iota(m_ref.shape, 0) == 0, _iota(m_ref.shape, 1) < N_HEADS)
    for h in range(N_HEADS):
        sl = slice(h * HEAD_DIM, (h + 1) * HEAD_DIM)
        c_ref[h] = c_new[sl, sl]
        n_ref[h:h + 1, :] = n_new[:, sl]
        m_out = jnp.where(jnp.logical_and(m_first, _iota(m_ref.shape, 1) == h), m_l[:, h * HEAD_DIM:h * HEAD_DIM + 1], m_out)
    m_ref[...] = m_out
    hn = _head_norm_lanes(hh, same_head, HEAD_NORM_EPS, True) * ng_ref[...]
    y_ref[...] = jax.nn.sigmoid(og) * hn


def _mlstm_prompt_call(p, pg, bsz, conv_w, conv_b, gate_b, norm_g):
    n = p.shape[0]
    t = n // bsz
    grp = _seq_group(bsz)
    outs = pl.pallas_call(
        _per_sequence(_mlstm_prompt_kernel, (True, True) + (False,) * 4 + (True,) * 9, carried=(11, 12, 13, 14)),
        grid=(bsz // grp, t // CHUNK),
        in_specs=[_chunk_specs(grp, 4 * MIX_W), _chunk_specs(grp, LANES), _full(conv_w.shape),
                  _full(conv_b.shape), _full(gate_b.shape), _full(norm_g.shape)],
        out_specs=[_chunk_specs(grp, MIX_W),
                   _per_seq_spec(grp, 8, 2 * MIX_W),
                   _state_spec(grp),
                   _per_seq_spec(grp, N_HEADS, HEAD_DIM),
                   _per_seq_spec(grp, 8, LANES)],
        out_shape=[jax.ShapeDtypeStruct((bsz, t, MIX_W), F32),
                   jax.ShapeDtypeStruct((bsz, 8, 2 * MIX_W), F32),
                   jax.ShapeDtypeStruct((bsz, N_HEADS, HEAD_DIM, HEAD_DIM), F32),
                   jax.ShapeDtypeStruct((bsz, N_HEADS, HEAD_DIM), F32),
                   jax.ShapeDtypeStruct((bsz, 8, LANES), F32)],
        scratch_shapes=[pltpu.VMEM((grp, CHUNK, 2 * MIX_W), F32), pltpu.VMEM((grp, MIX_W, MIX_W), F32),
                        pltpu.VMEM((grp, 1, MIX_W), F32), pltpu.VMEM((grp, 1, MIX_W), F32)],
        compiler_params=_params("parallel", "arbitrary"),
        name="mlstm_prompt",
    )(p.reshape(bsz, t, 4 * MIX_W), pg.reshape(bsz, t, LANES), conv_w, conv_b, gate_b, norm_g)
    return (outs[0].reshape(n, MIX_W),) + tuple(outs[1:])


def _head_rows(ref, base, h):
    return ref[pl.ds(pl.multiple_of(base + h * HEAD_DIM, HEAD_DIM), HEAD_DIM), :]


def _rwkv_sample_kernel(p_ref, sh_ref, s_ref, mu_ref, w0_ref, wup_ref, a0_ref, aup_ref, gup_ref, kk_ref, ka_ref,
                        rk_ref, lng_ref, lnb_ref, y_ref, so_ref, v_scr, y_scr):
    h = pl.program_id(0)
    W = MIX_W

    def xs_rows(lo, size, dyn):
        idx = pl.ds(pl.multiple_of(lo + h * HEAD_DIM, HEAD_DIM), size) if dyn else pl.ds(lo, size)
        pv = p_ref[idx, :]
        return pv + mu_ref[idx, :] * (sh_ref[idx, :] - pv)

    r = xs_rows(0, HEAD_DIM, True)
    k = xs_rows(W, HEAD_DIM, True)
    v = xs_rows(2 * W, HEAD_DIM, True)
    xw = xs_rows(3 * W, RWKV_W_RANK, False)
    xa = xs_rows(3 * W + RWKV_W_RANK, RWKV_A_RANK, False)
    xg = xs_rows(3 * W + RWKV_W_RANK + RWKV_A_RANK, RWKV_G_RANK, False)
    w = w0_ref[...] + _dot(wup_ref[...], jnp.tanh(xw), HI)
    wd = jnp.exp(-jnp.exp(-jax.nn.softplus(-w) - 0.5))
    a = jax.nn.sigmoid(a0_ref[...] + _dot(aup_ref[...], xa, HI))
    g = _dot(gup_ref[...], jax.nn.sigmoid(xg), HI)
    kk = k * kk_ref[...]
    kk = kk / jnp.maximum(jnp.sqrt(jnp.sum(kk * kk, axis=0, keepdims=True)), 1e-12)
    kq = k * (1.0 + (a - 1.0) * ka_ref[...])
    bb = kk * a
    v_scr[...] = v

    def body(i, carry):
        sv = s_ref[i]
        sa = -jnp.sum(sv * kk, axis=0, keepdims=True)
        snew = sv * wd + sa * bb + v_scr[pl.ds(i, 1), :] * kq
        so_ref[i] = snew
        y_scr[pl.ds(i, 1), :] = jnp.sum(snew * r, axis=0, keepdims=True)
        return carry

    lax.fori_loop(0, HEAD_DIM, body, 0)
    y = _head_norm_cols(y_scr[...], RWKV_GN_EPS, True) * lng_ref[...] + lnb_ref[...]
    y = y + jnp.sum(r * kq * rk_ref[...], axis=0, keepdims=True) * v
    y_ref[...] = y * g


def _head_block(cols):
    return pl.BlockSpec((HEAD_DIM, cols), lambda h: (h, 0))


def _state_block(bsz):
    return pl.BlockSpec((None, HEAD_DIM, HEAD_DIM, bsz), lambda h: (h, 0, 0, 0))


def _rwkv_sample_call(p_t, shift_t, s_t, prm):
    bsz = p_t.shape[1]
    return pl.pallas_call(
        _rwkv_sample_kernel,
        grid=(N_HEADS,),
        in_specs=[_full(p_t.shape), _full(shift_t.shape), _state_block(bsz), _full((RWKV_COLS, 1)),
                  _head_block(1), _head_block(RWKV_W_RANK), _head_block(1), _head_block(RWKV_A_RANK),
                  _head_block(RWKV_G_RANK), _head_block(1), _head_block(1), _head_block(1), _head_block(1),
                  _head_block(1)],
        out_specs=[_head_block(bsz), _state_block(bsz)],
        out_shape=[jax.ShapeDtypeStruct((MIX_W, bsz), F32), jax.ShapeDtypeStruct(s_t.shape, F32)],
        scratch_shapes=[pltpu.VMEM((HEAD_DIM, bsz), F32), pltpu.VMEM((HEAD_DIM, bsz), F32)],
        compiler_params=_params("parallel"),
        name="rwkv_sample",
    )(p_t, shift_t, s_t, prm["mu_c"], prm["w0_c"], prm["wup_t"], prm["a0_c"], prm["aup_t"], prm["gup_t"],
      prm["kk_c"], prm["ka_c"], prm["rk_c"], prm["lng_c"], prm["lnb_c"])


def _kv_state_step(s_ref, so_ref, q_scr, k_scr, f_scr, v, decay):
    def body(i, o):
        dec = f_scr[pl.ds(i, 1), :] if decay is None else decay
        snew = s_ref[i] * dec + k_scr[pl.ds(i, 1), :] * v
        so_ref[i] = snew
        return o + q_scr[pl.ds(i, 1), :] * snew

    return lax.fori_loop(0, HEAD_DIM, body, jnp.zeros_like(v))


def _hgrn_sample_kernel(layer, p_ref, s_ref, lg_ref, ng_ref, y_ref, so_ref, q_scr, k_scr, f_scr):
    h = pl.program_id(0)
    W = MIX_W
    lg = lg_ref[...]
    l0, l1 = lg[:, 0:1], lg[:, 1:2]
    m = jnp.maximum(l0, l1)
    e0, e1 = jnp.exp(l0 - m), jnp.exp(l1 - m)
    z = e0 + e1
    lb0, lb1 = e0 / z, e1 / z
    lb = (lb0 - lb0) if layer == 0 else ((lb0 + lb1) - lb0)
    log_f = _hgrn_log_f(_head_rows(p_ref, W, h), lb)
    f = jnp.exp(log_f)
    q_scr[...] = jax.nn.silu(_head_rows(p_ref, 0, h))
    k_scr[...] = 1.0 - f
    f_scr[...] = f
    v = _head_rows(p_ref, 2 * W, h)
    g = _head_rows(p_ref, 3 * W, h)
    o = _kv_state_step(s_ref, so_ref, q_scr, k_scr, f_scr, v, None)
    y_ref[...] = _head_norm_cols(o, HEAD_NORM_EPS, False) * ng_ref[...] * jax.nn.silu(g)


def _hgrn_sample_call(p_t, s_t, logits_c, norm_g_c, layer):
    bsz = p_t.shape[1]
    return pl.pallas_call(
        functools.partial(_hgrn_sample_kernel, layer),
        grid=(N_HEADS,),
        in_specs=[_full(p_t.shape), _state_block(bsz), _head_block(2), _head_block(1)],
        out_specs=[_head_block(bsz), _state_block(bsz)],
        out_shape=[jax.ShapeDtypeStruct((MIX_W, bsz), F32), jax.ShapeDtypeStruct(s_t.shape, F32)],
        scratch_shapes=[pltpu.VMEM((HEAD_DIM, bsz), F32)] * 3,
        compiler_params=_params("parallel"),
        name="hgrn_sample",
    )(p_t, s_t, logits_c, norm_g_c)


def _ret_sample_kernel(pos0, p_ref, s_ref, inv_ref, y_ref, so_ref, q_scr, k_scr):
    h = pl.program_id(0)
    W = MIX_W
    ang = jnp.float32(pos0) * inv_ref[...]
    cosv, sinv = jnp.cos(ang), jnp.sin(ang)

    def rot(z):
        z1, z2 = z[0:HALF_DIM, :], z[HALF_DIM:HEAD_DIM, :]
        return jnp.concatenate([z1 * cosv - z2 * sinv, z1 * sinv + z2 * cosv], axis=0)

    q_scr[...] = rot(_head_rows(p_ref, 0, h))
    k_scr[...] = rot(_head_rows(p_ref, W, h)) * HEAD_DIM ** -0.5
    v = _head_rows(p_ref, 2 * W, h)
    g = _head_rows(p_ref, 3 * W, h)
    gamma = jnp.float32(math.exp(RET_LOG_GAMMA[N_HEADS - 1]))
    for hh in range(N_HEADS - 2, -1, -1):
        gamma = jnp.where(h == hh, jnp.float32(math.exp(RET_LOG_GAMMA[hh])), gamma)
    o = _kv_state_step(s_ref, so_ref, q_scr, k_scr, None, v, gamma)
    y_ref[...] = _head_norm_cols(o, HEAD_NORM_EPS, False) * jax.nn.silu(g)


def _ret_sample_call(p_t, s_t, pos0):
    bsz = p_t.shape[1]
    inv = RET_THETA ** (-np.arange(HALF_DIM).astype(np.float64) / HALF_DIM)
    inv = jnp.asarray(inv.astype(np.float32).reshape(HALF_DIM, 1))
    return pl.pallas_call(
        functools.partial(_ret_sample_kernel, pos0),
        grid=(N_HEADS,),
        in_specs=[_full(p_t.shape), _state_block(bsz), _full(inv.shape)],
        out_specs=[_head_block(bsz), _state_block(bsz)],
        out_shape=[jax.ShapeDtypeStruct((MIX_W, bsz), F32), jax.ShapeDtypeStruct(s_t.shape, F32)],
        scratch_shapes=[pltpu.VMEM((HEAD_DIM, bsz), F32)] * 2,
        compiler_params=_params("parallel"),
        name="ret_sample",
    )(p_t, s_t, inv)


def _mlstm_sample_kernel(p_ref, pg_ref, conv_ref, c_ref, n_ref, m_ref, cw_ref, cb_ref, igb_ref, fgb_ref, ng_ref,
                         y_ref, co_ref, no_ref, mo_ref, q_scr, k_scr):
    h = pl.program_id(0)
    W = MIX_W

    def conv_rows(base):
        idx = pl.ds(pl.multiple_of(base + h * HEAD_DIM, HEAD_DIM), HEAD_DIM)
        cw = cw_ref[idx, :]
        acc = cw[:, MLSTM_CONV - 1:MLSTM_CONV] * p_ref[idx, :]
        for j in range(MLSTM_CONV - 1):
            acc = acc + cw[:, j:j + 1] * conv_ref[j, idx, :]
        return jax.nn.silu(acc + cb_ref[idx, :])

    q = conv_rows(0)
    k = conv_rows(W) * HEAD_DIM ** -0.5
    v = _head_rows(p_ref, 2 * W, h)
    og = _head_rows(p_ref, 3 * W, h)
    ig = pg_ref[pl.ds(h, 1), :] + igb_ref[pl.ds(h, 1), :]
    lf = jax.nn.log_sigmoid(pg_ref[pl.ds(N_HEADS + h, 1), :] + fgb_ref[pl.ds(h, 1), :])
    m_prev = m_ref[pl.ds(h, 1), :]
    n0 = n_ref[...]
    m_inter = lf + m_prev
    m_t = jnp.maximum(m_inter, ig)
    sc = jnp.sum(q * k, axis=0, keepdims=True) * jnp.exp(ig - m_t)
    w_inter = jnp.exp(m_inter - m_t)
    a_prev = jnp.exp(lf + m_prev - m_t)
    a_s = jnp.exp(ig + lf - lf - m_t)
    q_scr[...] = q
    k_scr[...] = k * a_s

    def body(i, qc):
        c0 = c_ref[i]
        co_ref[i] = c0 * a_prev + k_scr[pl.ds(i, 1), :] * v
        return qc + q_scr[pl.ds(i, 1), :] * c0

    qc = lax.fori_loop(0, HEAD_DIM, body, jnp.zeros_like(v))
    num = sc * v + w_inter * qc
    den = sc + w_inter * jnp.sum(q * n0, axis=0, keepdims=True)
    hh = num / jnp.maximum(jnp.abs(den), jnp.exp(-m_t))
    no_ref[...] = n0 * a_prev + k * a_s
    mo_ref[pl.ds(h, 1), :] = m_t
    y_ref[...] = jax.nn.sigmoid(og) * (_head_norm_cols(hh, HEAD_NORM_EPS, True) * ng_ref[...])


def _mlstm_sample_call(p_t, pg_t, conv_t, c_t, n_t, m_t, prm):
    bsz = p_t.shape[1]
    nblk = pl.BlockSpec((None, HEAD_DIM, bsz), lambda h: (h, 0, 0))
    return pl.pallas_call(
        _mlstm_sample_kernel,
        grid=(N_HEADS,),
        in_specs=[_full(p_t.shape), _full(pg_t.shape), _full(conv_t.shape), _state_block(bsz), nblk,
                  _full(m_t.shape), _full((2 * MIX_W, MLSTM_CONV)), _full((2 * MIX_W, 1)),
                  _full((N_HEADS, 1)), _full((N_HEADS, 1)), _head_block(1)],
        out_specs=[_head_block(bsz), _state_block(bsz), nblk, _full(m_t.shape)],
        out_shape=[jax.ShapeDtypeStruct((MIX_W, bsz), F32), jax.ShapeDtypeStruct(c_t.shape, F32),
                   jax.ShapeDtypeStruct(n_t.shape, F32), jax.ShapeDtypeStruct(m_t.shape, F32)],
        scratch_shapes=[pltpu.VMEM((HEAD_DIM, bsz), F32)] * 2,
        compiler_params=_params("arbitrary"),
        name="mlstm_sample",
    )(p_t, pg_t, conv_t, c_t, n_t, m_t, prm["cw_t"], prm["cb_c"], prm["igb_c"], prm["fgb_c"], prm["ng_c"])


def _merge_kernel(x_ref, xn_ref, ya_ref, yb_ref, yc_ref, yd_ref, wg_ref, wb_ref, wo_ref, g2_ref, x1_ref, xn2_ref):
    gates = jax.nn.sigmoid(_dot(xn_ref[...], wg_ref[...]))
    merged = None
    for m, y_ref in enumerate((ya_ref, yb_ref, yc_ref, yd_ref)):
        br = _dot(y_ref[...].astype(BF16), wb_ref[m])
        term = gates[:, m * D_MODEL:(m + 1) * D_MODEL] * br
        merged = term if merged is None else merged + term
    x1 = x_ref[...] + _dot(merged.astype(BF16), wo_ref[...])
    x1_ref[...] = x1
    xn2 = x1 * lax.rsqrt(jnp.mean(x1 * x1, axis=-1, keepdims=True) + NORM_EPS) * g2_ref[...]
    xn2_ref[...] = xn2.astype(BF16)


def _merge_call(x2d, xn, ys, w_gate, w_branch, w_out, g2, tm):
    n = x2d.shape[0]
    row = lambda i: (i, 0)
    return pl.pallas_call(
        _merge_kernel,
        grid=(n // tm,),
        in_specs=[pl.BlockSpec((tm, D_MODEL), row)] * 2 + [pl.BlockSpec((tm, MIX_W), row)] * 4
        + [_full(w_gate.shape), _full(w_branch.shape), _full(w_out.shape), _full((1, D_MODEL))],
        out_specs=[pl.BlockSpec((tm, D_MODEL), row)] * 2,
        out_shape=[jax.ShapeDtypeStruct((n, D_MODEL), F32), jax.ShapeDtypeStruct((n, D_MODEL), BF16)],
        compiler_params=_params("parallel"),
        name="merge_out",
    )(x2d, xn, *ys, w_gate, w_branch, w_out, g2.reshape(1, D_MODEL))


def _top_ranked(s, count, break_ties):
    idx = _iota(s.shape, 0)
    rank = jnp.full(s.shape, float(count), F32)
    vals = []
    for r in range(count):
        mx = jnp.max(s, axis=0, keepdims=True)
        sel = s == mx
        if break_ties:
            sel = idx == jnp.min(jnp.where(sel, idx, s.shape[0]), axis=0, keepdims=True)
        rank = jnp.where(sel, float(r), rank)
        s = jnp.where(sel, NEG_BIG, s)
        vals.append(mx)
    ranked = jnp.sum((rank < float(count)).astype(F32), axis=0, keepdims=True)
    return vals, rank, ranked


_LATTICE_WIDTHS = tuple(PEER_TOPK // (i + 1) for i in range(PEER_TOPK))


def _peer_head_stats(s1, s2, break_ties):
    v1, rank1, n1 = _top_ranked(s1, PEER_TOPK, break_ties)
    v2, rank2, n2 = _top_ranked(s2, PEER_TOPK, break_ties)
    rows = [v1[i] + v2[j] for i in range(PEER_TOPK) for j in range(_LATTICE_WIDTHS[i])]
    pad = (-len(rows)) % 8
    cand = jnp.concatenate(rows + [jnp.full((pad, s1.shape[1]), NEG_BIG, F32)], axis=0)
    _, crank, _ = _top_ranked(cand, PEER_TOPK, True)
    chosen = crank < float(PEER_TOPK)
    z = jnp.sum(jnp.where(chosen, jnp.exp(cand - rows[0]), 0.0), axis=0, keepdims=True)
    taken = chosen.astype(F32)
    lim = jnp.zeros_like(s1)
    off = 0
    for i in range(PEER_TOPK):
        count_i = jnp.sum(taken[off:off + _LATTICE_WIDTHS[i], :], axis=0, keepdims=True)
        lim = jnp.where(rank1 == float(i), count_i, lim)
        off += _LATTICE_WIDTHS[i]
    tied = jnp.maximum(n1, n2) - float(PEER_TOPK)
    stats = (lim, jnp.exp(s1 - v1[0]) / z, rank2.astype(BF16), jnp.exp(s2 - v2[0]).astype(BF16))
    return stats, tied


def _peer_score_kernel(xn_ref, wq_ref, keys_ref, lim_ref, cf_ref, r2_ref, e2_ref, s_scr):
    q_t = _dot_nt(wq_ref[...], xn_ref[...])
    out_refs = (lim_ref, cf_ref, r2_ref, e2_ref)
    tied = jnp.zeros((1, q_t.shape[1]), F32)
    for h in range(PEER_HEADS):
        base = h * 2 * PEER_QHALF
        s1 = _dot(keys_ref[h, 0], q_t[base:base + PEER_QHALF, :], HI)
        s2 = _dot(keys_ref[h, 1], q_t[base + PEER_QHALF:base + 2 * PEER_QHALF, :], HI)
        s_scr[h, 0] = s1
        s_scr[h, 1] = s2
        stats, tied_h = _peer_head_stats(s1, s2, False)
        for ref, val in zip(out_refs, stats):
            ref[h] = val
        tied = jnp.maximum(tied, tied_h)

    @pl.when(jnp.max(tied) > 0.0)
    def _():
        for h in range(PEER_HEADS):
            exact, _ = _peer_head_stats(s_scr[h, 0], s_scr[h, 1], True)
            for ref, val in zip(out_refs, exact):
                ref[h] = val


def _peer_score_call(xn2, wq_t, keys, tm):
    n = xn2.shape[0]
    blk = pl.BlockSpec((PEER_HEADS, PEER_N_KEYS, tm), lambda i: (0, 0, i))
    big = lambda dt: jax.ShapeDtypeStruct((PEER_HEADS, PEER_N_KEYS, n), dt)
    return pl.pallas_call(
        _peer_score_kernel,
        grid=(n // tm,),
        in_specs=[pl.BlockSpec((tm, D_MODEL), lambda i: (i, 0)), _full(wq_t.shape), _full(keys.shape)],
        out_specs=[blk] * 4,
        out_shape=[big(F32), big(F32), big(BF16), big(BF16)],
        scratch_shapes=[pltpu.VMEM((PEER_HEADS, 2, PEER_N_KEYS, tm), F32)],
        compiler_params=_params("parallel"),
        name="peer_scores",
    )(xn2, wq_t, keys)


def _peer_expert_kernel(xn_ref, x1_ref, u_ref, vt_ref, lim_ref, cf_ref, r2_ref, e2_ref, o_ref, acc_ref):
    j = pl.program_id(1)

    @pl.when(j == 0)
    def _():
        acc_ref[...] = jnp.zeros_like(acc_ref)

    te = u_ref.shape[0]
    tm = xn_ref.shape[0]
    per_tile = te // PEER_N_KEYS
    pieces = []
    for al in range(per_tile):
        a = j * per_tile + al
        hid = _dot_nt(u_ref[al * PEER_N_KEYS:(al + 1) * PEER_N_KEYS, :], xn_ref[...])
        act = 0.5 * hid * (1.0 + lax.erf(hid * INV_SQRT2))
        grouped = (PEER_N_KEYS // BF16_ROWS, BF16_ROWS, tm)
        gate = jnp.zeros(grouped, BF16)
        for h in range(PEER_HEADS):
            lim = jnp.broadcast_to(lim_ref[h, pl.ds(a, 1), :], (BF16_ROWS, tm)).astype(BF16)
            cf = jnp.broadcast_to(cf_ref[h, pl.ds(a, 1), :], (BF16_ROWS, tm)).astype(BF16)
            r2 = r2_ref[h].reshape(grouped)
            e2 = e2_ref[h].reshape(grouped)
            gate = gate + jnp.where(r2 < lim[None], e2 * cf[None], jnp.zeros((), BF16))
        pieces.append((act * gate.reshape(PEER_N_KEYS, tm).astype(F32)).astype(BF16))
    acc_ref[...] += _dot(vt_ref[...], jnp.concatenate(pieces, axis=0))

    @pl.when(j == pl.num_programs(1) - 1)
    def _():
        o_ref[...] = x1_ref[...] + acc_ref[...].T


def _peer_expert_call(xn2, x1, u, v_t, stats, tm, te):
    n = xn2.shape[0]
    n_exp = u.shape[0]
    tok = pl.BlockSpec((tm, D_MODEL), lambda i, j: (i, 0))
    blk = pl.BlockSpec((PEER_HEADS, PEER_N_KEYS, tm), lambda i, j: (0, 0, i))
    return pl.pallas_call(
        _peer_expert_kernel,
        grid=(n // tm, n_exp // te),
        in_specs=[tok, tok,
                  pl.BlockSpec((te, D_MODEL), lambda i, j: (j, 0)),
                  pl.BlockSpec((D_MODEL, te), lambda i, j: (0, j)),
                  blk, blk, blk, blk],
        out_specs=tok,
        out_shape=jax.ShapeDtypeStruct((n, D_MODEL), F32),
        scratch_shapes=[pltpu.VMEM((D_MODEL, tm), F32)],
        compiler_params=_params("parallel", "arbitrary"),
        name="peer_experts",
    )(xn2, x1, u, v_t, *stats)


def _final_norm_kernel(x_ref, g_ref, o_ref):
    x = x_ref[...]
    o_ref[...] = x * lax.rsqrt(jnp.mean(x * x, axis=-1, keepdims=True) + NORM_EPS) * g_ref[...]


def _final_norm_call(x2d, g, tm):
    n = x2d.shape[0]
    row = lambda i: (i, 0)
    return pl.pallas_call(
        _final_norm_kernel,
        grid=(n // tm,),
        in_specs=[pl.BlockSpec((tm, D_MODEL), row), _full((1, D_MODEL))],
        out_specs=pl.BlockSpec((tm, D_MODEL), row),
        out_shape=jax.ShapeDtypeStruct((n, D_MODEL), F32),
        compiler_params=_params("parallel"),
        name="final_norm",
    )(x2d, g.reshape(1, D_MODEL))


def _layer_weights(l, w_in, rw, hg, ml, w_branch, w_out, ln1_g, ln2_g, peer_w_q, peer_keys, peer_u, peer_v):
    o3 = RWKV_COLS + 2 * 4 * MIX_W
    o4 = o3 + MLSTM_COLS
    gcol = o3 + 3 * MIX_W
    w = w_in[l]
    w_main = jnp.concatenate([w[:, 0:gcol], w[:, gcol + 2 * N_HEADS:o4]], axis=1).astype(BF16)
    w_mlg = jnp.pad(w[:, gcol:gcol + 2 * N_HEADS], ((0, 0), (0, LANES - 2 * N_HEADS)))
    row = lambda a: a[l].reshape(1, -1)
    col = lambda a: a[l].reshape(-1, 1)
    rwkv = dict(mu=row(rw["mu"]), w0=row(rw["w0"]), wup=rw["w_up"][l], a0=row(rw["a0"]), aup=rw["a_up"][l],
                gup=rw["g_up"][l], kk=row(rw["k_k"]), ka=row(rw["k_a"]), rk=row(rw["r_k"]),
                lng=row(rw["ln_g"]), lnb=row(rw["ln_b"]),
                mu_c=col(rw["mu"]), w0_c=col(rw["w0"]), wup_t=rw["w_up"][l].T, a0_c=col(rw["a0"]),
                aup_t=rw["a_up"][l].T, gup_t=rw["g_up"][l].T, kk_c=col(rw["k_k"]), ka_c=col(rw["k_a"]),
                rk_c=col(rw["r_k"]), lng_c=col(rw["ln_g"]), lnb_c=col(rw["ln_b"]))
    gate_b = jnp.pad(jnp.concatenate([ml["ig_b"][l], ml["fg_b"][l]]), (0, LANES - 2 * N_HEADS)).reshape(1, LANES)
    mlstm = dict(cw=ml["conv_w"][l], cb=row(ml["conv_b"]), gate_b=gate_b, ng=row(ml["norm_g"]),
                 cw_t=ml["conv_w"][l].T, cb_c=col(ml["conv_b"]), igb_c=col(ml["ig_b"]), fgb_c=col(ml["fg_b"]),
                 ng_c=col(ml["norm_g"]))
    return dict(
        ln1=ln1_g[l], ln2=ln2_g[l], w_main=w_main, w_mlg=w_mlg, w_gate=w[:, o4:].astype(BF16),
        w_branch=w_branch[l].astype(BF16), w_out=w_out[l].astype(BF16),
        rwkv=rwkv, hgrn_logits=hg["lb_logits"], hgrn_logits_c=hg["lb_logits"].T,
        hgrn_ng=row(hg["norm_g"]), hgrn_ng_c=col(hg["norm_g"]), mlstm=mlstm,
        wq_t=peer_w_q[l].T.astype(BF16), keys=peer_keys[l], u=peer_u[l].astype(BF16),
        v_t=peer_v[l].T.astype(BF16))


def _lanes_last(a):
    return jnp.moveaxis(a, 0, -1)


def _prompt_mixers(proj, bsz, lw, layer):
    _, pr, ph, pt, pm, pg = proj
    ya, shift, s_a = _rwkv_prompt_call(pr, bsz, lw["rwkv"])
    yb, s_b = _hgrn_prompt_call(ph, bsz, lw["hgrn_logits"], lw["hgrn_ng"], layer)
    yc, s_c = _ret_prompt_call(pt, bsz, 0)
    m = lw["mlstm"]
    yd, conv, c_d, n_d, m_d = _mlstm_prompt_call(pm, pg, bsz, m["cw"], m["cb"], m["gate_b"], m["ng"])
    states = (shift[:, 0, :], s_a, s_b, s_c, conv[:, 8 - (MLSTM_CONV - 1):, :], c_d, n_d, m_d[:, 0, :N_HEADS])
    return (ya, yb, yc, yd), states


def _sample_mixers(proj, st, lw, layer, pos0):
    _, pr, ph, pt, pm, pg = proj
    shift, s_a, s_b, s_c, conv, c_d, n_d, m_d = st
    to_t = lambda a: jnp.moveaxis(a, 0, -1)
    from_t = lambda a: jnp.moveaxis(a, -1, 0)
    ya, s_a2 = _rwkv_sample_call(pr.T, shift.T, to_t(s_a), lw["rwkv"])
    yb, s_b2 = _hgrn_sample_call(ph.T, to_t(s_b), lw["hgrn_logits_c"], lw["hgrn_ng_c"], layer)
    yc, s_c2 = _ret_sample_call(pt.T, to_t(s_c), pos0)
    yd, c_d2, n_d2, m_d2 = _mlstm_sample_call(pm.T, pg.T, to_t(conv), to_t(c_d), to_t(n_d), m_d.T, lw["mlstm"])
    conv2 = jnp.concatenate([conv[:, 1:, :], pm[:, None, 0:2 * MIX_W]], axis=1)
    states = (pr, from_t(s_a2), from_t(s_b2), from_t(s_c2), conv2, from_t(c_d2), from_t(n_d2), m_d2.T)
    return (ya.T, yb.T, yc.T, yd.T), states


def _token_tiles(n):
    pick = lambda want: want if n % want == 0 else n
    return pick(256), pick(256), pick(512)


PEER_EXPERT_TILE = 1024


def _run_trunk(x, states, pos0, weights, lnf_g, prompt):
    bsz, t, _ = x.shape
    n = bsz * t
    x2d = x.reshape(n, D_MODEL)
    tm, tm_score, tm_peer = _token_tiles(n)
    new_states = []
    for layer, lw in enumerate(weights):
        proj = _proj_call(x2d, lw["ln1"], lw["w_main"], lw["w_mlg"], tm)
        if prompt:
            ys, st_new = _prompt_mixers(proj, bsz, lw, layer)
        else:
            ys, st_new = _sample_mixers(proj, tuple(s[layer] for s in states), lw, layer, pos0)
        new_states.append(st_new)
        x1, xn2 = _merge_call(x2d, proj[0], ys, lw["w_gate"], lw["w_branch"], lw["w_out"], lw["ln2"], tm)
        stats = _peer_score_call(xn2, lw["wq_t"], lw["keys"], tm_score)
        x2d = _peer_expert_call(xn2, x1, lw["u"], lw["v_t"], stats, tm_peer, PEER_EXPERT_TILE)
    y = _final_norm_call(x2d, lnf_g, tm).reshape(bsz, t, D_MODEL)
    stacked = tuple(jnp.stack([s[i] for s in new_states], axis=0) for i in range(8))
    return y, stacked


def kernel(x_prompt, x_sample, state_rwkv_shift, state_rwkv_S, state_hgrn_S, state_ret_S, state_mlstm_conv, state_mlstm_C, state_mlstm_n, state_mlstm_m, ln1_g, w_in, rwkv_mu, rwkv_w0, rwkv_w_up, rwkv_a0, rwkv_a_up, rwkv_g_up, rwkv_k_k, rwkv_k_a, rwkv_r_k, rwkv_ln_g, rwkv_ln_b, hgrn_lb_logits, hgrn_norm_g, mlstm_conv_w, mlstm_conv_b, mlstm_ig_b, mlstm_fg_b, mlstm_norm_g, w_branch, w_out, ln2_g, peer_w_q, peer_keys, peer_u, peer_v, lnf_g):
    depth = w_in.shape[0]
    rw = dict(mu=rwkv_mu, w0=rwkv_w0, w_up=rwkv_w_up, a0=rwkv_a0, a_up=rwkv_a_up, g_up=rwkv_g_up,
              k_k=rwkv_k_k, k_a=rwkv_k_a, r_k=rwkv_r_k, ln_g=rwkv_ln_g, ln_b=rwkv_ln_b)
    hg = dict(lb_logits=hgrn_lb_logits, norm_g=hgrn_norm_g)
    ml = dict(conv_w=mlstm_conv_w, conv_b=mlstm_conv_b, ig_b=mlstm_ig_b, fg_b=mlstm_fg_b, norm_g=mlstm_norm_g)
    weights = [_layer_weights(l, w_in, rw, hg, ml, w_branch, w_out, ln1_g, ln2_g, peer_w_q, peer_keys,
                              peer_u, peer_v) for l in range(depth)]
    past_len = 16384
    y_prompt, p_states = _run_trunk(x_prompt, None, 0, weights, lnf_g, True)
    sample_init = (state_rwkv_shift, state_rwkv_S, state_hgrn_S, state_ret_S, state_mlstm_conv,
                   state_mlstm_C, state_mlstm_n, state_mlstm_m)
    y_sample, s_states = _run_trunk(x_sample, sample_init, past_len, weights, lnf_g, False)
    return (y_prompt, y_sample) + p_states + s_states
```

```python
import functools
import math

import numpy as np
import jax
import jax.numpy as jnp
from jax import lax
from jax.experimental import pallas as pl
from jax.experimental.pallas import tpu as pltpu

F32 = jnp.float32
BF16 = jnp.bfloat16
HI = lax.Precision.HIGHEST

D_MODEL = 1024
N_HEADS = 4
HEAD_DIM = 64
HALF_DIM = HEAD_DIM // 2
MIX_W = N_HEADS * HEAD_DIM
RWKV_W_RANK = 64
RWKV_A_RANK = 64
RWKV_G_RANK = 128
RWKV_COLS = 3 * MIX_W + RWKV_W_RANK + RWKV_A_RANK + RWKV_G_RANK
MLSTM_COLS = 4 * MIX_W + 2 * N_HEADS
GATE_COLS = 4 * D_MODEL
MLSTM_CONV = 4
CHUNK = 64
SUB = 16
RET_THETA = 10000.0
PEER_HEADS = 8
PEER_N_KEYS = 128
PEER_TOPK = 16
PEER_QHALF = 64
NORM_EPS = 1e-6
RWKV_GN_EPS = 64e-5
HEAD_NORM_EPS = 1e-5
MASK_NEG = -1e30
LB_FLOOR = 1e-30
NEG_BIG = -1e30
INV_SQRT2 = 1.0 / math.sqrt(2.0)
RET_LOG_GAMMA = tuple(math.log(1.0 - 2.0 ** (-5.0 - h)) for h in range(N_HEADS))
LANES = 128
BF16_ROWS = 16
PEER_EXPERT_TILE = 1024
VMEM_LIMIT = 56 * 1024 * 1024


X1 = "bf16"


def _dot_dims(a, b, dims, prec):
    if prec == X1:
        return lax.dot_general(a.astype(BF16), b.astype(BF16), dims, preferred_element_type=F32)
    return lax.dot_general(a, b, dims, precision=prec, preferred_element_type=F32)


def _dot(a, b, prec=None):
    return _dot_dims(a, b, (((1,), (0,)), ((), ())), prec)


def _dot_nt(a, b, prec=None):
    return _dot_dims(a, b, (((1,), (1,)), ((), ())), prec)


def _dot_tn(a, b, prec=None):
    return _dot_dims(a, b, (((0,), (0,)), ((), ())), prec)


def _params(*sem):
    return pltpu.CompilerParams(dimension_semantics=sem, vmem_limit_bytes=VMEM_LIMIT)


def _full(shape):
    nd = len(shape)
    return pl.BlockSpec(shape, lambda *_: (0,) * nd)


def _iota(shape, axis):
    return lax.broadcasted_iota(jnp.int32, shape, axis)


def _shift_rows(cur, prev, j):
    row = _iota(cur.shape, 0)
    return jnp.where(row >= j, pltpu.roll(cur, j, 0), pltpu.roll(prev, j, 0))


def _head_norm_rows(y, eps, center):
    if center:
        y = y - jnp.mean(y, axis=1, keepdims=True)
    return y * lax.rsqrt(jnp.mean(y * y, axis=1, keepdims=True) + eps)


def _same_head_mask(width):
    return (_iota((width, width), 0) // HEAD_DIM) == (_iota((width, width), 1) // HEAD_DIM)


def _per_head_row(values, width):
    head = _iota((1, width), 1) // HEAD_DIM
    row = jnp.full((1, width), values[-1], F32)
    for h in range(len(values) - 2, -1, -1):
        row = jnp.where(head == h, values[h], row)
    return row


def _block_diag(x, same_head):
    reps = same_head.shape[0] // x.shape[0]
    return jnp.where(same_head, jnp.concatenate([x] * reps, axis=0), 0.0)


def _head_mean_lanes(y, same_head, split=True):
    e = jnp.where(same_head, 1.0 / HEAD_DIM, 0.0).astype(BF16)
    hi = y.astype(BF16)
    if not split:
        return _dot(hi, e)
    lo = (y - hi.astype(F32)).astype(BF16)
    return _dot(jnp.concatenate([hi, lo], axis=1), jnp.concatenate([e, e], axis=0))


def _head_norm_lanes(y, same_head, eps, center):
    if center:
        y = y - _head_mean_lanes(y, same_head)
    return y * lax.rsqrt(_head_mean_lanes(y * y, same_head) + eps)


def _head_norm_cols(y, eps, center):
    if center:
        y = y - jnp.mean(y, axis=0, keepdims=True)
    return y * lax.rsqrt(jnp.mean(y * y, axis=0, keepdims=True) + eps)


def _proj_kernel(x_ref, g_ref, w_ref, wg_ref, xn_ref, pr_ref, ph_ref, pt_ref, pm_ref, pg_ref):
    x = x_ref[...]
    xn = x * lax.rsqrt(jnp.mean(x * x, axis=-1, keepdims=True) + NORM_EPS) * g_ref[...]
    xb = xn.astype(BF16)
    xn_ref[...] = xb
    p = _dot(xb, w_ref[...])
    pr_ref[...] = p[:, 0:1024]
    ph_ref[...] = p[:, 1024:2048]
    pt_ref[...] = p[:, 2048:3072]
    pm_ref[...] = p[:, 3072:4096]
    pg_ref[...] = _dot(xn, wg_ref[...], HI)


def _proj_call(x2d, g, w_main, w_mlg, tm):
    n = x2d.shape[0]
    row = lambda i: (i, 0)
    out = lambda c, dt: jax.ShapeDtypeStruct((n, c), dt)
    return pl.pallas_call(
        _proj_kernel,
        grid=(n // tm,),
        in_specs=[pl.BlockSpec((tm, D_MODEL), row), _full((1, D_MODEL)),
                  _full((D_MODEL, 4096)), _full((D_MODEL, LANES))],
        out_specs=[pl.BlockSpec((tm, D_MODEL), row)] * 5 + [pl.BlockSpec((tm, LANES), row)],
        out_shape=[out(D_MODEL, BF16)] + [out(D_MODEL, F32)] * 4 + [out(LANES, F32)],
        compiler_params=_params("parallel"),
        name="norm_proj",
    )(x2d, g.reshape(1, D_MODEL), w_main, w_mlg)


def _cumsum_rows(x):
    row = _iota(x.shape, 0)
    shift = 1
    while shift < x.shape[0]:
        x = x + jnp.where(row >= shift, pltpu.roll(x, shift, 0), 0.0)
        shift *= 2
    return x


def _rwkv_prompt_kernel(p_ref, mu_ref, w0_ref, wup_ref, a0_ref, aup_ref, gup_ref, kk_ref, ka_ref, rk_ref,
                        lng_ref, lnb_ref, y_ref, shift_ref, s_ref, prow_ref, sbd_ref):
    p = p_ref[...]
    L = p.shape[0]
    row = _iota(p.shape, 0)
    prev = jnp.where(row >= 1, pltpu.roll(p, 1, 0), jnp.broadcast_to(prow_ref[...], p.shape))
    last = p[L - 1:L, :]
    prow_ref[...] = last
    shift_ref[...] = last
    xs = p + mu_ref[...] * (prev - p)
    W = MIX_W
    r, k, v = xs[:, 0:W], xs[:, W:2 * W], xs[:, 2 * W:3 * W]
    o = 3 * W
    xw = xs[:, o:o + RWKV_W_RANK]
    o += RWKV_W_RANK
    xa = xs[:, o:o + RWKV_A_RANK]
    o += RWKV_A_RANK
    xg = xs[:, o:o + RWKV_G_RANK]
    w = w0_ref[...] + _dot(jnp.tanh(xw), wup_ref[...], X1)
    logw = -jnp.exp(-jax.nn.softplus(-w) - 0.5)
    a = jax.nn.sigmoid(a0_ref[...] + _dot(xa, aup_ref[...], X1))
    g = _dot(jax.nn.sigmoid(xg), gup_ref[...], X1)
    same_head = _same_head_mask(W)
    kk = k * kk_ref[...]
    kk = kk / jnp.maximum(jnp.sqrt(HEAD_DIM * _head_mean_lanes(kk * kk, same_head)), 1e-12)
    kq = k * (1.0 + (a - 1.0) * ka_ref[...])
    bb = kk * a

    cum = _cumsum_rows(logw)
    cum_l = cum[L - 1:L, :]
    e_neg = jnp.exp(-cum)
    e_tail = jnp.exp(cum_l - cum)
    lhs = jnp.concatenate([kk * jnp.exp(cum - logw), r * jnp.exp(cum)], axis=0)
    s0 = sbd_ref[...]
    yield
    gb = _dot_nt(lhs, _block_diag(bb * e_neg, same_head), X1)
    yield
    gk = _dot_nt(lhs, _block_diag(kq * e_neg, same_head), X1)
    qs = _dot_nt(lhs, s0, X1)
    t_idx = _iota((L, W), 0)
    s_idx = _iota((L, W), 1) % HEAD_DIM
    strict = t_idx > s_idx
    incl = t_idx >= s_idx
    n_k = jnp.where(strict, gk[0:L], 0.0)
    n_b = jnp.where(strict, gb[0:L], 0.0)
    m_k = jnp.where(incl, gk[L:2 * L], 0.0)
    m_b = jnp.where(incl, gb[L:2 * L], 0.0)
    tinv = (t_idx == s_idx).astype(F32) - n_b
    pw = n_b
    for _ in range(5):
        yield
        pw = _dot(pw, _block_diag(pw, same_head), X1)
        yield
        tinv = tinv + _dot(tinv, _block_diag(pw, same_head), X1)
    kv = _dot(jnp.concatenate([n_k, m_k], axis=0), _block_diag(v, same_head), X1)
    yield
    u = _dot(tinv, _block_diag(qs[0:L] + kv[0:L], same_head), X1)
    yield
    y = qs[L:2 * L] + kv[L:2 * L] - _dot(m_b, _block_diag(u, same_head), X1)
    upd = _dot_tn(jnp.concatenate([v, -u], axis=0),
                  jnp.concatenate([kq * e_tail, bb * e_tail], axis=0), X1)
    yield
    s_new = s0 * jnp.exp(cum_l) + jnp.where(same_head, upd, 0.0)
    sbd_ref[...] = s_new
    for h in range(N_HEADS):
        sl = slice(h * HEAD_DIM, (h + 1) * HEAD_DIM)
        s_ref[h] = s_new[sl, sl]
    y = _head_norm_lanes(y, same_head, RWKV_GN_EPS, True) * lng_ref[...] + lnb_ref[...]
    y = y + HEAD_DIM * _head_mean_lanes(r * kq * rk_ref[...], same_head) * v
    y_ref[...] = y * g


def _per_sequence(body, batched, carried):
    def kernel(*refs):
        @pl.when(pl.program_id(1) == 0)
        def _():
            for i in carried:
                refs[i][...] = jnp.zeros_like(refs[i])

        group = next(r.shape[0] for r, b in zip(refs, batched) if b)
        runs = [body(*[r.at[gi] if b else r for r, b in zip(refs, batched)]) for gi in range(group)]
        runs = [r for r in runs if r is not None]
        while runs:
            for r in list(runs):
                if next(r, _DONE) is _DONE:
                    runs.remove(r)

    return kernel


_DONE = object()


def _seq_group(bsz):
    return next(g for g in (4, 2, 1) if bsz % g == 0)


def _chunk_specs(grp, width):
    return pl.BlockSpec((grp, CHUNK, width), lambda b, c: (b, c, 0))


def _per_seq_spec(grp, *dims):
    return pl.BlockSpec((grp,) + dims, lambda b, c: (b,) + (0,) * len(dims))


def _state_spec(grp):
    return _per_seq_spec(grp, N_HEADS, HEAD_DIM, HEAD_DIM)


def _rwkv_prompt_call(p, bsz, prm):
    n = p.shape[0]
    t = n // bsz
    grp = _seq_group(bsz)
    small = [prm[k] for k in ("mu", "w0", "wup", "a0", "aup", "gup", "kk", "ka", "rk", "lng", "lnb")]
    y, shift, s = pl.pallas_call(
        _per_sequence(_rwkv_prompt_kernel, (True,) + (False,) * len(small) + (True,) * 5,
                      carried=(len(small) + 4, len(small) + 5)),
        grid=(bsz // grp, t // CHUNK),
        in_specs=[_chunk_specs(grp, RWKV_COLS)] + [_full(a.shape) for a in small],
        out_specs=[_chunk_specs(grp, MIX_W), _per_seq_spec(grp, 1, RWKV_COLS), _state_spec(grp)],
        out_shape=[jax.ShapeDtypeStruct((bsz, t, MIX_W), F32),
                   jax.ShapeDtypeStruct((bsz, 1, RWKV_COLS), F32),
                   jax.ShapeDtypeStruct((bsz, N_HEADS, HEAD_DIM, HEAD_DIM), F32)],
        scratch_shapes=[pltpu.VMEM((grp, 1, RWKV_COLS), F32), pltpu.VMEM((grp, MIX_W, MIX_W), F32)],
        compiler_params=_params("parallel", "arbitrary"),
        name="rwkv_prompt",
    )(p.reshape(bsz, t, RWKV_COLS), *small)
    return y.reshape(n, MIX_W), shift, s


def _hgrn_lower_bound(logits, layer):
    l0, l1 = logits[0:1, :], logits[1:2, :]
    m = jnp.maximum(l0, l1)
    e0, e1 = jnp.exp(l0 - m), jnp.exp(l1 - m)
    z = e0 + e1
    lb0, lb1 = e0 / z, e1 / z
    return (lb0 - lb0) if layer == 0 else ((lb0 + lb1) - lb0)


def _hgrn_log_f(z, lb):
    return jnp.logaddexp(jnp.log(jnp.maximum(lb, LB_FLOOR)), jnp.log1p(-lb) + jax.nn.log_sigmoid(z))


def _hgrn_prompt_kernel(layer, p_ref, lg_ref, ng_ref, y_ref, s_ref, st_ref):
    p = p_ref[...]
    L = p.shape[0]
    W = MIX_W
    q = jax.nn.silu(p[:, 0:W])
    lb = _hgrn_lower_bound(lg_ref[...], layer)
    log_f = _hgrn_log_f(p[:, W:2 * W], lb)
    k = 1.0 - jnp.exp(log_f)
    v = p[:, 2 * W:3 * W]
    g = p[:, 3 * W:4 * W]
    same_head = _same_head_mask(W)
    row = _iota((L, W), 0) % SUB
    b = log_f
    shift = 1
    while shift < SUB:
        b = b + jnp.where(row >= shift, pltpu.roll(b, shift, 0), 0.0)
        shift *= 2
    qe = q * jnp.exp(b)
    intra = jnp.zeros((L, W), F32)
    for lag in range(SUB):
        valid = row >= lag
        kr = k if lag == 0 else pltpu.roll(k, lag, 0)
        br = b if lag == 0 else pltpu.roll(b, lag, 0)
        vr = v if lag == 0 else pltpu.roll(v, lag, 0)
        term = jnp.where(valid, q * kr * jnp.exp(jnp.where(valid, b - br, 0.0)), 0.0)
        intra = intra + HEAD_DIM * _head_mean_lanes(term, same_head, split=False) * vr
        if lag % 4 == 3:
            yield
    st = st_ref[...]
    pieces = []
    for j in range(L // SUB):
        rs = slice(j * SUB, (j + 1) * SUB)
        pieces.append(_dot_nt(qe[rs], st, X1))
        b_l = b[(j + 1) * SUB - 1:(j + 1) * SUB]
        kd = k[rs] * jnp.exp(b_l - b[rs])
        st = st * jnp.exp(b_l) + jnp.where(same_head, _dot_tn(v[rs], kd, X1), 0.0)
        yield
    st_ref[...] = st
    for h in range(N_HEADS):
        sl = slice(h * HEAD_DIM, (h + 1) * HEAD_DIM)
        s_ref[h] = st[sl, sl].T
    o = jnp.concatenate(pieces, axis=0) + intra
    y_ref[...] = _head_norm_lanes(o, same_head, HEAD_NORM_EPS, False) * ng_ref[...] * jax.nn.silu(g)


def _hgrn_prompt_call(p, bsz, logits, norm_g, layer):
    n = p.shape[0]
    t = n // bsz
    grp = _seq_group(bsz)
    y, s = pl.pallas_call(
        _per_sequence(functools.partial(_hgrn_prompt_kernel, layer), (True, False, False, True, True, True),
                      carried=(5,)),
        grid=(bsz // grp, t // CHUNK),
        in_specs=[_chunk_specs(grp, 4 * MIX_W), _full(logits.shape), _full(norm_g.shape)],
        out_specs=[_chunk_specs(grp, MIX_W), _state_spec(grp)],
        out_shape=[jax.ShapeDtypeStruct((bsz, t, MIX_W), F32),
                   jax.ShapeDtypeStruct((bsz, N_HEADS, HEAD_DIM, HEAD_DIM), F32)],
        scratch_shapes=[pltpu.VMEM((grp, MIX_W, MIX_W), F32)],
        compiler_params=_params("parallel", "arbitrary"),
        name="hgrn_prompt",
    )(p.reshape(bsz, t, 4 * MIX_W), logits, norm_g)
    return y.reshape(n, MIX_W), s


def _ret_prompt_kernel(pos0, p_ref, inv_ref, y_ref, s_ref, sbd_ref):
    c = pl.program_id(1)
    p = p_ref[...]
    L = p.shape[0]
    W = MIX_W
    pos = (pos0 + c * L + _iota((L, W), 0)).astype(F32)
    ang = pos * inv_ref[...]
    cosv, sinv = jnp.cos(ang), jnp.sin(ang)
    first = (_iota((L, W), 1) % HEAD_DIM) < HALF_DIM

    def rot(z):
        partner = jnp.where(first, -pltpu.roll(z, W - HALF_DIM, 1), pltpu.roll(z, HALF_DIM, 1))
        return z * cosv + partner * sinv

    q = rot(p[:, 0:W])
    k = rot(p[:, W:2 * W]) * HEAD_DIM ** -0.5
    v = p[:, 2 * W:3 * W]
    g = p[:, 3 * W:4 * W]
    same_head = _same_head_mask(W)
    lg = _per_head_row(RET_LOG_GAMMA, W)
    t_row = _iota((L, W), 0).astype(F32)
    rel = t_row - (_iota((L, W), 1) % HEAD_DIM).astype(F32)
    dmat = jnp.where(rel >= 0, jnp.exp(lg * jnp.maximum(rel, 0.0)), 0.0)
    d_in = jnp.exp(lg * (t_row + 1.0))
    d_st = jnp.exp(lg * (L - 1.0 - t_row))
    s0 = sbd_ref[...]
    yield
    att = _dot_nt(q, _block_diag(k, same_head), X1) * dmat
    inter = _dot(q, s0, X1) * d_in
    yield
    o = _dot(att, _block_diag(v, same_head), X1) + inter
    s_new = s0 * jnp.exp(lg * float(L)) + jnp.where(same_head, _dot_tn(k * d_st, v, X1), 0.0)
    yield
    sbd_ref[...] = s_new
    for h in range(N_HEADS):
        sl = slice(h * HEAD_DIM, (h + 1) * HEAD_DIM)
        s_ref[h] = s_new[sl, sl]
    y_ref[...] = _head_norm_lanes(o, same_head, HEAD_NORM_EPS, False) * jax.nn.silu(g)


def _rotary_inv_row():
    inv = RET_THETA ** (-(np.arange(MIX_W) % HALF_DIM).astype(np.float64) / HALF_DIM)
    return jnp.asarray(inv.astype(np.float32).reshape(1, MIX_W))


def _ret_prompt_call(p, bsz, pos0):
    n = p.shape[0]
    t = n // bsz
    grp = _seq_group(bsz)
    inv = _rotary_inv_row()
    y, s = pl.pallas_call(
        _per_sequence(functools.partial(_ret_prompt_kernel, pos0), (True, False, True, True, True), carried=(4,)),
        grid=(bsz // grp, t // CHUNK),
        in_specs=[_chunk_specs(grp, 4 * MIX_W), _full(inv.shape)],
        out_specs=[_chunk_specs(grp, MIX_W), _state_spec(grp)],
        out_shape=[jax.ShapeDtypeStruct((bsz, t, MIX_W), F32),
                   jax.ShapeDtypeStruct((bsz, N_HEADS, HEAD_DIM, HEAD_DIM), F32)],
        scratch_shapes=[pltpu.VMEM((grp, MIX_W, MIX_W), F32)],
        compiler_params=_params("parallel", "arbitrary"),
        name="ret_prompt",
    )(p.reshape(bsz, t, 4 * MIX_W), inv)
    return y.reshape(n, MIX_W), s


def _mlstm_prompt_kernel(p_ref, pg_ref, cw_ref, cb_ref, gb_ref, ng_ref,
                         y_ref, conv_ref, c_ref, n_ref, m_ref, prev_ref, cbd_ref, nrow_ref, mrow_ref):
    p = p_ref[...]
    L = p.shape[0]
    W = MIX_W
    raw = p[:, 0:2 * W]
    prev = prev_ref[...]
    acc = cw_ref[MLSTM_CONV - 1:MLSTM_CONV, :] * raw
    for j in range(1, MLSTM_CONV):
        acc = acc + cw_ref[MLSTM_CONV - 1 - j:MLSTM_CONV - j, :] * _shift_rows(raw, prev, j)
    prev_ref[...] = raw
    conv_ref[...] = raw[L - 8:L, :]
    qk = jax.nn.silu(acc + cb_ref[...])
    q_all = qk[:, 0:W]
    k_all = qk[:, W:2 * W] * HEAD_DIM ** -0.5
    v_all = p[:, 2 * W:3 * W]
    og = p[:, 3 * W:4 * W]

    gb = pg_ref[...] + gb_ref[...]
    bcum = _cumsum_rows(jax.nn.log_sigmoid(gb))
    gb_t = gb.T
    bcum_t = bcum.T
    spread = lambda a, base: jnp.concatenate(
        [jnp.broadcast_to(a[:, base + h:base + h + 1], (L, HEAD_DIM)) for h in range(N_HEADS)], axis=1)
    rows = lambda a_t, base: jnp.concatenate([a_t[base + h:base + h + 1, :] for h in range(N_HEADS)], axis=1)
    ig_col, b_col = spread(gb, 0), spread(bcum, N_HEADS)
    ig_row, b_row = rows(gb_t, 0), rows(bcum_t, N_HEADS)
    same_head = _same_head_mask(W)
    causal = _iota((L, W), 0) >= (_iota((L, W), 1) % HEAD_DIM)
    m_prev = mrow_ref[...]
    log_d = jnp.where(causal, b_col - b_row + ig_row, MASK_NEG)
    m_inter = b_col + m_prev
    row_max = jnp.concatenate(
        [jnp.broadcast_to(jnp.max(log_d[:, h * HEAD_DIM:(h + 1) * HEAD_DIM], axis=1, keepdims=True), (L, HEAD_DIM))
         for h in range(N_HEADS)], axis=1)
    m_t = jnp.maximum(m_inter, row_max)
    c0 = cbd_ref[...]
    n0 = nrow_ref[...]
    yield
    sc = _dot_nt(q_all, _block_diag(k_all, same_head), X1) * jnp.exp(log_d - m_t)
    w_inter = jnp.exp(m_inter - m_t)
    inter = w_inter * _dot(q_all, c0, X1)
    m_l = m_t[L - 1:L, :]
    b_l = b_col[L - 1:L, :]
    a_prev = jnp.exp(b_l + m_prev - m_l)
    ks = k_all * jnp.exp(ig_col + b_l - b_col - m_l)
    yield
    num = _dot(sc, _block_diag(v_all, same_head), X1) + inter
    c_new = c0 * a_prev + jnp.where(same_head, _dot_tn(ks, v_all, X1), 0.0)
    yield
    den = HEAD_DIM * (_head_mean_lanes(sc, same_head) + w_inter * _head_mean_lanes(q_all * n0, same_head))
    hh = num / jnp.maximum(jnp.abs(den), jnp.exp(-m_t))
    n_new = n0 * a_prev + jnp.sum(ks, axis=0, keepdims=True)
    yield
    cbd_ref[...] = c_new
    nrow_ref[...] = n_new
    mrow_ref[...] = m_l
    m_out = jnp.zeros(m_ref.shape, F32)
    m_first = jnp.logical_and(_iota(m_ref.shape, 0) == 0, _iota(m_ref.shape, 1) < N_HEADS)
    for h in range(N_HEADS):
        sl = slice(h * HEAD_DIM, (h + 1) * HEAD_DIM)
        c_ref[h] = c_new[sl, sl]
        n_ref[h:h + 1, :] = n_new[:, sl]
        m_out = jnp.where(jnp.logical_and(m_first, _iota(m_ref.shape, 1) == h), m_l[:, h * HEAD_DIM:h * HEAD_DIM + 1], m_out)
    m_ref[...] = m_out
    hn = _head_norm_lanes(hh, same_head, HEAD_NORM_EPS, True) * ng_ref[...]
    y_ref[...] = jax.nn.sigmoid(og) * hn


def _mlstm_prompt_call(p, pg, bsz, conv_w, conv_b, gate_b, norm_g):
    n = p.shape[0]
    t = n // bsz
    grp = _seq_group(bsz)
    outs = pl.pallas_call(
        _per_sequence(_mlstm_prompt_kernel, (True, True) + (False,) * 4 + (True,) * 9, carried=(11, 12, 13, 14)),
        grid=(bsz // grp, t // CHUNK),
        in_specs=[_chunk_specs(grp, 4 * MIX_W), _chunk_specs(grp, LANES), _full(conv_w.shape),
                  _full(conv_b.shape), _full(gate_b.shape), _full(norm_g.shape)],
        out_specs=[_chunk_specs(grp, MIX_W),
                   _per_seq_spec(grp, 8, 2 * MIX_W),
                   _state_spec(grp),
                   _per_seq_spec(grp, N_HEADS, HEAD_DIM),
                   _per_seq_spec(grp, 8, LANES)],
        out_shape=[jax.ShapeDtypeStruct((bsz, t, MIX_W), F32),
                   jax.ShapeDtypeStruct((bsz, 8, 2 * MIX_W), F32),
                   jax.ShapeDtypeStruct((bsz, N_HEADS, HEAD_DIM, HEAD_DIM), F32),
                   jax.ShapeDtypeStruct((bsz, N_HEADS, HEAD_DIM), F32),
                   jax.ShapeDtypeStruct((bsz, 8, LANES), F32)],
        scratch_shapes=[pltpu.VMEM((grp, CHUNK, 2 * MIX_W), F32), pltpu.VMEM((grp, MIX_W, MIX_W), F32),
                        pltpu.VMEM((grp, 1, MIX_W), F32), pltpu.VMEM((grp, 1, MIX_W), F32)],
        compiler_params=_params("parallel", "arbitrary"),
        name="mlstm_prompt",
    )(p.reshape(bsz, t, 4 * MIX_W), pg.reshape(bsz, t, LANES), conv_w, conv_b, gate_b, norm_g)
    return (outs[0].reshape(n, MIX_W),) + tuple(outs[1:])


def _head_rows(ref, base, h):
    return ref[pl.ds(pl.multiple_of(base + h * HEAD_DIM, HEAD_DIM), HEAD_DIM), :]


def _rwkv_sample_kernel(p_ref, sh_ref, s_ref, mu_ref, w0_ref, wup_ref, a0_ref, aup_ref, gup_ref, kk_ref, ka_ref,
                        rk_ref, lng_ref, lnb_ref, y_ref, so_ref, v_scr, y_scr):
    h = pl.program_id(0)
    W = MIX_W

    def xs_rows(lo, size, dyn):
        idx = pl.ds(pl.multiple_of(lo + h * HEAD_DIM, HEAD_DIM), size) if dyn else pl.ds(lo, size)
        pv = p_ref[idx, :]
        return pv + mu_ref[idx, :] * (sh_ref[idx, :] - pv)

    r = xs_rows(0, HEAD_DIM, True)
    k = xs_rows(W, HEAD_DIM, True)
    v = xs_rows(2 * W, HEAD_DIM, True)
    xw = xs_rows(3 * W, RWKV_W_RANK, False)
    xa = xs_rows(3 * W + RWKV_W_RANK, RWKV_A_RANK, False)
    xg = xs_rows(3 * W + RWKV_W_RANK + RWKV_A_RANK, RWKV_G_RANK, False)
    w = w0_ref[...] + _dot(wup_ref[...], jnp.tanh(xw), HI)
    wd = jnp.exp(-jnp.exp(-jax.nn.softplus(-w) - 0.5))
    a = jax.nn.sigmoid(a0_ref[...] + _dot(aup_ref[...], xa, HI))
    g = _dot(gup_ref[...], jax.nn.sigmoid(xg), HI)
    kk = k * kk_ref[...]
    kk = kk / jnp.maximum(jnp.sqrt(jnp.sum(kk * kk, axis=0, keepdims=True)), 1e-12)
    kq = k * (1.0 + (a - 1.0) * ka_ref[...])
    bb = kk * a
    v_scr[...] = v

    def body(i, carry):
        sv = s_ref[i]
        sa = -jnp.sum(sv * kk, axis=0, keepdims=True)
        snew = sv * wd + sa * bb + v_scr[pl.ds(i, 1), :] * kq
        so_ref[i] = snew
        y_scr[pl.ds(i, 1), :] = jnp.sum(snew * r, axis=0, keepdims=True)
        return carry

    lax.fori_loop(0, HEAD_DIM, body, 0)
    y = _head_norm_cols(y_scr[...], RWKV_GN_EPS, True) * lng_ref[...] + lnb_ref[...]
    y = y + jnp.sum(r * kq * rk_ref[...], axis=0, keepdims=True) * v
    y_ref[...] = y * g


def _head_block(cols):
    return pl.BlockSpec((HEAD_DIM, cols), lambda h: (h, 0))


def _state_block(bsz):
    return pl.BlockSpec((None, HEAD_DIM, HEAD_DIM, bsz), lambda h: (h, 0, 0, 0))


def _rwkv_sample_call(p_t, shift_t, s_t, prm):
    bsz = p_t.shape[1]
    return pl.pallas_call(
        _rwkv_sample_kernel,
        grid=(N_HEADS,),
        in_specs=[_full(p_t.shape), _full(shift_t.shape), _state_block(bsz), _full((RWKV_COLS, 1)),
                  _head_block(1), _head_block(RWKV_W_RANK), _head_block(1), _head_block(RWKV_A_RANK),
                  _head_block(RWKV_G_RANK), _head_block(1), _head_block(1), _head_block(1), _head_block(1),
                  _head_block(1)],
        out_specs=[_head_block(bsz), _state_block(bsz)],
        out_shape=[jax.ShapeDtypeStruct((MIX_W, bsz), F32), jax.ShapeDtypeStruct(s_t.shape, F32)],
        scratch_shapes=[pltpu.VMEM((HEAD_DIM, bsz), F32), pltpu.VMEM((HEAD_DIM, bsz), F32)],
        compiler_params=_params("parallel"),
        name="rwkv_sample",
    )(p_t, shift_t, s_t, prm["mu_c"], prm["w0_c"], prm["wup_t"], prm["a0_c"], prm["aup_t"], prm["gup_t"],
      prm["kk_c"], prm["ka_c"], prm["rk_c"], prm["lng_c"], prm["lnb_c"])


def _kv_state_step(s_ref, so_ref, q_scr, k_scr, f_scr, v, decay):
    def body(i, o):
        dec = f_scr[pl.ds(i, 1), :] if decay is None else decay
        snew = s_ref[i] * dec + k_scr[pl.ds(i, 1), :] * v
        so_ref[i] = snew
        return o + q_scr[pl.ds(i, 1), :] * snew

    return lax.fori_loop(0, HEAD_DIM, body, jnp.zeros_like(v))


def _hgrn_sample_kernel(layer, p_ref, s_ref, lg_ref, ng_ref, y_ref, so_ref, q_scr, k_scr, f_scr):
    h = pl.program_id(0)
    W = MIX_W
    lg = lg_ref[...]
    l0, l1 = lg[:, 0:1], lg[:, 1:2]
    m = jnp.maximum(l0, l1)
    e0, e1 = jnp.exp(l0 - m), jnp.exp(l1 - m)
    z = e0 + e1
    lb0, lb1 = e0 / z, e1 / z
    lb = (lb0 - lb0) if layer == 0 else ((lb0 + lb1) - lb0)
    log_f = _hgrn_log_f(_head_rows(p_ref, W, h), lb)
    f = jnp.exp(log_f)
    q_scr[...] = jax.nn.silu(_head_rows(p_ref, 0, h))
    k_scr[...] = 1.0 - f
    f_scr[...] = f
    v = _head_rows(p_ref, 2 * W, h)
    g = _head_rows(p_ref, 3 * W, h)
    o = _kv_state_step(s_ref, so_ref, q_scr, k_scr, f_scr, v, None)
    y_ref[...] = _head_norm_cols(o, HEAD_NORM_EPS, False) * ng_ref[...] * jax.nn.silu(g)


def _hgrn_sample_call(p_t, s_t, logits_c, norm_g_c, layer):
    bsz = p_t.shape[1]
    return pl.pallas_call(
        functools.partial(_hgrn_sample_kernel, layer),
        grid=(N_HEADS,),
        in_specs=[_full(p_t.shape), _state_block(bsz), _head_block(2), _head_block(1)],
        out_specs=[_head_block(bsz), _state_block(bsz)],
        out_shape=[jax.ShapeDtypeStruct((MIX_W, bsz), F32), jax.ShapeDtypeStruct(s_t.shape, F32)],
        scratch_shapes=[pltpu.VMEM((HEAD_DIM, bsz), F32)] * 3,
        compiler_params=_params("parallel"),
        name="hgrn_sample",
    )(p_t, s_t, logits_c, norm_g_c)


def _ret_sample_kernel(pos0, p_ref, s_ref, inv_ref, y_ref, so_ref, q_scr, k_scr):
    h = pl.program_id(0)
    W = MIX_W
    ang = jnp.float32(pos0) * inv_ref[...]
    cosv, sinv = jnp.cos(ang), jnp.sin(ang)

    def rot(z):
        z1, z2 = z[0:HALF_DIM, :], z[HALF_DIM:HEAD_DIM, :]
        return jnp.concatenate([z1 * cosv - z2 * sinv, z1 * sinv + z2 * cosv], axis=0)

    q_scr[...] = rot(_head_rows(p_ref, 0, h))
    k_scr[...] = rot(_head_rows(p_ref, W, h)) * HEAD_DIM ** -0.5
    v = _head_rows(p_ref, 2 * W, h)
    g = _head_rows(p_ref, 3 * W, h)
    gamma = jnp.float32(math.exp(RET_LOG_GAMMA[N_HEADS - 1]))
    for hh in range(N_HEADS - 2, -1, -1):
        gamma = jnp.where(h == hh, jnp.float32(math.exp(RET_LOG_GAMMA[hh])), gamma)
    o = _kv_state_step(s_ref, so_ref, q_scr, k_scr, None, v, gamma)
    y_ref[...] = _head_norm_cols(o, HEAD_NORM_EPS, False) * jax.nn.silu(g)


def _ret_sample_call(p_t, s_t, pos0):
    bsz = p_t.shape[1]
    inv = RET_THETA ** (-np.arange(HALF_DIM).astype(np.float64) / HALF_DIM)
    inv = jnp.asarray(inv.astype(np.float32).reshape(HALF_DIM, 1))
    return pl.pallas_call(
        functools.partial(_ret_sample_kernel, pos0),
        grid=(N_HEADS,),
        in_specs=[_full(p_t.shape), _state_block(bsz), _full(inv.shape)],
        out_specs=[_head_block(bsz), _state_block(bsz)],
        out_shape=[jax.ShapeDtypeStruct((MIX_W, bsz), F32), jax.ShapeDtypeStruct(s_t.shape, F32)],
        scratch_shapes=[pltpu.VMEM((HEAD_DIM, bsz), F32)] * 2,
        compiler_params=_params("parallel"),
        name="ret_sample",
    )(p_t, s_t, inv)


def _mlstm_sample_kernel(p_ref, pg_ref, conv_ref, c_ref, n_ref, m_ref, cw_ref, cb_ref, igb_ref, fgb_ref, ng_ref,
                         y_ref, co_ref, no_ref, mo_ref, q_scr, k_scr):
    h = pl.program_id(0)
    W = MIX_W

    def conv_rows(base):
        idx = pl.ds(pl.multiple_of(base + h * HEAD_DIM, HEAD_DIM), HEAD_DIM)
        cw = cw_ref[idx, :]
        acc = cw[:, MLSTM_CONV - 1:MLSTM_CONV] * p_ref[idx, :]
        for j in range(MLSTM_CONV - 1):
            acc = acc + cw[:, j:j + 1] * conv_ref[j, idx, :]
        return jax.nn.silu(acc + cb_ref[idx, :])

    q = conv_rows(0)
    k = conv_rows(W) * HEAD_DIM ** -0.5
    v = _head_rows(p_ref, 2 * W, h)
    og = _head_rows(p_ref, 3 * W, h)
    ig = pg_ref[pl.ds(h, 1), :] + igb_ref[pl.ds(h, 1), :]
    lf = jax.nn.log_sigmoid(pg_ref[pl.ds(N_HEADS + h, 1), :] + fgb_ref[pl.ds(h, 1), :])
    m_prev = m_ref[pl.ds(h, 1), :]
    n0 = n_ref[...]
    m_inter = lf + m_prev
    m_t = jnp.maximum(m_inter, ig)
    sc = jnp.sum(q * k, axis=0, keepdims=True) * jnp.exp(ig - m_t)
    w_inter = jnp.exp(m_inter - m_t)
    a_prev = jnp.exp(lf + m_prev - m_t)
    a_s = jnp.exp(ig + lf - lf - m_t)
    q_scr[...] = q
    k_scr[...] = k * a_s

    def body(i, qc):
        c0 = c_ref[i]
        co_ref[i] = c0 * a_prev + k_scr[pl.ds(i, 1), :] * v
        return qc + q_scr[pl.ds(i, 1), :] * c0

    qc = lax.fori_loop(0, HEAD_DIM, body, jnp.zeros_like(v))
    num = sc * v + w_inter * qc
    den = sc + w_inter * jnp.sum(q * n0, axis=0, keepdims=True)
    hh = num / jnp.maximum(jnp.abs(den), jnp.exp(-m_t))
    no_ref[...] = n0 * a_prev + k * a_s
    mo_ref[pl.ds(h, 1), :] = m_t
    y_ref[...] = jax.nn.sigmoid(og) * (_head_norm_cols(hh, HEAD_NORM_EPS, True) * ng_ref[...])


def _mlstm_sample_call(p_t, pg_t, conv_t, c_t, n_t, m_t, prm):
    bsz = p_t.shape[1]
    nblk = pl.BlockSpec((None, HEAD_DIM, bsz), lambda h: (h, 0, 0))
    return pl.pallas_call(
        _mlstm_sample_kernel,
        grid=(N_HEADS,),
        in_specs=[_full(p_t.shape), _full(pg_t.shape), _full(conv_t.shape), _state_block(bsz), nblk,
                  _full(m_t.shape), _full((2 * MIX_W, MLSTM_CONV)), _full((2 * MIX_W, 1)),
                  _full((N_HEADS, 1)), _full((N_HEADS, 1)), _head_block(1)],
        out_specs=[_head_block(bsz), _state_block(bsz), nblk, _full(m_t.shape)],
        out_shape=[jax.ShapeDtypeStruct((MIX_W, bsz), F32), jax.ShapeDtypeStruct(c_t.shape, F32),
                   jax.ShapeDtypeStruct(n_t.shape, F32), jax.ShapeDtypeStruct(m_t.shape, F32)],
        scratch_shapes=[pltpu.VMEM((HEAD_DIM, bsz), F32)] * 2,
        compiler_params=_params("arbitrary"),
        name="mlstm_sample",
    )(p_t, pg_t, conv_t, c_t, n_t, m_t, prm["cw_t"], prm["cb_c"], prm["igb_c"], prm["fgb_c"], prm["ng_c"])


def _merge_kernel(x_ref, xn_ref, ya_ref, yb_ref, yc_ref, yd_ref, wg_ref, wb_ref, wo_ref, g2_ref, x1_ref, xn2_ref):
    gates = jax.nn.sigmoid(_dot(xn_ref[...], wg_ref[...]))
    merged = None
    for m, y_ref in enumerate((ya_ref, yb_ref, yc_ref, yd_ref)):
        br = _dot(y_ref[...].astype(BF16), wb_ref[m])
        term = gates[:, m * D_MODEL:(m + 1) * D_MODEL] * br
        merged = term if merged is None else merged + term
    x1 = x_ref[...] + _dot(merged.astype(BF16), wo_ref[...])
    x1_ref[...] = x1
    xn2 = x1 * lax.rsqrt(jnp.mean(x1 * x1, axis=-1, keepdims=True) + NORM_EPS) * g2_ref[...]
    xn2_ref[...] = xn2.T.astype(BF16)


def _merge_call(x2d, xn, ys, w_gate, w_branch, w_out, g2, tm):
    n = x2d.shape[0]
    row = lambda i: (i, 0)
    return pl.pallas_call(
        _merge_kernel,
        grid=(n // tm,),
        in_specs=[pl.BlockSpec((tm, D_MODEL), row)] * 2 + [pl.BlockSpec((tm, MIX_W), row)] * 4
        + [_full(w_gate.shape), _full(w_branch.shape), _full(w_out.shape), _full((1, D_MODEL))],
        out_specs=[pl.BlockSpec((tm, D_MODEL), row), pl.BlockSpec((D_MODEL, tm), lambda i: (0, i))],
        out_shape=[jax.ShapeDtypeStruct((n, D_MODEL), F32), jax.ShapeDtypeStruct((D_MODEL, n), BF16)],
        compiler_params=_params("parallel"),
        name="merge_out",
    )(x2d, xn, *ys, w_gate, w_branch, w_out, g2.reshape(1, D_MODEL))


def _top_ranked(s, count, break_ties):
    idx = _iota(s.shape, 0)
    rank = jnp.full(s.shape, float(count), F32)
    vals = []
    for r in range(count):
        mx = jnp.max(s, axis=0, keepdims=True)
        sel = s == mx
        if break_ties:
            sel = idx == jnp.min(jnp.where(sel, idx, s.shape[0]), axis=0, keepdims=True)
        rank = jnp.where(sel, float(r), rank)
        s = jnp.where(sel, NEG_BIG, s)
        vals.append(mx)
    ranked = jnp.sum((rank < float(count)).astype(F32), axis=0, keepdims=True)
    return vals, rank, ranked


_LATTICE_WIDTHS = tuple(PEER_TOPK // (i + 1) for i in range(PEER_TOPK))


def _peer_head_stats(s1, s2, break_ties):
    v1, rank1, n1 = _top_ranked(s1, PEER_TOPK, break_ties)
    v2, rank2, n2 = _top_ranked(s2, PEER_TOPK, break_ties)
    rows = [v1[i] + v2[j] for i in range(PEER_TOPK) for j in range(_LATTICE_WIDTHS[i])]
    pad = (-len(rows)) % 8
    cand = jnp.concatenate(rows + [jnp.full((pad, s1.shape[1]), NEG_BIG, F32)], axis=0)
    _, crank, _ = _top_ranked(cand, PEER_TOPK, True)
    chosen = crank < float(PEER_TOPK)
    z = jnp.sum(jnp.where(chosen, jnp.exp(cand - rows[0]), 0.0), axis=0, keepdims=True)
    taken = chosen.astype(F32)
    lim = jnp.zeros_like(s1)
    off = 0
    for i in range(PEER_TOPK):
        count_i = jnp.sum(taken[off:off + _LATTICE_WIDTHS[i], :], axis=0, keepdims=True)
        lim = jnp.where(rank1 == float(i), count_i, lim)
        off += _LATTICE_WIDTHS[i]
    tied = jnp.maximum(n1, n2) - float(PEER_TOPK)
    stats = (lim, jnp.exp(s1 - v1[0]) / z, rank2.astype(BF16), jnp.exp(s2 - v2[0]).astype(BF16))
    return stats, tied


def _peer_score_kernel(xn_ref, wq_ref, keys_ref, lim_ref, cf_ref, r2_ref, e2_ref, s_scr):
    q_t = _dot(wq_ref[...], xn_ref[...])
    out_refs = (lim_ref, cf_ref, r2_ref, e2_ref)
    tied = jnp.zeros((1, q_t.shape[1]), F32)
    for h in range(PEER_HEADS):
        base = h * 2 * PEER_QHALF
        s1 = _dot(keys_ref[h, 0], q_t[base:base + PEER_QHALF, :], HI)
        s2 = _dot(keys_ref[h, 1], q_t[base + PEER_QHALF:base + 2 * PEER_QHALF, :], HI)
        s_scr[h, 0] = s1
        s_scr[h, 1] = s2
        stats, tied_h = _peer_head_stats(s1, s2, False)
        for ref, val in zip(out_refs, stats):
            ref[h] = val
        tied = jnp.maximum(tied, tied_h)

    @pl.when(jnp.max(tied) > 0.0)
    def _():
        for h in range(PEER_HEADS):
            exact, _ = _peer_head_stats(s_scr[h, 0], s_scr[h, 1], True)
            for ref, val in zip(out_refs, exact):
                ref[h] = val


def _peer_score_call(xn2_t, wq_t, keys, tm):
    n = xn2_t.shape[1]
    blk = pl.BlockSpec((PEER_HEADS, PEER_N_KEYS, tm), lambda i: (0, 0, i))
    big = lambda dt: jax.ShapeDtypeStruct((PEER_HEADS, PEER_N_KEYS, n), dt)
    return pl.pallas_call(
        _peer_score_kernel,
        grid=(n // tm,),
        in_specs=[pl.BlockSpec((D_MODEL, tm), lambda i: (0, i)), _full(wq_t.shape), _full(keys.shape)],
        out_specs=[blk] * 4,
        out_shape=[big(F32), big(F32), big(BF16), big(BF16)],
        scratch_shapes=[pltpu.VMEM((PEER_HEADS, 2, PEER_N_KEYS, tm), F32)],
        compiler_params=_params("parallel"),
        name="peer_scores",
    )(xn2_t, wq_t, keys)


def _peer_expert_kernel(xn_ref, x1_ref, u_ref, vt_ref, lim_ref, cf_ref, r2_ref, e2_ref, o_ref, acc_ref):
    j = pl.program_id(1)

    @pl.when(j == 0)
    def _():
        acc_ref[...] = jnp.zeros_like(acc_ref)

    te = u_ref.shape[0]
    tm = xn_ref.shape[1]
    per_tile = te // PEER_N_KEYS
    pieces = []
    for al in range(per_tile):
        a = j * per_tile + al
        hid = _dot(u_ref[al * PEER_N_KEYS:(al + 1) * PEER_N_KEYS, :], xn_ref[...])
        act = hid * (1.0 + lax.erf(hid * INV_SQRT2))
        grouped = (PEER_N_KEYS // BF16_ROWS, BF16_ROWS, tm)
        gate = jnp.zeros(grouped, BF16)
        for h in range(PEER_HEADS):
            lim = jnp.broadcast_to(lim_ref[h, pl.ds(a, 1), :], (BF16_ROWS, tm)).astype(BF16)
            cf = jnp.broadcast_to(0.5 * cf_ref[h, pl.ds(a, 1), :], (BF16_ROWS, tm)).astype(BF16)
            r2 = r2_ref[h].reshape(grouped)
            e2 = e2_ref[h].reshape(grouped)
            gate = gate + jnp.where(r2 < lim[None], e2 * cf[None], jnp.zeros((), BF16))
        pieces.append((act * gate.reshape(PEER_N_KEYS, tm).astype(F32)).astype(BF16))
    acc_ref[...] += _dot(vt_ref[...], jnp.concatenate(pieces, axis=0))

    @pl.when(j == pl.num_programs(1) - 1)
    def _():
        o_ref[...] = x1_ref[...] + acc_ref[...].T


def _peer_expert_call(xn2_t, x1, u, v_t, stats, tm):
    n = xn2_t.shape[1]
    n_exp = u.shape[0]
    te = v_t.shape[2]
    tok = pl.BlockSpec((tm, D_MODEL), lambda i, j: (i, 0))
    blk = pl.BlockSpec((PEER_HEADS, PEER_N_KEYS, tm), lambda i, j: (0, 0, i))
    return pl.pallas_call(
        _peer_expert_kernel,
        grid=(n // tm, n_exp // te),
        in_specs=[pl.BlockSpec((D_MODEL, tm), lambda i, j: (0, i)), tok,
                  pl.BlockSpec((te, D_MODEL), lambda i, j: (j, 0)),
                  pl.BlockSpec((None, D_MODEL, te), lambda i, j: (j, 0, 0)),
                  blk, blk, blk, blk],
        out_specs=tok,
        out_shape=jax.ShapeDtypeStruct((n, D_MODEL), F32),
        scratch_shapes=[pltpu.VMEM((D_MODEL, tm), F32)],
        compiler_params=_params("parallel", "arbitrary"),
        name="peer_experts",
    )(xn2_t, x1, u, v_t, *stats)


def _final_norm_kernel(x_ref, g_ref, o_ref):
    x = x_ref[...]
    o_ref[...] = x * lax.rsqrt(jnp.mean(x * x, axis=-1, keepdims=True) + NORM_EPS) * g_ref[...]


def _final_norm_call(x2d, g, tm):
    n = x2d.shape[0]
    row = lambda i: (i, 0)
    return pl.pallas_call(
        _final_norm_kernel,
        grid=(n // tm,),
        in_specs=[pl.BlockSpec((tm, D_MODEL), row), _full((1, D_MODEL))],
        out_specs=pl.BlockSpec((tm, D_MODEL), row),
        out_shape=jax.ShapeDtypeStruct((n, D_MODEL), F32),
        compiler_params=_params("parallel"),
        name="final_norm",
    )(x2d, g.reshape(1, D_MODEL))


def _layer_weights(l, w_in, rw, hg, ml, w_branch, w_out, ln1_g, ln2_g, peer_w_q, peer_keys, peer_u, peer_v):
    o3 = RWKV_COLS + 2 * 4 * MIX_W
    o4 = o3 + MLSTM_COLS
    gcol = o3 + 3 * MIX_W
    w = w_in[l]
    w_main = jnp.concatenate([w[:, 0:gcol], w[:, gcol + 2 * N_HEADS:o4]], axis=1).astype(BF16)
    w_mlg = jnp.pad(w[:, gcol:gcol + 2 * N_HEADS], ((0, 0), (0, LANES - 2 * N_HEADS)))
    row = lambda a: a[l].reshape(1, -1)
    col = lambda a: a[l].reshape(-1, 1)
    rwkv = dict(mu=row(rw["mu"]), w0=row(rw["w0"]), wup=rw["w_up"][l], a0=row(rw["a0"]), aup=rw["a_up"][l],
                gup=rw["g_up"][l], kk=row(rw["k_k"]), ka=row(rw["k_a"]), rk=row(rw["r_k"]),
                lng=row(rw["ln_g"]), lnb=row(rw["ln_b"]),
                mu_c=col(rw["mu"]), w0_c=col(rw["w0"]), wup_t=rw["w_up"][l].T, a0_c=col(rw["a0"]),
                aup_t=rw["a_up"][l].T, gup_t=rw["g_up"][l].T, kk_c=col(rw["k_k"]), ka_c=col(rw["k_a"]),
                rk_c=col(rw["r_k"]), lng_c=col(rw["ln_g"]), lnb_c=col(rw["ln_b"]))
    gate_b = jnp.pad(jnp.concatenate([ml["ig_b"][l], ml["fg_b"][l]]), (0, LANES - 2 * N_HEADS)).reshape(1, LANES)
    mlstm = dict(cw=ml["conv_w"][l], cb=row(ml["conv_b"]), gate_b=gate_b, ng=row(ml["norm_g"]),
                 cw_t=ml["conv_w"][l].T, cb_c=col(ml["conv_b"]), igb_c=col(ml["ig_b"]), fgb_c=col(ml["fg_b"]),
                 ng_c=col(ml["norm_g"]))
    return dict(
        ln1=ln1_g[l], ln2=ln2_g[l], w_main=w_main, w_mlg=w_mlg, w_gate=w[:, o4:].astype(BF16),
        w_branch=w_branch[l].astype(BF16), w_out=w_out[l].astype(BF16),
        rwkv=rwkv, hgrn_logits=hg["lb_logits"], hgrn_logits_c=hg["lb_logits"].T,
        hgrn_ng=row(hg["norm_g"]), hgrn_ng_c=col(hg["norm_g"]), mlstm=mlstm,
        wq_t=peer_w_q[l].T.astype(BF16), keys=peer_keys[l], u=peer_u[l].astype(BF16),
        v_t=jnp.swapaxes(peer_v[l].astype(BF16).reshape(-1, PEER_EXPERT_TILE, D_MODEL), 1, 2))


def _lanes_last(a):
    return jnp.moveaxis(a, 0, -1)


def _prompt_mixers(proj, bsz, lw, layer):
    _, pr, ph, pt, pm, pg = proj
    ya, shift, s_a = _rwkv_prompt_call(pr, bsz, lw["rwkv"])
    yb, s_b = _hgrn_prompt_call(ph, bsz, lw["hgrn_logits"], lw["hgrn_ng"], layer)
    yc, s_c = _ret_prompt_call(pt, bsz, 0)
    m = lw["mlstm"]
    yd, conv, c_d, n_d, m_d = _mlstm_prompt_call(pm, pg, bsz, m["cw"], m["cb"], m["gate_b"], m["ng"])
    states = (shift[:, 0, :], s_a, s_b, s_c, conv[:, 8 - (MLSTM_CONV - 1):, :], c_d, n_d, m_d[:, 0, :N_HEADS])
    return (ya, yb, yc, yd), states


def _sample_mixers(proj, st, lw, layer, pos0):
    _, pr, ph, pt, pm, pg = proj
    shift, s_a, s_b, s_c, conv, c_d, n_d, m_d = st
    to_t = lambda a: jnp.moveaxis(a, 0, -1)
    from_t = lambda a: jnp.moveaxis(a, -1, 0)
    ya, s_a2 = _rwkv_sample_call(pr.T, shift.T, to_t(s_a), lw["rwkv"])
    yb, s_b2 = _hgrn_sample_call(ph.T, to_t(s_b), lw["hgrn_logits_c"], lw["hgrn_ng_c"], layer)
    yc, s_c2 = _ret_sample_call(pt.T, to_t(s_c), pos0)
    yd, c_d2, n_d2, m_d2 = _mlstm_sample_call(pm.T, pg.T, to_t(conv), to_t(c_d), to_t(n_d), m_d.T, lw["mlstm"])
    conv2 = jnp.concatenate([conv[:, 1:, :], pm[:, None, 0:2 * MIX_W]], axis=1)
    states = (pr, from_t(s_a2), from_t(s_b2), from_t(s_c2), conv2, from_t(c_d2), from_t(n_d2), m_d2.T)
    return (ya.T, yb.T, yc.T, yd.T), states


def _token_tiles(n):
    pick = lambda want: want if n % want == 0 else n
    return pick(256), pick(256), pick(512)


def _run_trunk(x, states, pos0, weights, lnf_g, prompt):
    bsz, t, _ = x.shape
    n = bsz * t
    x2d = x.reshape(n, D_MODEL)
    tm, tm_score, tm_peer = _token_tiles(n)
    new_states = []
    for layer, lw in enumerate(weights):
        proj = _proj_call(x2d, lw["ln1"], lw["w_main"], lw["w_mlg"], tm)
        if prompt:
            ys, st_new = _prompt_mixers(proj, bsz, lw, layer)
        else:
            ys, st_new = _sample_mixers(proj, tuple(s[layer] for s in states), lw, layer, pos0)
        new_states.append(st_new)
        x1, xn2 = _merge_call(x2d, proj[0], ys, lw["w_gate"], lw["w_branch"], lw["w_out"], lw["ln2"], tm)
        stats = _peer_score_call(xn2, lw["wq_t"], lw["keys"], tm_score)
        x2d = _peer_expert_call(xn2, x1, lw["u"], lw["v_t"], stats, tm_peer)
    y = _final_norm_call(x2d, lnf_g, tm).reshape(bsz, t, D_MODEL)
    stacked = tuple(jnp.stack([s[i] for s in new_states], axis=0) for i in range(8))
    return y, stacked


def kernel(x_prompt, x_sample, state_rwkv_shift, state_rwkv_S, state_hgrn_S, state_ret_S, state_mlstm_conv, state_mlstm_C, state_mlstm_n, state_mlstm_m, ln1_g, w_in, rwkv_mu, rwkv_w0, rwkv_w_up, rwkv_a0, rwkv_a_up, rwkv_g_up, rwkv_k_k, rwkv_k_a, rwkv_r_k, rwkv_ln_g, rwkv_ln_b, hgrn_lb_logits, hgrn_norm_g, mlstm_conv_w, mlstm_conv_b, mlstm_ig_b, mlstm_fg_b, mlstm_norm_g, w_branch, w_out, ln2_g, peer_w_q, peer_keys, peer_u, peer_v, lnf_g):
    depth = w_in.shape[0]
    rw = dict(mu=rwkv_mu, w0=rwkv_w0, w_up=rwkv_w_up, a0=rwkv_a0, a_up=rwkv_a_up, g_up=rwkv_g_up,
              k_k=rwkv_k_k, k_a=rwkv_k_a, r_k=rwkv_r_k, ln_g=rwkv_ln_g, ln_b=rwkv_ln_b)
    hg = dict(lb_logits=hgrn_lb_logits, norm_g=hgrn_norm_g)
    ml = dict(conv_w=mlstm_conv_w, conv_b=mlstm_conv_b, ig_b=mlstm_ig_b, fg_b=mlstm_fg_b, norm_g=mlstm_norm_g)
    weights = [_layer_weights(l, w_in, rw, hg, ml, w_branch, w_out, ln1_g, ln2_g, peer_w_q, peer_keys,
                              peer_u, peer_v) for l in range(depth)]
    past_len = 16384
    y_prompt, p_states = _run_trunk(x_prompt, None, 0, weights, lnf_g, True)
    sample_init = (state_rwkv_shift, state_rwkv_S, state_hgrn_S, state_ret_S, state_mlstm_conv,
                   state_mlstm_C, state_mlstm_n, state_mlstm_m)
    y_sample, s_states = _run_trunk(x_sample, sample_init, past_len, weights, lnf_g, False)
    return (y_prompt, y_sample) + p_states + s_states
```

```python
import functools
import math

import numpy as np
import jax
import jax.numpy as jnp
from jax import lax
from jax.experimental import pallas as pl
from jax.experimental.pallas import tpu as pltpu

F32 = jnp.float32
BF16 = jnp.bfloat16
HI = lax.Precision.HIGHEST

D_MODEL = 1024
N_HEADS = 4
HEAD_DIM = 64
HALF_DIM = HEAD_DIM // 2
MIX_W = N_HEADS * HEAD_DIM
RWKV_W_RANK = 64
RWKV_A_RANK = 64
RWKV_G_RANK = 128
RWKV_COLS = 3 * MIX_W + RWKV_W_RANK + RWKV_A_RANK + RWKV_G_RANK
MLSTM_COLS = 4 * MIX_W + 2 * N_HEADS
GATE_COLS = 4 * D_MODEL
MLSTM_CONV = 4
CHUNK = 64
SUB = 16
RET_THETA = 10000.0
PEER_HEADS = 8
PEER_N_KEYS = 128
PEER_TOPK = 16
PEER_QHALF = 64
NORM_EPS = 1e-6
RWKV_GN_EPS = 64e-5
HEAD_NORM_EPS = 1e-5
MASK_NEG = -1e30
LB_FLOOR = 1e-30
NEG_BIG = -1e30
INV_SQRT2 = 1.0 / math.sqrt(2.0)
RET_LOG_GAMMA = tuple(math.log(1.0 - 2.0 ** (-5.0 - h)) for h in range(N_HEADS))
LANES = 128
BF16_ROWS = 16
PEER_EXPERT_TILE = 1024
VMEM_LIMIT = 56 * 1024 * 1024


X1 = "bf16"


def _dot_dims(a, b, dims, prec):
    if prec == X1:
        return lax.dot_general(a.astype(BF16), b.astype(BF16), dims, preferred_element_type=F32)
    return lax.dot_general(a, b, dims, precision=prec, preferred_element_type=F32)


def _dot(a, b, prec=None):
    return _dot_dims(a, b, (((1,), (0,)), ((), ())), prec)


def _dot_nt(a, b, prec=None):
    return _dot_dims(a, b, (((1,), (1,)), ((), ())), prec)


def _dot_tn(a, b, prec=None):
    return _dot_dims(a, b, (((0,), (0,)), ((), ())), prec)


def _params(*sem):
    return pltpu.CompilerParams(dimension_semantics=sem, vmem_limit_bytes=VMEM_LIMIT)


def _full(shape):
    nd = len(shape)
    return pl.BlockSpec(shape, lambda *_: (0,) * nd)


def _iota(shape, axis):
    return lax.broadcasted_iota(jnp.int32, shape, axis)


def _shift_rows(cur, prev, j):
    row = _iota(cur.shape, 0)
    return jnp.where(row >= j, pltpu.roll(cur, j, 0), pltpu.roll(prev, j, 0))


def _head_norm_rows(y, eps, center):
    if center:
        y = y - jnp.mean(y, axis=1, keepdims=True)
    return y * lax.rsqrt(jnp.mean(y * y, axis=1, keepdims=True) + eps)


def _same_head_mask(width):
    return (_iota((width, width), 0) // HEAD_DIM) == (_iota((width, width), 1) // HEAD_DIM)


def _per_head_row(values, width):
    head = _iota((1, width), 1) // HEAD_DIM
    row = jnp.full((1, width), values[-1], F32)
    for h in range(len(values) - 2, -1, -1):
        row = jnp.where(head == h, values[h], row)
    return row


def _block_diag(x, same_head):
    reps = same_head.shape[0] // x.shape[0]
    return jnp.where(same_head, jnp.concatenate([x] * reps, axis=0), 0.0)


def _head_mean_lanes(y, same_head, split=True):
    e = jnp.where(same_head, 1.0 / HEAD_DIM, 0.0).astype(BF16)
    hi = y.astype(BF16)
    if not split:
        return _dot(hi, e)
    lo = (y - hi.astype(F32)).astype(BF16)
    return _dot(jnp.concatenate([hi, lo], axis=1), jnp.concatenate([e, e], axis=0))


def _head_norm_lanes(y, same_head, eps, center):
    if center:
        y = y - _head_mean_lanes(y, same_head)
    return y * lax.rsqrt(_head_mean_lanes(y * y, same_head) + eps)


def _head_norm_cols(y, eps, center):
    if center:
        y = y - jnp.mean(y, axis=0, keepdims=True)
    return y * lax.rsqrt(jnp.mean(y * y, axis=0, keepdims=True) + eps)


def _proj_kernel(x_ref, g_ref, w_ref, wg_ref, xn_ref, pr_ref, ph_ref, pt_ref, pm_ref, pg_ref):
    x = x_ref[...]
    xn = x * lax.rsqrt(jnp.mean(x * x, axis=-1, keepdims=True) + NORM_EPS) * g_ref[...]
    xb = xn.astype(BF16)
    xn_ref[...] = xb
    p = _dot(xb, w_ref[...])
    pr_ref[...] = p[:, 0:1024]
    ph_ref[...] = p[:, 1024:2048]
    pt_ref[...] = p[:, 2048:3072]
    pm_ref[...] = p[:, 3072:4096]
    pg_ref[...] = _dot(xn, wg_ref[...], HI)


def _proj_call(x2d, g, w_main, w_mlg, tm):
    n = x2d.shape[0]
    row = lambda i: (i, 0)
    out = lambda c, dt: jax.ShapeDtypeStruct((n, c), dt)
    return pl.pallas_call(
        _proj_kernel,
        grid=(n // tm,),
        in_specs=[pl.BlockSpec((tm, D_MODEL), row), _full((1, D_MODEL)),
                  _full((D_MODEL, 4096)), _full((D_MODEL, LANES))],
        out_specs=[pl.BlockSpec((tm, D_MODEL), row)] * 5 + [pl.BlockSpec((tm, LANES), row)],
        out_shape=[out(D_MODEL, BF16)] + [out(D_MODEL, F32)] * 4 + [out(LANES, F32)],
        compiler_params=_params("parallel"),
        name="norm_proj",
    )(x2d, g.reshape(1, D_MODEL), w_main, w_mlg)


def _cumsum_rows(x):
    row = _iota(x.shape, 0)
    shift = 1
    while shift < x.shape[0]:
        x = x + jnp.where(row >= shift, pltpu.roll(x, shift, 0), 0.0)
        shift *= 2
    return x


def _rwkv_prompt_kernel(p_ref, mu_ref, w0_ref, wup_ref, a0_ref, aup_ref, gup_ref, kk_ref, ka_ref, rk_ref,
                        lng_ref, lnb_ref, y_ref, shift_ref, s_ref, prow_ref, sbd_ref):
    p = p_ref[...]
    L = p.shape[0]
    row = _iota(p.shape, 0)
    prev = jnp.where(row >= 1, pltpu.roll(p, 1, 0), jnp.broadcast_to(prow_ref[...], p.shape))
    last = p[L - 1:L, :]
    prow_ref[...] = last
    shift_ref[...] = last
    xs = p + mu_ref[...] * (prev - p)
    W = MIX_W
    r, k, v = xs[:, 0:W], xs[:, W:2 * W], xs[:, 2 * W:3 * W]
    o = 3 * W
    xw = xs[:, o:o + RWKV_W_RANK]
    o += RWKV_W_RANK
    xa = xs[:, o:o + RWKV_A_RANK]
    o += RWKV_A_RANK
    xg = xs[:, o:o + RWKV_G_RANK]
    w = w0_ref[...] + _dot(jnp.tanh(xw), wup_ref[...], X1)
    logw = -jnp.exp(-jax.nn.softplus(-w) - 0.5)
    a = jax.nn.sigmoid(a0_ref[...] + _dot(xa, aup_ref[...], X1))
    g = _dot(jax.nn.sigmoid(xg), gup_ref[...], X1)
    same_head = _same_head_mask(W)
    kk = k * kk_ref[...]
    kk = kk / jnp.maximum(jnp.sqrt(HEAD_DIM * _head_mean_lanes(kk * kk, same_head)), 1e-12)
    kq = k * (1.0 + (a - 1.0) * ka_ref[...])
    bb = kk * a

    cum = _cumsum_rows(logw)
    cum_l = cum[L - 1:L, :]
    e_neg = jnp.exp(-cum)
    e_tail = jnp.exp(cum_l - cum)
    lhs = jnp.concatenate([kk * jnp.exp(cum - logw), r * jnp.exp(cum)], axis=0)
    s0 = sbd_ref[...]
    yield
    gb = _dot_nt(lhs, _block_diag(bb * e_neg, same_head), X1)
    yield
    gk = _dot_nt(lhs, _block_diag(kq * e_neg, same_head), X1)
    qs = _dot_nt(lhs, s0, X1)
    t_idx = _iota((L, W), 0)
    s_idx = _iota((L, W), 1) % HEAD_DIM
    strict = t_idx > s_idx
    incl = t_idx >= s_idx
    n_k = jnp.where(strict, gk[0:L], 0.0)
    n_b = jnp.where(strict, gb[0:L], 0.0)
    m_k = jnp.where(incl, gk[L:2 * L], 0.0)
    m_b = jnp.where(incl, gb[L:2 * L], 0.0)
    tinv = (t_idx == s_idx).astype(F32) - n_b
    pw = n_b
    for _ in range(5):
        yield
        pw = _dot(pw, _block_diag(pw, same_head), X1)
        yield
        tinv = tinv + _dot(tinv, _block_diag(pw, same_head), X1)
    kv = _dot(jnp.concatenate([n_k, m_k], axis=0), _block_diag(v, same_head), X1)
    yield
    u = _dot(tinv, _block_diag(qs[0:L] + kv[0:L], same_head), X1)
    yield
    y = qs[L:2 * L] + kv[L:2 * L] - _dot(m_b, _block_diag(u, same_head), X1)
    upd = _dot_tn(jnp.concatenate([v, -u], axis=0),
                  jnp.concatenate([kq * e_tail, bb * e_tail], axis=0), X1)
    yield
    s_new = s0 * jnp.exp(cum_l) + jnp.where(same_head, upd, 0.0)
    sbd_ref[...] = s_new
    for h in range(N_HEADS):
        sl = slice(h * HEAD_DIM, (h + 1) * HEAD_DIM)
        s_ref[h] = s_new[sl, sl]
    y = _head_norm_lanes(y, same_head, RWKV_GN_EPS, True) * lng_ref[...] + lnb_ref[...]
    y = y + HEAD_DIM * _head_mean_lanes(r * kq * rk_ref[...], same_head) * v
    y_ref[...] = y * g


def _per_sequence(body, batched, carried):
    def kernel(*refs):
        @pl.when(pl.program_id(1) == 0)
        def _():
            for i in carried:
                refs[i][...] = jnp.zeros_like(refs[i])

        group = next(r.shape[0] for r, b in zip(refs, batched) if b)
        runs = [body(*[r.at[gi] if b else r for r, b in zip(refs, batched)]) for gi in range(group)]
        runs = [r for r in runs if r is not None]
        while runs:
            for r in list(runs):
                if next(r, _DONE) is _DONE:
                    runs.remove(r)

    return kernel


_DONE = object()


def _seq_group(bsz):
    return next(g for g in (4, 2, 1) if bsz % g == 0)


def _chunk_specs(grp, width):
    return pl.BlockSpec((grp, CHUNK, width), lambda b, c: (b, c, 0))


def _per_seq_spec(grp, *dims):
    return pl.BlockSpec((grp,) + dims, lambda b, c: (b,) + (0,) * len(dims))


def _state_spec(grp):
    return _per_seq_spec(grp, N_HEADS, HEAD_DIM, HEAD_DIM)


def _rwkv_prompt_call(p, bsz, prm):
    n = p.shape[0]
    t = n // bsz
    grp = _seq_group(bsz)
    small = [prm[k] for k in ("mu", "w0", "wup", "a0", "aup", "gup", "kk", "ka", "rk", "lng", "lnb")]
    y, shift, s = pl.pallas_call(
        _per_sequence(_rwkv_prompt_kernel, (True,) + (False,) * len(small) + (True,) * 5,
                      carried=(len(small) + 4, len(small) + 5)),
        grid=(bsz // grp, t // CHUNK),
        in_specs=[_chunk_specs(grp, RWKV_COLS)] + [_full(a.shape) for a in small],
        out_specs=[_chunk_specs(grp, MIX_W), _per_seq_spec(grp, 1, RWKV_COLS), _state_spec(grp)],
        out_shape=[jax.ShapeDtypeStruct((bsz, t, MIX_W), F32),
                   jax.ShapeDtypeStruct((bsz, 1, RWKV_COLS), F32),
                   jax.ShapeDtypeStruct((bsz, N_HEADS, HEAD_DIM, HEAD_DIM), F32)],
        scratch_shapes=[pltpu.VMEM((grp, 1, RWKV_COLS), F32), pltpu.VMEM((grp, MIX_W, MIX_W), F32)],
        compiler_params=_params("parallel", "arbitrary"),
        name="rwkv_prompt",
    )(p.reshape(bsz, t, RWKV_COLS), *small)
    return y.reshape(n, MIX_W), shift, s


def _hgrn_lower_bound(logits, layer):
    l0, l1 = logits[0:1, :], logits[1:2, :]
    m = jnp.maximum(l0, l1)
    e0, e1 = jnp.exp(l0 - m), jnp.exp(l1 - m)
    z = e0 + e1
    lb0, lb1 = e0 / z, e1 / z
    return (lb0 - lb0) if layer == 0 else ((lb0 + lb1) - lb0)


def _hgrn_log_f(z, lb):
    return jnp.logaddexp(jnp.log(jnp.maximum(lb, LB_FLOOR)), jnp.log1p(-lb) + jax.nn.log_sigmoid(z))


def _hgrn_prompt_kernel(layer, p_ref, lg_ref, ng_ref, y_ref, s_ref, st_ref):
    p = p_ref[...]
    L = p.shape[0]
    W = MIX_W
    q = jax.nn.silu(p[:, 0:W])
    lb = _hgrn_lower_bound(lg_ref[...], layer)
    log_f = _hgrn_log_f(p[:, W:2 * W], lb)
    k = 1.0 - jnp.exp(log_f)
    v = p[:, 2 * W:3 * W]
    g = p[:, 3 * W:4 * W]
    same_head = _same_head_mask(W)
    row = _iota((L, W), 0) % SUB
    b = log_f
    shift = 1
    while shift < SUB:
        b = b + jnp.where(row >= shift, pltpu.roll(b, shift, 0), 0.0)
        shift *= 2
    qe = q * jnp.exp(b)
    intra = jnp.zeros((L, W), F32)
    for lag in range(SUB):
        valid = row >= lag
        kr = k if lag == 0 else pltpu.roll(k, lag, 0)
        br = b if lag == 0 else pltpu.roll(b, lag, 0)
        vr = v if lag == 0 else pltpu.roll(v, lag, 0)
        term = jnp.where(valid, q * kr * jnp.exp(jnp.where(valid, b - br, 0.0)), 0.0)
        intra = intra + HEAD_DIM * _head_mean_lanes(term, same_head, split=False) * vr
        if lag % 4 == 3:
            yield
    st = st_ref[...]
    pieces = []
    for j in range(L // SUB):
        rs = slice(j * SUB, (j + 1) * SUB)
        pieces.append(_dot_nt(qe[rs], st, X1))
        b_l = b[(j + 1) * SUB - 1:(j + 1) * SUB]
        kd = k[rs] * jnp.exp(b_l - b[rs])
        st = st * jnp.exp(b_l) + jnp.where(same_head, _dot_tn(v[rs], kd, X1), 0.0)
        yield
    st_ref[...] = st
    for h in range(N_HEADS):
        sl = slice(h * HEAD_DIM, (h + 1) * HEAD_DIM)
        s_ref[h] = st[sl, sl].T
    o = jnp.concatenate(pieces, axis=0) + intra
    y_ref[...] = _head_norm_lanes(o, same_head, HEAD_NORM_EPS, False) * ng_ref[...] * jax.nn.silu(g)


def _hgrn_prompt_call(p, bsz, logits, norm_g, layer):
    n = p.shape[0]
    t = n // bsz
    grp = _seq_group(bsz)
    y, s = pl.pallas_call(
        _per_sequence(functools.partial(_hgrn_prompt_kernel, layer), (True, False, False, True, True, True),
                      carried=(5,)),
        grid=(bsz // grp, t // CHUNK),
        in_specs=[_chunk_specs(grp, 4 * MIX_W), _full(logits.shape), _full(norm_g.shape)],
        out_specs=[_chunk_specs(grp, MIX_W), _state_spec(grp)],
        out_shape=[jax.ShapeDtypeStruct((bsz, t, MIX_W), F32),
                   jax.ShapeDtypeStruct((bsz, N_HEADS, HEAD_DIM, HEAD_DIM), F32)],
        scratch_shapes=[pltpu.VMEM((grp, MIX_W, MIX_W), F32)],
        compiler_params=_params("parallel", "arbitrary"),
        name="hgrn_prompt",
    )(p.reshape(bsz, t, 4 * MIX_W), logits, norm_g)
    return y.reshape(n, MIX_W), s


def _ret_prompt_kernel(pos0, p_ref, inv_ref, y_ref, s_ref, sbd_ref):
    c = pl.program_id(1)
    p = p_ref[...]
    L = p.shape[0]
    W = MIX_W
    pos = (pos0 + c * L + _iota((L, W), 0)).astype(F32)
    ang = pos * inv_ref[...]
    cosv, sinv = jnp.cos(ang), jnp.sin(ang)
    first = (_iota((L, W), 1) % HEAD_DIM) < HALF_DIM

    def rot(z):
        partner = jnp.where(first, -pltpu.roll(z, W - HALF_DIM, 1), pltpu.roll(z, HALF_DIM, 1))
        return z * cosv + partner * sinv

    q = rot(p[:, 0:W])
    k = rot(p[:, W:2 * W]) * HEAD_DIM ** -0.5
    v = p[:, 2 * W:3 * W]
    g = p[:, 3 * W:4 * W]
    same_head = _same_head_mask(W)
    lg = _per_head_row(RET_LOG_GAMMA, W)
    t_row = _iota((L, W), 0).astype(F32)
    rel = t_row - (_iota((L, W), 1) % HEAD_DIM).astype(F32)
    dmat = jnp.where(rel >= 0, jnp.exp(lg * jnp.maximum(rel, 0.0)), 0.0)
    d_in = jnp.exp(lg * (t_row + 1.0))
    d_st = jnp.exp(lg * (L - 1.0 - t_row))
    s0 = sbd_ref[...]
    yield
    att = _dot_nt(q, _block_diag(k, same_head), X1) * dmat
    inter = _dot(q, s0, X1) * d_in
    yield
    o = _dot(att, _block_diag(v, same_head), X1) + inter
    s_new = s0 * jnp.exp(lg * float(L)) + jnp.where(same_head, _dot_tn(k * d_st, v, X1), 0.0)
    yield
    sbd_ref[...] = s_new
    for h in range(N_HEADS):
        sl = slice(h * HEAD_DIM, (h + 1) * HEAD_DIM)
        s_ref[h] = s_new[sl, sl]
    y_ref[...] = _head_norm_lanes(o, same_head, HEAD_NORM_EPS, False) * jax.nn.silu(g)


def _rotary_inv_row():
    inv = RET_THETA ** (-(np.arange(MIX_W) % HALF_DIM).astype(np.float64) / HALF_DIM)
    return jnp.asarray(inv.astype(np.float32).reshape(1, MIX_W))


def _ret_prompt_call(p, bsz, pos0):
    n = p.shape[0]
    t = n // bsz
    grp = _seq_group(bsz)
    inv = _rotary_inv_row()
    y, s = pl.pallas_call(
        _per_sequence(functools.partial(_ret_prompt_kernel, pos0), (True, False, True, True, True), carried=(4,)),
        grid=(bsz // grp, t // CHUNK),
        in_specs=[_chunk_specs(grp, 4 * MIX_W), _full(inv.shape)],
        out_specs=[_chunk_specs(grp, MIX_W), _state_spec(grp)],
        out_shape=[jax.ShapeDtypeStruct((bsz, t, MIX_W), F32),
                   jax.ShapeDtypeStruct((bsz, N_HEADS, HEAD_DIM, HEAD_DIM), F32)],
        scratch_shapes=[pltpu.VMEM((grp, MIX_W, MIX_W), F32)],
        compiler_params=_params("parallel", "arbitrary"),
        name="ret_prompt",
    )(p.reshape(bsz, t, 4 * MIX_W), inv)
    return y.reshape(n, MIX_W), s


def _mlstm_prompt_kernel(p_ref, pg_ref, cw_ref, cb_ref, gb_ref, ng_ref,
                         y_ref, conv_ref, c_ref, n_ref, m_ref, prev_ref, cbd_ref, nrow_ref, mrow_ref):
    p = p_ref[...]
    L = p.shape[0]
    W = MIX_W
    raw = p[:, 0:2 * W]
    prev = prev_ref[...]
    acc = cw_ref[MLSTM_CONV - 1:MLSTM_CONV, :] * raw
    for j in range(1, MLSTM_CONV):
        acc = acc + cw_ref[MLSTM_CONV - 1 - j:MLSTM_CONV - j, :] * _shift_rows(raw, prev, j)
    prev_ref[...] = raw
    conv_ref[...] = raw[L - 8:L, :]
    qk = jax.nn.silu(acc + cb_ref[...])
    q_all = qk[:, 0:W]
    k_all = qk[:, W:2 * W] * HEAD_DIM ** -0.5
    v_all = p[:, 2 * W:3 * W]
    og = p[:, 3 * W:4 * W]

    gb = pg_ref[...] + gb_ref[...]
    bcum = _cumsum_rows(jax.nn.log_sigmoid(gb))
    gb_t = gb.T
    bcum_t = bcum.T
    spread = lambda a, base: jnp.concatenate(
        [jnp.broadcast_to(a[:, base + h:base + h + 1], (L, HEAD_DIM)) for h in range(N_HEADS)], axis=1)
    rows = lambda a_t, base: jnp.concatenate([a_t[base + h:base + h + 1, :] for h in range(N_HEADS)], axis=1)
    ig_col, b_col = spread(gb, 0), spread(bcum, N_HEADS)
    ig_row, b_row = rows(gb_t, 0), rows(bcum_t, N_HEADS)
    same_head = _same_head_mask(W)
    causal = _iota((L, W), 0) >= (_iota((L, W), 1) % HEAD_DIM)
    m_prev = mrow_ref[...]
    log_d = jnp.where(causal, b_col - b_row + ig_row, MASK_NEG)
    m_inter = b_col + m_prev
    row_max = jnp.concatenate(
        [jnp.broadcast_to(jnp.max(log_d[:, h * HEAD_DIM:(h + 1) * HEAD_DIM], axis=1, keepdims=True), (L, HEAD_DIM))
         for h in range(N_HEADS)], axis=1)
    m_t = jnp.maximum(m_inter, row_max)
    c0 = cbd_ref[...]
    n0 = nrow_ref[...]
    yield
    sc = _dot_nt(q_all, _block_diag(k_all, same_head), X1) * jnp.exp(log_d - m_t)
    w_inter = jnp.exp(m_inter - m_t)
    inter = w_inter * _dot(q_all, c0, X1)
    m_l = m_t[L - 1:L, :]
    b_l = b_col[L - 1:L, :]
    a_prev = jnp.exp(b_l + m_prev - m_l)
    ks = k_all * jnp.exp(ig_col + b_l - b_col - m_l)
    yield
    num = _dot(sc, _block_diag(v_all, same_head), X1) + inter
    c_new = c0 * a_prev + jnp.where(same_head, _dot_tn(ks, v_all, X1), 0.0)
    yield
    den = HEAD_DIM * (_head_mean_lanes(sc, same_head) + w_inter * _head_mean_lanes(q_all * n0, same_head))
    hh = num / jnp.maximum(jnp.abs(den), jnp.exp(-m_t))
    n_new = n0 * a_prev + jnp.sum(ks, axis=0, keepdims=True)
    yield
    cbd_ref[...] = c_new
    nrow_ref[...] = n_new
    mrow_ref[...] = m_l
    m_out = jnp.zeros(m_ref.shape, F32)
    m_first = jnp.logical_and(_iota(m_ref.shape, 0) == 0, _iota(m_ref.shape, 1) < N_HEADS)
    for h in range(N_HEADS):
        sl = slice(h * HEAD_DIM, (h + 1) * HEAD_DIM)
        c_ref[h] = c_new[sl, sl]
        n_ref[h:h + 1, :] = n_new[:, sl]
        m_out = jnp.where(jnp.logical_and(m_first, _iota(m_ref.shape, 1) == h), m_l[:, h * HEAD_DIM:h * HEAD_DIM + 1], m_out)
    m_ref[...] = m_out
    hn = _head_norm_lanes(hh, same_head, HEAD_NORM_EPS, True) * ng_ref[...]
    y_ref[...] = jax.nn.sigmoid(og) * hn


def _mlstm_prompt_call(p, pg, bsz, conv_w, conv_b, gate_b, norm_g):
    n = p.shape[0]
    t = n // bsz
    grp = _seq_group(bsz)
    outs = pl.pallas_call(
        _per_sequence(_mlstm_prompt_kernel, (True, True) + (False,) * 4 + (True,) * 9, carried=(11, 12, 13, 14)),
        grid=(bsz // grp, t // CHUNK),
        in_specs=[_chunk_specs(grp, 4 * MIX_W), _chunk_specs(grp, LANES), _full(conv_w.shape),
                  _full(conv_b.shape), _full(gate_b.shape), _full(norm_g.shape)],
        out_specs=[_chunk_specs(grp, MIX_W),
                   _per_seq_spec(grp, 8, 2 * MIX_W),
                   _state_spec(grp),
                   _per_seq_spec(grp, N_HEADS, HEAD_DIM),
                   _per_seq_spec(grp, 8, LANES)],
        out_shape=[jax.ShapeDtypeStruct((bsz, t, MIX_W), F32),
                   jax.ShapeDtypeStruct((bsz, 8, 2 * MIX_W), F32),
                   jax.ShapeDtypeStruct((bsz, N_HEADS, HEAD_DIM, HEAD_DIM), F32),
                   jax.ShapeDtypeStruct((bsz, N_HEADS, HEAD_DIM), F32),
                   jax.ShapeDtypeStruct((bsz, 8, LANES), F32)],
        scratch_shapes=[pltpu.VMEM((grp, CHUNK, 2 * MIX_W), F32), pltpu.VMEM((grp, MIX_W, MIX_W), F32),
                        pltpu.VMEM((grp, 1, MIX_W), F32), pltpu.VMEM((grp, 1, MIX_W), F32)],
        compiler_params=_params("parallel", "arbitrary"),
        name="mlstm_prompt",
    )(p.reshape(bsz, t, 4 * MIX_W), pg.reshape(bsz, t, LANES), conv_w, conv_b, gate_b, norm_g)
    return (outs[0].reshape(n, MIX_W),) + tuple(outs[1:])


def _head_rows(ref, base, h):
    return ref[pl.ds(pl.multiple_of(base + h * HEAD_DIM, HEAD_DIM), HEAD_DIM), :]


def _rwkv_sample_kernel(p_ref, sh_ref, s_ref, mu_ref, w0_ref, wup_ref, a0_ref, aup_ref, gup_ref, kk_ref, ka_ref,
                        rk_ref, lng_ref, lnb_ref, y_ref, so_ref, v_scr, y_scr):
    h = pl.program_id(0)
    W = MIX_W

    def xs_rows(lo, size, dyn):
        idx = pl.ds(pl.multiple_of(lo + h * HEAD_DIM, HEAD_DIM), size) if dyn else pl.ds(lo, size)
        pv = p_ref[idx, :]
        return pv + mu_ref[idx, :] * (sh_ref[idx, :] - pv)

    r = xs_rows(0, HEAD_DIM, True)
    k = xs_rows(W, HEAD_DIM, True)
    v = xs_rows(2 * W, HEAD_DIM, True)
    xw = xs_rows(3 * W, RWKV_W_RANK, False)
    xa = xs_rows(3 * W + RWKV_W_RANK, RWKV_A_RANK, False)
    xg = xs_rows(3 * W + RWKV_W_RANK + RWKV_A_RANK, RWKV_G_RANK, False)
    w = w0_ref[...] + _dot(wup_ref[...], jnp.tanh(xw), HI)
    wd = jnp.exp(-jnp.exp(-jax.nn.softplus(-w) - 0.5))
    a = jax.nn.sigmoid(a0_ref[...] + _dot(aup_ref[...], xa, HI))
    g = _dot(gup_ref[...], jax.nn.sigmoid(xg), HI)
    kk = k * kk_ref[...]
    kk = kk / jnp.maximum(jnp.sqrt(jnp.sum(kk * kk, axis=0, keepdims=True)), 1e-12)
    kq = k * (1.0 + (a - 1.0) * ka_ref[...])
    bb = kk * a
    v_scr[...] = v

    def body(i, carry):
        sv = s_ref[i]
        sa = -jnp.sum(sv * kk, axis=0, keepdims=True)
        snew = sv * wd + sa * bb + v_scr[pl.ds(i, 1), :] * kq
        so_ref[i] = snew
        y_scr[pl.ds(i, 1), :] = jnp.sum(snew * r, axis=0, keepdims=True)
        return carry

    lax.fori_loop(0, HEAD_DIM, body, 0)
    y = _head_norm_cols(y_scr[...], RWKV_GN_EPS, True) * lng_ref[...] + lnb_ref[...]
    y = y + jnp.sum(r * kq * rk_ref[...], axis=0, keepdims=True) * v
    y_ref[...] = y * g


def _head_block(cols):
    return pl.BlockSpec((HEAD_DIM, cols), lambda h: (h, 0))


def _state_block(bsz):
    return pl.BlockSpec((None, HEAD_DIM, HEAD_DIM, bsz), lambda h: (h, 0, 0, 0))


def _rwkv_sample_call(p_t, shift_t, s_t, prm):
    bsz = p_t.shape[1]
    return pl.pallas_call(
        _rwkv_sample_kernel,
        grid=(N_HEADS,),
        in_specs=[_full(p_t.shape), _full(shift_t.shape), _state_block(bsz), _full((RWKV_COLS, 1)),
                  _head_block(1), _head_block(RWKV_W_RANK), _head_block(1), _head_block(RWKV_A_RANK),
                  _head_block(RWKV_G_RANK), _head_block(1), _head_block(1), _head_block(1), _head_block(1),
                  _head_block(1)],
        out_specs=[_head_block(bsz), _state_block(bsz)],
        out_shape=[jax.ShapeDtypeStruct((MIX_W, bsz), F32), jax.ShapeDtypeStruct(s_t.shape, F32)],
        scratch_shapes=[pltpu.VMEM((HEAD_DIM, bsz), F32), pltpu.VMEM((HEAD_DIM, bsz), F32)],
        compiler_params=_params("parallel"),
        name="rwkv_sample",
    )(p_t, shift_t, s_t, prm["mu_c"], prm["w0_c"], prm["wup_t"], prm["a0_c"], prm["aup_t"], prm["gup_t"],
      prm["kk_c"], prm["ka_c"], prm["rk_c"], prm["lng_c"], prm["lnb_c"])


def _kv_state_step(s_ref, so_ref, q_scr, k_scr, f_scr, v, decay):
    def body(i, o):
        dec = f_scr[pl.ds(i, 1), :] if decay is None else decay
        snew = s_ref[i] * dec + k_scr[pl.ds(i, 1), :] * v
        so_ref[i] = snew
        return o + q_scr[pl.ds(i, 1), :] * snew

    return lax.fori_loop(0, HEAD_DIM, body, jnp.zeros_like(v))


def _hgrn_sample_kernel(layer, p_ref, s_ref, lg_ref, ng_ref, y_ref, so_ref, q_scr, k_scr, f_scr):
    h = pl.program_id(0)
    W = MIX_W
    lg = lg_ref[...]
    l0, l1 = lg[:, 0:1], lg[:, 1:2]
    m = jnp.maximum(l0, l1)
    e0, e1 = jnp.exp(l0 - m), jnp.exp(l1 - m)
    z = e0 + e1
    lb0, lb1 = e0 / z, e1 / z
    lb = (lb0 - lb0) if layer == 0 else ((lb0 + lb1) - lb0)
    log_f = _hgrn_log_f(_head_rows(p_ref, W, h), lb)
    f = jnp.exp(log_f)
    q_scr[...] = jax.nn.silu(_head_rows(p_ref, 0, h))
    k_scr[...] = 1.0 - f
    f_scr[...] = f
    v = _head_rows(p_ref, 2 * W, h)
    g = _head_rows(p_ref, 3 * W, h)
    o = _kv_state_step(s_ref, so_ref, q_scr, k_scr, f_scr, v, None)
    y_ref[...] = _head_norm_cols(o, HEAD_NORM_EPS, False) * ng_ref[...] * jax.nn.silu(g)


def _hgrn_sample_call(p_t, s_t, logits_c, norm_g_c, layer):
    bsz = p_t.shape[1]
    return pl.pallas_call(
        functools.partial(_hgrn_sample_kernel, layer),
        grid=(N_HEADS,),
        in_specs=[_full(p_t.shape), _state_block(bsz), _head_block(2), _head_block(1)],
        out_specs=[_head_block(bsz), _state_block(bsz)],
        out_shape=[jax.ShapeDtypeStruct((MIX_W, bsz), F32), jax.ShapeDtypeStruct(s_t.shape, F32)],
        scratch_shapes=[pltpu.VMEM((HEAD_DIM, bsz), F32)] * 3,
        compiler_params=_params("parallel"),
        name="hgrn_sample",
    )(p_t, s_t, logits_c, norm_g_c)


def _ret_sample_kernel(pos0, p_ref, s_ref, inv_ref, y_ref, so_ref, q_scr, k_scr):
    h = pl.program_id(0)
    W = MIX_W
    ang = jnp.float32(pos0) * inv_ref[...]
    cosv, sinv = jnp.cos(ang), jnp.sin(ang)

    def rot(z):
        z1, z2 = z[0:HALF_DIM, :], z[HALF_DIM:HEAD_DIM, :]
        return jnp.concatenate([z1 * cosv - z2 * sinv, z1 * sinv + z2 * cosv], axis=0)

    q_scr[...] = rot(_head_rows(p_ref, 0, h))
    k_scr[...] = rot(_head_rows(p_ref, W, h)) * HEAD_DIM ** -0.5
    v = _head_rows(p_ref, 2 * W, h)
    g = _head_rows(p_ref, 3 * W, h)
    gamma = jnp.float32(math.exp(RET_LOG_GAMMA[N_HEADS - 1]))
    for hh in range(N_HEADS - 2, -1, -1):
        gamma = jnp.where(h == hh, jnp.float32(math.exp(RET_LOG_GAMMA[hh])), gamma)
    o = _kv_state_step(s_ref, so_ref, q_scr, k_scr, None, v, gamma)
    y_ref[...] = _head_norm_cols(o, HEAD_NORM_EPS, False) * jax.nn.silu(g)


def _ret_sample_call(p_t, s_t, pos0):
    bsz = p_t.shape[1]
    inv = RET_THETA ** (-np.arange(HALF_DIM).astype(np.float64) / HALF_DIM)
    inv = jnp.asarray(inv.astype(np.float32).reshape(HALF_DIM, 1))
    return pl.pallas_call(
        functools.partial(_ret_sample_kernel, pos0),
        grid=(N_HEADS,),
        in_specs=[_full(p_t.shape), _state_block(bsz), _full(inv.shape)],
        out_specs=[_head_block(bsz), _state_block(bsz)],
        out_shape=[jax.ShapeDtypeStruct((MIX_W, bsz), F32), jax.ShapeDtypeStruct(s_t.shape, F32)],
        scratch_shapes=[pltpu.VMEM((HEAD_DIM, bsz), F32)] * 2,
        compiler_params=_params("parallel"),
        name="ret_sample",
    )(p_t, s_t, inv)


def _mlstm_sample_kernel(p_ref, pg_ref, conv_ref, c_ref, n_ref, m_ref, cw_ref, cb_ref, igb_ref, fgb_ref, ng_ref,
                         y_ref, co_ref, no_ref, mo_ref, q_scr, k_scr):
    h = pl.program_id(0)
    W = MIX_W

    def conv_rows(base):
        idx = pl.ds(pl.multiple_of(base + h * HEAD_DIM, HEAD_DIM), HEAD_DIM)
        cw = cw_ref[idx, :]
        acc = cw[:, MLSTM_CONV - 1:MLSTM_CONV] * p_ref[idx, :]
        for j in range(MLSTM_CONV - 1):
            acc = acc + cw[:, j:j + 1] * conv_ref[j, idx, :]
        return jax.nn.silu(acc + cb_ref[idx, :])

    q = conv_rows(0)
    k = conv_rows(W) * HEAD_DIM ** -0.5
    v = _head_rows(p_ref, 2 * W, h)
    og = _head_rows(p_ref, 3 * W, h)
    ig = pg_ref[pl.ds(h, 1), :] + igb_ref[pl.ds(h, 1), :]
    lf = jax.nn.log_sigmoid(pg_ref[pl.ds(N_HEADS + h, 1), :] + fgb_ref[pl.ds(h, 1), :])
    m_prev = m_ref[pl.ds(h, 1), :]
    n0 = n_ref[...]
    m_inter = lf + m_prev
    m_t = jnp.maximum(m_inter, ig)
    sc = jnp.sum(q * k, axis=0, keepdims=True) * jnp.exp(ig - m_t)
    w_inter = jnp.exp(m_inter - m_t)
    a_prev = jnp.exp(lf + m_prev - m_t)
    a_s = jnp.exp(ig + lf - lf - m_t)
    q_scr[...] = q
    k_scr[...] = k * a_s

    def body(i, qc):
        c0 = c_ref[i]
        co_ref[i] = c0 * a_prev + k_scr[pl.ds(i, 1), :] * v
        return qc + q_scr[pl.ds(i, 1), :] * c0

    qc = lax.fori_loop(0, HEAD_DIM, body, jnp.zeros_like(v))
    num = sc * v + w_inter * qc
    den = sc + w_inter * jnp.sum(q * n0, axis=0, keepdims=True)
    hh = num / jnp.maximum(jnp.abs(den), jnp.exp(-m_t))
    no_ref[...] = n0 * a_prev + k * a_s
    mo_ref[pl.ds(h, 1), :] = m_t
    y_ref[...] = jax.nn.sigmoid(og) * (_head_norm_cols(hh, HEAD_NORM_EPS, True) * ng_ref[...])


def _mlstm_sample_call(p_t, pg_t, conv_t, c_t, n_t, m_t, prm):
    bsz = p_t.shape[1]
    nblk = pl.BlockSpec((None, HEAD_DIM, bsz), lambda h: (h, 0, 0))
    return pl.pallas_call(
        _mlstm_sample_kernel,
        grid=(N_HEADS,),
        in_specs=[_full(p_t.shape), _full(pg_t.shape), _full(conv_t.shape), _state_block(bsz), nblk,
                  _full(m_t.shape), _full((2 * MIX_W, MLSTM_CONV)), _full((2 * MIX_W, 1)),
                  _full((N_HEADS, 1)), _full((N_HEADS, 1)), _head_block(1)],
        out_specs=[_head_block(bsz), _state_block(bsz), nblk, _full(m_t.shape)],
        out_shape=[jax.ShapeDtypeStruct((MIX_W, bsz), F32), jax.ShapeDtypeStruct(c_t.shape, F32),
                   jax.ShapeDtypeStruct(n_t.shape, F32), jax.ShapeDtypeStruct(m_t.shape, F32)],
        scratch_shapes=[pltpu.VMEM((HEAD_DIM, bsz), F32)] * 2,
        compiler_params=_params("arbitrary"),
        name="mlstm_sample",
    )(p_t, pg_t, conv_t, c_t, n_t, m_t, prm["cw_t"], prm["cb_c"], prm["igb_c"], prm["fgb_c"], prm["ng_c"])


def _merge_kernel(x_ref, xn_ref, ya_ref, yb_ref, yc_ref, yd_ref, wg_ref, wb_ref, wo_ref, g2_ref, x1_ref, xn2_ref):
    gates = jax.nn.sigmoid(_dot(xn_ref[...], wg_ref[...]))
    merged = None
    for m, y_ref in enumerate((ya_ref, yb_ref, yc_ref, yd_ref)):
        br = _dot(y_ref[...].astype(BF16), wb_ref[m])
        term = gates[:, m * D_MODEL:(m + 1) * D_MODEL] * br
        merged = term if merged is None else merged + term
    x1 = x_ref[...] + _dot(merged.astype(BF16), wo_ref[...])
    x1_ref[...] = x1
    xn2 = x1 * lax.rsqrt(jnp.mean(x1 * x1, axis=-1, keepdims=True) + NORM_EPS) * g2_ref[...]
    xn2_ref[...] = xn2.T.astype(BF16)


def _merge_call(x2d, xn, ys, w_gate, w_branch, w_out, g2, tm):
    n = x2d.shape[0]
    row = lambda i: (i, 0)
    return pl.pallas_call(
        _merge_kernel,
        grid=(n // tm,),
        in_specs=[pl.BlockSpec((tm, D_MODEL), row)] * 2 + [pl.BlockSpec((tm, MIX_W), row)] * 4
        + [_full(w_gate.shape), _full(w_branch.shape), _full(w_out.shape), _full((1, D_MODEL))],
        out_specs=[pl.BlockSpec((tm, D_MODEL), row), pl.BlockSpec((D_MODEL, tm), lambda i: (0, i))],
        out_shape=[jax.ShapeDtypeStruct((n, D_MODEL), F32), jax.ShapeDtypeStruct((D_MODEL, n), BF16)],
        compiler_params=_params("parallel"),
        name="merge_out",
    )(x2d, xn, *ys, w_gate, w_branch, w_out, g2.reshape(1, D_MODEL))


def _top_ranked(s, count, break_ties):
    idx = _iota(s.shape, 0)
    rank = jnp.full(s.shape, float(count), F32)
    vals = []
    for r in range(count):
        mx = jnp.max(s, axis=0, keepdims=True)
        sel = s == mx
        if break_ties:
            sel = idx == jnp.min(jnp.where(sel, idx, s.shape[0]), axis=0, keepdims=True)
        rank = jnp.where(sel, float(r), rank)
        s = jnp.where(sel, NEG_BIG, s)
        vals.append(mx)
    ranked = jnp.sum((rank < float(count)).astype(F32), axis=0, keepdims=True)
    return vals, rank, ranked


_LATTICE_WIDTHS = tuple(PEER_TOPK // (i + 1) for i in range(PEER_TOPK))


def _peer_head_stats(s1, s2, break_ties):
    v1, rank1, n1 = _top_ranked(s1, PEER_TOPK, break_ties)
    v2, rank2, n2 = _top_ranked(s2, PEER_TOPK, break_ties)
    rows = [v1[i] + v2[j] for i in range(PEER_TOPK) for j in range(_LATTICE_WIDTHS[i])]
    pad = (-len(rows)) % 8
    cand = jnp.concatenate(rows + [jnp.full((pad, s1.shape[1]), NEG_BIG, F32)], axis=0)
    _, crank, _ = _top_ranked(cand, PEER_TOPK, True)
    chosen = crank < float(PEER_TOPK)
    z = jnp.sum(jnp.where(chosen, jnp.exp(cand - rows[0]), 0.0), axis=0, keepdims=True)
    taken = chosen.astype(F32)
    lim = jnp.zeros_like(s1)
    off = 0
    for i in range(PEER_TOPK):
        count_i = jnp.sum(taken[off:off + _LATTICE_WIDTHS[i], :], axis=0, keepdims=True)
        lim = jnp.where(rank1 == float(i), count_i, lim)
        off += _LATTICE_WIDTHS[i]
    tied = jnp.maximum(n1, n2) - float(PEER_TOPK)
    stats = (lim, jnp.exp(s1 - v1[0]) / z, rank2.astype(BF16), jnp.exp(s2 - v2[0]).astype(BF16))
    return stats, tied


def _peer_score_kernel(xn_ref, wq_ref, keys_ref, lim_ref, cf_ref, r2_ref, e2_ref, s_scr):
    q_t = _dot(wq_ref[...], xn_ref[...])
    out_refs = (lim_ref, cf_ref, r2_ref, e2_ref)
    tied = jnp.zeros((1, q_t.shape[1]), F32)
    q_hi = q_t.astype(BF16)
    q_lo = (q_t - q_hi.astype(F32)).astype(BF16)

    def scores(h, c):
        rows = slice((2 * h + c) * PEER_QHALF, (2 * h + c + 1) * PEER_QHALF)
        return _dot(keys_ref[h, c], jnp.concatenate([q_hi[rows], q_hi[rows], q_lo[rows]], axis=0))

    for h in range(PEER_HEADS):
        s1 = scores(h, 0)
        s2 = scores(h, 1)
        s_scr[h, 0] = s1
        s_scr[h, 1] = s2
        stats, tied_h = _peer_head_stats(s1, s2, False)
        for ref, val in zip(out_refs, stats):
            ref[h] = val
        tied = jnp.maximum(tied, tied_h)

    @pl.when(jnp.max(tied) > 0.0)
    def _():
        for h in range(PEER_HEADS):
            exact, _ = _peer_head_stats(s_scr[h, 0], s_scr[h, 1], True)
            for ref, val in zip(out_refs, exact):
                ref[h] = val


def _peer_score_call(xn2_t, wq_t, keys, tm):
    n = xn2_t.shape[1]
    blk = pl.BlockSpec((PEER_HEADS, PEER_N_KEYS, tm), lambda i: (0, 0, i))
    big = lambda dt: jax.ShapeDtypeStruct((PEER_HEADS, PEER_N_KEYS, n), dt)
    return pl.pallas_call(
        _peer_score_kernel,
        grid=(n // tm,),
        in_specs=[pl.BlockSpec((D_MODEL, tm), lambda i: (0, i)), _full(wq_t.shape), _full(keys.shape)],
        out_specs=[blk] * 4,
        out_shape=[big(F32), big(F32), big(BF16), big(BF16)],
        scratch_shapes=[pltpu.VMEM((PEER_HEADS, 2, PEER_N_KEYS, tm), F32)],
        compiler_params=_params("parallel"),
        name="peer_scores",
    )(xn2_t, wq_t, keys)


def _peer_expert_kernel(xn_ref, x1_ref, u_ref, vt_ref, lim_ref, cf_ref, r2_ref, e2_ref, o_ref, acc_ref):
    j = pl.program_id(1)

    @pl.when(j == 0)
    def _():
        acc_ref[...] = jnp.zeros_like(acc_ref)

    te = u_ref.shape[0]
    tm = xn_ref.shape[1]
    per_tile = te // PEER_N_KEYS
    pieces = []
    for al in range(per_tile):
        a = j * per_tile + al
        hid = _dot(u_ref[al * PEER_N_KEYS:(al + 1) * PEER_N_KEYS, :], xn_ref[...])
        act = hid * (1.0 + lax.erf(hid * INV_SQRT2))
        grouped = (PEER_N_KEYS // BF16_ROWS, BF16_ROWS, tm)
        gate = jnp.zeros(grouped, BF16)
        for h in range(PEER_HEADS):
            lim = jnp.broadcast_to(lim_ref[h, pl.ds(a, 1), :], (BF16_ROWS, tm)).astype(BF16)
            cf = jnp.broadcast_to(0.5 * cf_ref[h, pl.ds(a, 1), :], (BF16_ROWS, tm)).astype(BF16)
            r2 = r2_ref[h].reshape(grouped)
            e2 = e2_ref[h].reshape(grouped)
            gate = gate + jnp.where(r2 < lim[None], e2, jnp.zeros((), BF16)) * cf[None]
        pieces.append((act * gate.reshape(PEER_N_KEYS, tm).astype(F32)).astype(BF16))
    acc_ref[...] += _dot(vt_ref[...], jnp.concatenate(pieces, axis=0))

    @pl.when(j == pl.num_programs(1) - 1)
    def _():
        o_ref[...] = x1_ref[...] + acc_ref[...].T


def _peer_expert_call(xn2_t, x1, u, v_t, stats, tm):
    n = xn2_t.shape[1]
    n_tiles, _, te = v_t.shape
    tok = pl.BlockSpec((tm, D_MODEL), lambda i, j: (i, 0))
    blk = pl.BlockSpec((PEER_HEADS, PEER_N_KEYS, tm), lambda i, j: (0, 0, i))
    return pl.pallas_call(
        _peer_expert_kernel,
        grid=(n // tm, n_tiles),
        in_specs=[pl.BlockSpec((D_MODEL, tm), lambda i, j: (0, i)), tok,
                  pl.BlockSpec((te, D_MODEL), lambda i, j: (j, 0)),
                  pl.BlockSpec((None, D_MODEL, te), lambda i, j: (j, 0, 0)),
                  blk, blk, blk, blk],
        out_specs=tok,
        out_shape=jax.ShapeDtypeStruct((n, D_MODEL), F32),
        scratch_shapes=[pltpu.VMEM((D_MODEL, tm), F32)],
        compiler_params=_params("parallel", "arbitrary"),
        name="peer_experts",
    )(xn2_t, x1, u, v_t, *stats)


def _final_norm_kernel(x_ref, g_ref, o_ref):
    x = x_ref[...]
    o_ref[...] = x * lax.rsqrt(jnp.mean(x * x, axis=-1, keepdims=True) + NORM_EPS) * g_ref[...]


def _final_norm_call(x2d, g, tm):
    n = x2d.shape[0]
    row = lambda i: (i, 0)
    return pl.pallas_call(
        _final_norm_kernel,
        grid=(n // tm,),
        in_specs=[pl.BlockSpec((tm, D_MODEL), row), _full((1, D_MODEL))],
        out_specs=pl.BlockSpec((tm, D_MODEL), row),
        out_shape=jax.ShapeDtypeStruct((n, D_MODEL), F32),
        compiler_params=_params("parallel"),
        name="final_norm",
    )(x2d, g.reshape(1, D_MODEL))


def _hi_lo_hi(w):
    hi = w.astype(BF16)
    lo = (w - hi.astype(F32)).astype(BF16)
    return jnp.concatenate([hi, lo, hi], axis=-1)


def _layer_weights(l, w_in, rw, hg, ml, w_branch, w_out, ln1_g, ln2_g, peer_w_q, peer_keys, peer_u, peer_v):
    o3 = RWKV_COLS + 2 * 4 * MIX_W
    o4 = o3 + MLSTM_COLS
    gcol = o3 + 3 * MIX_W
    w = w_in[l]
    w_main = jnp.concatenate([w[:, 0:gcol], w[:, gcol + 2 * N_HEADS:o4]], axis=1).astype(BF16)
    w_mlg = jnp.pad(w[:, gcol:gcol + 2 * N_HEADS], ((0, 0), (0, LANES - 2 * N_HEADS)))
    row = lambda a: a[l].reshape(1, -1)
    col = lambda a: a[l].reshape(-1, 1)
    rwkv = dict(mu=row(rw["mu"]), w0=row(rw["w0"]), wup=rw["w_up"][l], a0=row(rw["a0"]), aup=rw["a_up"][l],
                gup=rw["g_up"][l], kk=row(rw["k_k"]), ka=row(rw["k_a"]), rk=row(rw["r_k"]),
                lng=row(rw["ln_g"]), lnb=row(rw["ln_b"]),
                mu_c=col(rw["mu"]), w0_c=col(rw["w0"]), wup_t=rw["w_up"][l].T, a0_c=col(rw["a0"]),
                aup_t=rw["a_up"][l].T, gup_t=rw["g_up"][l].T, kk_c=col(rw["k_k"]), ka_c=col(rw["k_a"]),
                rk_c=col(rw["r_k"]), lng_c=col(rw["ln_g"]), lnb_c=col(rw["ln_b"]))
    gate_b = jnp.pad(jnp.concatenate([ml["ig_b"][l], ml["fg_b"][l]]), (0, LANES - 2 * N_HEADS)).reshape(1, LANES)
    mlstm = dict(cw=ml["conv_w"][l], cb=row(ml["conv_b"]), gate_b=gate_b, ng=row(ml["norm_g"]),
                 cw_t=ml["conv_w"][l].T, cb_c=col(ml["conv_b"]), igb_c=col(ml["ig_b"]), fgb_c=col(ml["fg_b"]),
                 ng_c=col(ml["norm_g"]))
    return dict(
        ln1=ln1_g[l], ln2=ln2_g[l], w_main=w_main, w_mlg=w_mlg, w_gate=w[:, o4:].astype(BF16),
        w_branch=w_branch[l].astype(BF16), w_out=w_out[l].astype(BF16),
        rwkv=rwkv, hgrn_logits=hg["lb_logits"], hgrn_logits_c=hg["lb_logits"].T,
        hgrn_ng=row(hg["norm_g"]), hgrn_ng_c=col(hg["norm_g"]), mlstm=mlstm,
        wq_t=peer_w_q[l].T.astype(BF16), keys=_hi_lo_hi(peer_keys[l]), u=peer_u[l].astype(BF16),
        v_t=jnp.swapaxes(peer_v[l].astype(BF16).reshape(-1, PEER_EXPERT_TILE, D_MODEL), 1, 2))


def _lanes_last(a):
    return jnp.moveaxis(a, 0, -1)


def _prompt_mixers(proj, bsz, lw, layer):
    _, pr, ph, pt, pm, pg = proj
    ya, shift, s_a = _rwkv_prompt_call(pr, bsz, lw["rwkv"])
    yb, s_b = _hgrn_prompt_call(ph, bsz, lw["hgrn_logits"], lw["hgrn_ng"], layer)
    yc, s_c = _ret_prompt_call(pt, bsz, 0)
    m = lw["mlstm"]
    yd, conv, c_d, n_d, m_d = _mlstm_prompt_call(pm, pg, bsz, m["cw"], m["cb"], m["gate_b"], m["ng"])
    states = (shift[:, 0, :], s_a, s_b, s_c, conv[:, 8 - (MLSTM_CONV - 1):, :], c_d, n_d, m_d[:, 0, :N_HEADS])
    return (ya, yb, yc, yd), states


def _sample_mixers(proj, st, lw, layer, pos0):
    _, pr, ph, pt, pm, pg = proj
    shift, s_a, s_b, s_c, conv, c_d, n_d, m_d = st
    to_t = lambda a: jnp.moveaxis(a, 0, -1)
    from_t = lambda a: jnp.moveaxis(a, -1, 0)
    ya, s_a2 = _rwkv_sample_call(pr.T, shift.T, to_t(s_a), lw["rwkv"])
    yb, s_b2 = _hgrn_sample_call(ph.T, to_t(s_b), lw["hgrn_logits_c"], lw["hgrn_ng_c"], layer)
    yc, s_c2 = _ret_sample_call(pt.T, to_t(s_c), pos0)
    yd, c_d2, n_d2, m_d2 = _mlstm_sample_call(pm.T, pg.T, to_t(conv), to_t(c_d), to_t(n_d), m_d.T, lw["mlstm"])
    conv2 = jnp.concatenate([conv[:, 1:, :], pm[:, None, 0:2 * MIX_W]], axis=1)
    states = (pr, from_t(s_a2), from_t(s_b2), from_t(s_c2), conv2, from_t(c_d2), from_t(n_d2), m_d2.T)
    return (ya.T, yb.T, yc.T, yd.T), states


def _token_tiles(n):
    pick = lambda want: want if n % want == 0 else n
    return pick(256), pick(256), pick(512)


def _run_trunk(x, states, pos0, weights, lnf_g, prompt):
    bsz, t, _ = x.shape
    n = bsz * t
    x2d = x.reshape(n, D_MODEL)
    tm, tm_score, tm_peer = _token_tiles(n)
    new_states = []
    for layer, lw in enumerate(weights):
        proj = _proj_call(x2d, lw["ln1"], lw["w_main"], lw["w_mlg"], tm)
        if prompt:
            ys, st_new = _prompt_mixers(proj, bsz, lw, layer)
        else:
            ys, st_new = _sample_mixers(proj, tuple(s[layer] for s in states), lw, layer, pos0)
        new_states.append(st_new)
        x1, xn2 = _merge_call(x2d, proj[0], ys, lw["w_gate"], lw["w_branch"], lw["w_out"], lw["ln2"], tm)
        stats = _peer_score_call(xn2, lw["wq_t"], lw["keys"], tm_score)
        x2d = _peer_expert_call(xn2, x1, lw["u"], lw["v_t"], stats, tm_peer)
    y = _final_norm_call(x2d, lnf_g, tm).reshape(bsz, t, D_MODEL)
    stacked = tuple(jnp.stack([s[i] for s in new_states], axis=0) for i in range(8))
    return y, stacked


def kernel(x_prompt, x_sample, state_rwkv_shift, state_rwkv_S, state_hgrn_S, state_ret_S, state_mlstm_conv, state_mlstm_C, state_mlstm_n, state_mlstm_m, ln1_g, w_in, rwkv_mu, rwkv_w0, rwkv_w_up, rwkv_a0, rwkv_a_up, rwkv_g_up, rwkv_k_k, rwkv_k_a, rwkv_r_k, rwkv_ln_g, rwkv_ln_b, hgrn_lb_logits, hgrn_norm_g, mlstm_conv_w, mlstm_conv_b, mlstm_ig_b, mlstm_fg_b, mlstm_norm_g, w_branch, w_out, ln2_g, peer_w_q, peer_keys, peer_u, peer_v, lnf_g):
    depth = w_in.shape[0]
    rw = dict(mu=rwkv_mu, w0=rwkv_w0, w_up=rwkv_w_up, a0=rwkv_a0, a_up=rwkv_a_up, g_up=rwkv_g_up,
              k_k=rwkv_k_k, k_a=rwkv_k_a, r_k=rwkv_r_k, ln_g=rwkv_ln_g, ln_b=rwkv_ln_b)
    hg = dict(lb_logits=hgrn_lb_logits, norm_g=hgrn_norm_g)
    ml = dict(conv_w=mlstm_conv_w, conv_b=mlstm_conv_b, ig_b=mlstm_ig_b, fg_b=mlstm_fg_b, norm_g=mlstm_norm_g)
    weights = [_layer_weights(l, w_in, rw, hg, ml, w_branch, w_out, ln1_g, ln2_g, peer_w_q, peer_keys,
                              peer_u, peer_v) for l in range(depth)]
    past_len = 16384
    y_prompt, p_states = _run_trunk(x_prompt, None, 0, weights, lnf_g, True)
    sample_init = (state_rwkv_shift, state_rwkv_S, state_hgrn_S, state_ret_S, state_mlstm_conv,
                   state_mlstm_C, state_mlstm_n, state_mlstm_m)
    y_sample, s_states = _run_trunk(x_sample, sample_init, past_len, weights, lnf_g, False)
    return (y_prompt, y_sample) + p_states + s_states
```

```python
import functools
import math

import numpy as np
import jax
import jax.numpy as jnp
from jax import lax
from jax.experimental import pallas as pl
from jax.experimental.pallas import tpu as pltpu

F32 = jnp.float32
BF16 = jnp.bfloat16
HI = lax.Precision.HIGHEST

D_MODEL = 1024
N_HEADS = 4
HEAD_DIM = 64
HALF_DIM = HEAD_DIM // 2
MIX_W = N_HEADS * HEAD_DIM
RWKV_W_RANK = 64
RWKV_A_RANK = 64
RWKV_G_RANK = 128
RWKV_COLS = 3 * MIX_W + RWKV_W_RANK + RWKV_A_RANK + RWKV_G_RANK
MLSTM_COLS = 4 * MIX_W + 2 * N_HEADS
GATE_COLS = 4 * D_MODEL
MLSTM_CONV = 4
CHUNK = 64
SUB = 16
RET_THETA = 10000.0
PEER_HEADS = 8
PEER_N_KEYS = 128
PEER_TOPK = 16
PEER_QHALF = 64
NORM_EPS = 1e-6
RWKV_GN_EPS = 64e-5
HEAD_NORM_EPS = 1e-5
MASK_NEG = -1e30
LB_FLOOR = 1e-30
NEG_BIG = -1e30
INV_SQRT2 = 1.0 / math.sqrt(2.0)
RET_LOG_GAMMA = tuple(math.log(1.0 - 2.0 ** (-5.0 - h)) for h in range(N_HEADS))
LANES = 128
BF16_ROWS = 16
PEER_EXPERT_TILE = 2048
VMEM_LIMIT = 56 * 1024 * 1024


X1 = "bf16"


def _dot_dims(a, b, dims, prec):
    if prec == X1:
        return lax.dot_general(a.astype(BF16), b.astype(BF16), dims, preferred_element_type=F32)
    return lax.dot_general(a, b, dims, precision=prec, preferred_element_type=F32)


def _dot(a, b, prec=None):
    return _dot_dims(a, b, (((1,), (0,)), ((), ())), prec)


def _dot_nt(a, b, prec=None):
    return _dot_dims(a, b, (((1,), (1,)), ((), ())), prec)


def _dot_tn(a, b, prec=None):
    return _dot_dims(a, b, (((0,), (0,)), ((), ())), prec)


def _params(*sem):
    return pltpu.CompilerParams(dimension_semantics=sem, vmem_limit_bytes=VMEM_LIMIT)


def _full(shape):
    nd = len(shape)
    return pl.BlockSpec(shape, lambda *_: (0,) * nd)


def _iota(shape, axis):
    return lax.broadcasted_iota(jnp.int32, shape, axis)


def _shift_rows(cur, prev, j):
    row = _iota(cur.shape, 0)
    return jnp.where(row >= j, pltpu.roll(cur, j, 0), pltpu.roll(prev, j, 0))


def _head_norm_rows(y, eps, center):
    if center:
        y = y - jnp.mean(y, axis=1, keepdims=True)
    return y * lax.rsqrt(jnp.mean(y * y, axis=1, keepdims=True) + eps)


def _same_head_mask(width):
    return (_iota((width, width), 0) // HEAD_DIM) == (_iota((width, width), 1) // HEAD_DIM)


def _per_head_row(values, width):
    head = _iota((1, width), 1) // HEAD_DIM
    row = jnp.full((1, width), values[-1], F32)
    for h in range(len(values) - 2, -1, -1):
        row = jnp.where(head == h, values[h], row)
    return row


def _block_diag(x, same_head):
    reps = same_head.shape[0] // x.shape[0]
    return jnp.where(same_head, jnp.concatenate([x] * reps, axis=0), 0.0)


def _head_mean_lanes(y, same_head, split=True):
    e = jnp.where(same_head, 1.0 / HEAD_DIM, 0.0).astype(BF16)
    hi = y.astype(BF16)
    if not split:
        return _dot(hi, e)
    lo = (y - hi.astype(F32)).astype(BF16)
    return _dot(jnp.concatenate([hi, lo], axis=1), jnp.concatenate([e, e], axis=0))


def _head_norm_lanes(y, same_head, eps, center):
    if center:
        y = y - _head_mean_lanes(y, same_head)
    return y * lax.rsqrt(_head_mean_lanes(y * y, same_head) + eps)


def _head_norm_cols(y, eps, center):
    if center:
        y = y - jnp.mean(y, axis=0, keepdims=True)
    return y * lax.rsqrt(jnp.mean(y * y, axis=0, keepdims=True) + eps)


def _proj_kernel(x_ref, g_ref, w_ref, wg_ref, xn_ref, pr_ref, ph_ref, pt_ref, pm_ref, pg_ref):
    x = x_ref[...]
    xn = x * lax.rsqrt(jnp.mean(x * x, axis=-1, keepdims=True) + NORM_EPS) * g_ref[...]
    xb = xn.astype(BF16)
    xn_ref[...] = xb
    p = _dot(xb, w_ref[...])
    pr_ref[...] = p[:, 0:1024]
    ph_ref[...] = p[:, 1024:2048]
    pt_ref[...] = p[:, 2048:3072]
    pm_ref[...] = p[:, 3072:4096]
    x_lo = (xn - xb.astype(F32)).astype(BF16)
    pg_ref[...] = _dot(jnp.concatenate([xb, xb, x_lo], axis=1), wg_ref[...])


def _proj_call(x2d, g, w_main, w_mlg, tm):
    n = x2d.shape[0]
    row = lambda i: (i, 0)
    out = lambda c, dt: jax.ShapeDtypeStruct((n, c), dt)
    return pl.pallas_call(
        _proj_kernel,
        grid=(n // tm,),
        in_specs=[pl.BlockSpec((tm, D_MODEL), row), _full((1, D_MODEL)),
                  _full((D_MODEL, 4096)), _full((3 * D_MODEL, LANES))],
        out_specs=[pl.BlockSpec((tm, D_MODEL), row)] * 5 + [pl.BlockSpec((tm, LANES), row)],
        out_shape=[out(D_MODEL, BF16)] + [out(D_MODEL, F32)] * 4 + [out(LANES, F32)],
        compiler_params=_params("parallel"),
        name="norm_proj",
    )(x2d, g.reshape(1, D_MODEL), w_main, w_mlg)


def _cumsum_rows(x):
    row = _iota(x.shape, 0)
    shift = 1
    while shift < x.shape[0]:
        x = x + jnp.where(row >= shift, pltpu.roll(x, shift, 0), 0.0)
        shift *= 2
    return x


def _rwkv_prompt_kernel(p_ref, mu_ref, w0_ref, wup_ref, a0_ref, aup_ref, gup_ref, kk_ref, ka_ref, rk_ref,
                        lng_ref, lnb_ref, y_ref, shift_ref, s_ref, prow_ref, sbd_ref):
    p = p_ref[...]
    L = p.shape[0]
    row = _iota(p.shape, 0)
    prev = jnp.where(row >= 1, pltpu.roll(p, 1, 0), jnp.broadcast_to(prow_ref[...], p.shape))
    last = p[L - 1:L, :]
    prow_ref[...] = last
    shift_ref[...] = last
    xs = p + mu_ref[...] * (prev - p)
    W = MIX_W
    r, k, v = xs[:, 0:W], xs[:, W:2 * W], xs[:, 2 * W:3 * W]
    o = 3 * W
    xw = xs[:, o:o + RWKV_W_RANK]
    o += RWKV_W_RANK
    xa = xs[:, o:o + RWKV_A_RANK]
    o += RWKV_A_RANK
    xg = xs[:, o:o + RWKV_G_RANK]
    w = w0_ref[...] + _dot(jnp.tanh(xw), wup_ref[...], X1)
    logw = -jnp.exp(-jax.nn.softplus(-w) - 0.5)
    a = jax.nn.sigmoid(a0_ref[...] + _dot(xa, aup_ref[...], X1))
    g = _dot(jax.nn.sigmoid(xg), gup_ref[...], X1)
    same_head = _same_head_mask(W)
    kk = k * kk_ref[...]
    kk = kk / jnp.maximum(jnp.sqrt(HEAD_DIM * _head_mean_lanes(kk * kk, same_head)), 1e-12)
    kq = k * (1.0 + (a - 1.0) * ka_ref[...])
    bb = kk * a

    cum = _cumsum_rows(logw)
    cum_l = cum[L - 1:L, :]
    e_neg = jnp.exp(-cum)
    e_tail = jnp.exp(cum_l - cum)
    lhs = jnp.concatenate([kk * jnp.exp(cum - logw), r * jnp.exp(cum)], axis=0)
    s0 = sbd_ref[...]
    yield
    gb = _dot_nt(lhs, _block_diag(bb * e_neg, same_head), X1)
    yield
    gk = _dot_nt(lhs, _block_diag(kq * e_neg, same_head), X1)
    qs = _dot_nt(lhs, s0, X1)
    t_idx = _iota((L, W), 0)
    s_idx = _iota((L, W), 1) % HEAD_DIM
    strict = t_idx > s_idx
    incl = t_idx >= s_idx
    n_k = jnp.where(strict, gk[0:L], 0.0)
    n_b = jnp.where(strict, gb[0:L], 0.0)
    m_k = jnp.where(incl, gk[L:2 * L], 0.0)
    m_b = jnp.where(incl, gb[L:2 * L], 0.0)
    tinv = (t_idx == s_idx).astype(F32) - n_b
    pw = n_b
    for _ in range(5):
        yield
        pw = _dot(pw, _block_diag(pw, same_head), X1)
        yield
        tinv = tinv + _dot(tinv, _block_diag(pw, same_head), X1)
    kv = _dot(jnp.concatenate([n_k, m_k], axis=0), _block_diag(v, same_head), X1)
    yield
    u = _dot(tinv, _block_diag(qs[0:L] + kv[0:L], same_head), X1)
    yield
    y = qs[L:2 * L] + kv[L:2 * L] - _dot(m_b, _block_diag(u, same_head), X1)
    upd = _dot_tn(jnp.concatenate([v, -u], axis=0),
                  jnp.concatenate([kq * e_tail, bb * e_tail], axis=0), X1)
    yield
    s_new = s0 * jnp.exp(cum_l) + jnp.where(same_head, upd, 0.0)
    sbd_ref[...] = s_new
    for h in range(N_HEADS):
        sl = slice(h * HEAD_DIM, (h + 1) * HEAD_DIM)
        s_ref[h] = s_new[sl, sl]
    y = _head_norm_lanes(y, same_head, RWKV_GN_EPS, True) * lng_ref[...] + lnb_ref[...]
    y = y + HEAD_DIM * _head_mean_lanes(r * kq * rk_ref[...], same_head) * v
    y_ref[...] = y * g


def _per_sequence(body, batched, carried):
    def kernel(*refs):
        @pl.when(pl.program_id(1) == 0)
        def _():
            for i in carried:
                refs[i][...] = jnp.zeros_like(refs[i])

        group = next(r.shape[0] for r, b in zip(refs, batched) if b)
        runs = [body(*[r.at[gi] if b else r for r, b in zip(refs, batched)]) for gi in range(group)]
        runs = [r for r in runs if r is not None]
        while runs:
            for r in list(runs):
                if next(r, _DONE) is _DONE:
                    runs.remove(r)

    return kernel


_DONE = object()


def _seq_group(bsz):
    return next(g for g in (4, 2, 1) if bsz % g == 0)


def _chunk_specs(grp, width):
    return pl.BlockSpec((grp, CHUNK, width), lambda b, c: (b, c, 0))


def _per_seq_spec(grp, *dims):
    return pl.BlockSpec((grp,) + dims, lambda b, c: (b,) + (0,) * len(dims))


def _state_spec(grp):
    return _per_seq_spec(grp, N_HEADS, HEAD_DIM, HEAD_DIM)


def _rwkv_prompt_call(p, bsz, prm):
    n = p.shape[0]
    t = n // bsz
    grp = _seq_group(bsz)
    small = [prm[k] for k in ("mu", "w0", "wup", "a0", "aup", "gup", "kk", "ka", "rk", "lng", "lnb")]
    y, shift, s = pl.pallas_call(
        _per_sequence(_rwkv_prompt_kernel, (True,) + (False,) * len(small) + (True,) * 5,
                      carried=(len(small) + 4, len(small) + 5)),
        grid=(bsz // grp, t // CHUNK),
        in_specs=[_chunk_specs(grp, RWKV_COLS)] + [_full(a.shape) for a in small],
        out_specs=[_chunk_specs(grp, MIX_W), _per_seq_spec(grp, 1, RWKV_COLS), _state_spec(grp)],
        out_shape=[jax.ShapeDtypeStruct((bsz, t, MIX_W), F32),
                   jax.ShapeDtypeStruct((bsz, 1, RWKV_COLS), F32),
                   jax.ShapeDtypeStruct((bsz, N_HEADS, HEAD_DIM, HEAD_DIM), F32)],
        scratch_shapes=[pltpu.VMEM((grp, 1, RWKV_COLS), F32), pltpu.VMEM((grp, MIX_W, MIX_W), F32)],
        compiler_params=_params("parallel", "arbitrary"),
        name="rwkv_prompt",
    )(p.reshape(bsz, t, RWKV_COLS), *small)
    return y.reshape(n, MIX_W), shift, s


def _hgrn_lower_bound(logits, layer):
    l0, l1 = logits[0:1, :], logits[1:2, :]
    m = jnp.maximum(l0, l1)
    e0, e1 = jnp.exp(l0 - m), jnp.exp(l1 - m)
    z = e0 + e1
    lb0, lb1 = e0 / z, e1 / z
    return (lb0 - lb0) if layer == 0 else ((lb0 + lb1) - lb0)


def _hgrn_log_f(z, lb):
    return jnp.logaddexp(jnp.log(jnp.maximum(lb, LB_FLOOR)), jnp.log1p(-lb) + jax.nn.log_sigmoid(z))


def _hgrn_prompt_kernel(layer, p_ref, lg_ref, ng_ref, y_ref, s_ref, st_ref):
    p = p_ref[...]
    L = p.shape[0]
    W = MIX_W
    q = jax.nn.silu(p[:, 0:W])
    lb = _hgrn_lower_bound(lg_ref[...], layer)
    log_f = _hgrn_log_f(p[:, W:2 * W], lb)
    k = 1.0 - jnp.exp(log_f)
    v = p[:, 2 * W:3 * W]
    g = p[:, 3 * W:4 * W]
    same_head = _same_head_mask(W)
    row = _iota((L, W), 0) % SUB
    b = log_f
    shift = 1
    while shift < SUB:
        b = b + jnp.where(row >= shift, pltpu.roll(b, shift, 0), 0.0)
        shift *= 2
    qe = q * jnp.exp(b)
    intra = jnp.zeros((L, W), F32)
    for lag in range(SUB):
        valid = row >= lag
        kr = k if lag == 0 else pltpu.roll(k, lag, 0)
        br = b if lag == 0 else pltpu.roll(b, lag, 0)
        vr = v if lag == 0 else pltpu.roll(v, lag, 0)
        term = jnp.where(valid, q * kr * jnp.exp(jnp.where(valid, b - br, 0.0)), 0.0)
        intra = intra + HEAD_DIM * _head_mean_lanes(term, same_head, split=False) * vr
        if lag % 4 == 3:
            yield
    st = st_ref[...]
    pieces = []
    for j in range(L // SUB):
        rs = slice(j * SUB, (j + 1) * SUB)
        pieces.append(_dot_nt(qe[rs], st, X1))
        b_l = b[(j + 1) * SUB - 1:(j + 1) * SUB]
        kd = k[rs] * jnp.exp(b_l - b[rs])
        st = st * jnp.exp(b_l) + jnp.where(same_head, _dot_tn(v[rs], kd, X1), 0.0)
        yield
    st_ref[...] = st
    for h in range(N_HEADS):
        sl = slice(h * HEAD_DIM, (h + 1) * HEAD_DIM)
        s_ref[h] = st[sl, sl].T
    o = jnp.concatenate(pieces, axis=0) + intra
    y_ref[...] = _head_norm_lanes(o, same_head, HEAD_NORM_EPS, False) * ng_ref[...] * jax.nn.silu(g)


def _hgrn_prompt_call(p, bsz, logits, norm_g, layer):
    n = p.shape[0]
    t = n // bsz
    grp = _seq_group(bsz)
    y, s = pl.pallas_call(
        _per_sequence(functools.partial(_hgrn_prompt_kernel, layer), (True, False, False, True, True, True),
                      carried=(5,)),
        grid=(bsz // grp, t // CHUNK),
        in_specs=[_chunk_specs(grp, 4 * MIX_W), _full(logits.shape), _full(norm_g.shape)],
        out_specs=[_chunk_specs(grp, MIX_W), _state_spec(grp)],
        out_shape=[jax.ShapeDtypeStruct((bsz, t, MIX_W), F32),
                   jax.ShapeDtypeStruct((bsz, N_HEADS, HEAD_DIM, HEAD_DIM), F32)],
        scratch_shapes=[pltpu.VMEM((grp, MIX_W, MIX_W), F32)],
        compiler_params=_params("parallel", "arbitrary"),
        name="hgrn_prompt",
    )(p.reshape(bsz, t, 4 * MIX_W), logits, norm_g)
    return y.reshape(n, MIX_W), s


def _ret_prompt_kernel(pos0, p_ref, inv_ref, y_ref, s_ref, sbd_ref):
    c = pl.program_id(1)
    p = p_ref[...]
    L = p.shape[0]
    W = MIX_W
    pos = (pos0 + c * L + _iota((L, W), 0)).astype(F32)
    ang = pos * inv_ref[...]
    cosv, sinv = jnp.cos(ang), jnp.sin(ang)
    first = (_iota((L, W), 1) % HEAD_DIM) < HALF_DIM

    def rot(z):
        partner = jnp.where(first, -pltpu.roll(z, W - HALF_DIM, 1), pltpu.roll(z, HALF_DIM, 1))
        return z * cosv + partner * sinv

    q = rot(p[:, 0:W])
    k = rot(p[:, W:2 * W]) * HEAD_DIM ** -0.5
    v = p[:, 2 * W:3 * W]
    g = p[:, 3 * W:4 * W]
    same_head = _same_head_mask(W)
    lg = _per_head_row(RET_LOG_GAMMA, W)
    t_row = _iota((L, W), 0).astype(F32)
    rel = t_row - (_iota((L, W), 1) % HEAD_DIM).astype(F32)
    dmat = jnp.where(rel >= 0, jnp.exp(lg * jnp.maximum(rel, 0.0)), 0.0)
    d_in = jnp.exp(lg * (t_row + 1.0))
    d_st = jnp.exp(lg * (L - 1.0 - t_row))
    s0 = sbd_ref[...]
    yield
    att = _dot_nt(q, _block_diag(k, same_head), X1) * dmat
    inter = _dot(q, s0, X1) * d_in
    yield
    o = _dot(att, _block_diag(v, same_head), X1) + inter
    s_new = s0 * jnp.exp(lg * float(L)) + jnp.where(same_head, _dot_tn(k * d_st, v, X1), 0.0)
    yield
    sbd_ref[...] = s_new
    for h in range(N_HEADS):
        sl = slice(h * HEAD_DIM, (h + 1) * HEAD_DIM)
        s_ref[h] = s_new[sl, sl]
    y_ref[...] = _head_norm_lanes(o, same_head, HEAD_NORM_EPS, False) * jax.nn.silu(g)


def _rotary_inv_row():
    inv = RET_THETA ** (-(np.arange(MIX_W) % HALF_DIM).astype(np.float64) / HALF_DIM)
    return jnp.asarray(inv.astype(np.float32).reshape(1, MIX_W))


def _ret_prompt_call(p, bsz, pos0):
    n = p.shape[0]
    t = n // bsz
    grp = _seq_group(bsz)
    inv = _rotary_inv_row()
    y, s = pl.pallas_call(
        _per_sequence(functools.partial(_ret_prompt_kernel, pos0), (True, False, True, True, True), carried=(4,)),
        grid=(bsz // grp, t // CHUNK),
        in_specs=[_chunk_specs(grp, 4 * MIX_W), _full(inv.shape)],
        out_specs=[_chunk_specs(grp, MIX_W), _state_spec(grp)],
        out_shape=[jax.ShapeDtypeStruct((bsz, t, MIX_W), F32),
                   jax.ShapeDtypeStruct((bsz, N_HEADS, HEAD_DIM, HEAD_DIM), F32)],
        scratch_shapes=[pltpu.VMEM((grp, MIX_W, MIX_W), F32)],
        compiler_params=_params("parallel", "arbitrary"),
        name="ret_prompt",
    )(p.reshape(bsz, t, 4 * MIX_W), inv)
    return y.reshape(n, MIX_W), s


def _mlstm_prompt_kernel(p_ref, pg_ref, cw_ref, cb_ref, gb_ref, ng_ref,
                         y_ref, conv_ref, c_ref, n_ref, m_ref, prev_ref, cbd_ref, nrow_ref, mrow_ref):
    p = p_ref[...]
    L = p.shape[0]
    W = MIX_W
    raw = p[:, 0:2 * W]
    prev = prev_ref[...]
    acc = cw_ref[MLSTM_CONV - 1:MLSTM_CONV, :] * raw
    for j in range(1, MLSTM_CONV):
        acc = acc + cw_ref[MLSTM_CONV - 1 - j:MLSTM_CONV - j, :] * _shift_rows(raw, prev, j)
    prev_ref[...] = raw
    conv_ref[...] = raw[L - 8:L, :]
    qk = jax.nn.silu(acc + cb_ref[...])
    q_all = qk[:, 0:W]
    k_all = qk[:, W:2 * W] * HEAD_DIM ** -0.5
    v_all = p[:, 2 * W:3 * W]
    og = p[:, 3 * W:4 * W]

    gb = pg_ref[...] + gb_ref[...]
    bcum = _cumsum_rows(jax.nn.log_sigmoid(gb))
    gb_t = gb.T
    bcum_t = bcum.T
    spread = lambda a, base: jnp.concatenate(
        [jnp.broadcast_to(a[:, base + h:base + h + 1], (L, HEAD_DIM)) for h in range(N_HEADS)], axis=1)
    rows = lambda a_t, base: jnp.concatenate([a_t[base + h:base + h + 1, :] for h in range(N_HEADS)], axis=1)
    ig_col, b_col = spread(gb, 0), spread(bcum, N_HEADS)
    ig_row, b_row = rows(gb_t, 0), rows(bcum_t, N_HEADS)
    same_head = _same_head_mask(W)
    causal = _iota((L, W), 0) >= (_iota((L, W), 1) % HEAD_DIM)
    m_prev = mrow_ref[...]
    log_d = jnp.where(causal, b_col - b_row + ig_row, MASK_NEG)
    m_inter = b_col + m_prev
    row_max = jnp.concatenate(
        [jnp.broadcast_to(jnp.max(log_d[:, h * HEAD_DIM:(h + 1) * HEAD_DIM], axis=1, keepdims=True), (L, HEAD_DIM))
         for h in range(N_HEADS)], axis=1)
    m_t = jnp.maximum(m_inter, row_max)
    c0 = cbd_ref[...]
    n0 = nrow_ref[...]
    yield
    sc = _dot_nt(q_all, _block_diag(k_all, same_head), X1) * jnp.exp(log_d - m_t)
    w_inter = jnp.exp(m_inter - m_t)
    inter = w_inter * _dot(q_all, c0, X1)
    m_l = m_t[L - 1:L, :]
    b_l = b_col[L - 1:L, :]
    a_prev = jnp.exp(b_l + m_prev - m_l)
    ks = k_all * jnp.exp(ig_col + b_l - b_col - m_l)
    yield
    num = _dot(sc, _block_diag(v_all, same_head), X1) + inter
    c_new = c0 * a_prev + jnp.where(same_head, _dot_tn(ks, v_all, X1), 0.0)
    yield
    den = HEAD_DIM * (_head_mean_lanes(sc, same_head) + w_inter * _head_mean_lanes(q_all * n0, same_head))
    hh = num / jnp.maximum(jnp.abs(den), jnp.exp(-m_t))
    n_new = n0 * a_prev + jnp.sum(ks, axis=0, keepdims=True)
    yield
    cbd_ref[...] = c_new
    nrow_ref[...] = n_new
    mrow_ref[...] = m_l
    m_out = jnp.zeros(m_ref.shape, F32)
    m_first = jnp.logical_and(_iota(m_ref.shape, 0) == 0, _iota(m_ref.shape, 1) < N_HEADS)
    for h in range(N_HEADS):
        sl = slice(h * HEAD_DIM, (h + 1) * HEAD_DIM)
        c_ref[h] = c_new[sl, sl]
        n_ref[h:h + 1, :] = n_new[:, sl]
        m_out = jnp.where(jnp.logical_and(m_first, _iota(m_ref.shape, 1) == h), m_l[:, h * HEAD_DIM:h * HEAD_DIM + 1], m_out)
    m_ref[...] = m_out
    hn = _head_norm_lanes(hh, same_head, HEAD_NORM_EPS, True) * ng_ref[...]
    y_ref[...] = jax.nn.sigmoid(og) * hn


def _mlstm_prompt_call(p, pg, bsz, conv_w, conv_b, gate_b, norm_g):
    n = p.shape[0]
    t = n // bsz
    grp = _seq_group(bsz)
    outs = pl.pallas_call(
        _per_sequence(_mlstm_prompt_kernel, (True, True) + (False,) * 4 + (True,) * 9, carried=(11, 12, 13, 14)),
        grid=(bsz // grp, t // CHUNK),
        in_specs=[_chunk_specs(grp, 4 * MIX_W), _chunk_specs(grp, LANES), _full(conv_w.shape),
                  _full(conv_b.shape), _full(gate_b.shape), _full(norm_g.shape)],
        out_specs=[_chunk_specs(grp, MIX_W),
                   _per_seq_spec(grp, 8, 2 * MIX_W),
                   _state_spec(grp),
                   _per_seq_spec(grp, N_HEADS, HEAD_DIM),
                   _per_seq_spec(grp, 8, LANES)],
        out_shape=[jax.ShapeDtypeStruct((bsz, t, MIX_W), F32),
                   jax.ShapeDtypeStruct((bsz, 8, 2 * MIX_W), F32),
                   jax.ShapeDtypeStruct((bsz, N_HEADS, HEAD_DIM, HEAD_DIM), F32),
                   jax.ShapeDtypeStruct((bsz, N_HEADS, HEAD_DIM), F32),
                   jax.ShapeDtypeStruct((bsz, 8, LANES), F32)],
        scratch_shapes=[pltpu.VMEM((grp, CHUNK, 2 * MIX_W), F32), pltpu.VMEM((grp, MIX_W, MIX_W), F32),
                        pltpu.VMEM((grp, 1, MIX_W), F32), pltpu.VMEM((grp, 1, MIX_W), F32)],
        compiler_params=_params("parallel", "arbitrary"),
        name="mlstm_prompt",
    )(p.reshape(bsz, t, 4 * MIX_W), pg.reshape(bsz, t, LANES), conv_w, conv_b, gate_b, norm_g)
    return (outs[0].reshape(n, MIX_W),) + tuple(outs[1:])


def _head_rows(ref, base, h):
    return ref[pl.ds(pl.multiple_of(base + h * HEAD_DIM, HEAD_DIM), HEAD_DIM), :]


def _rwkv_sample_kernel(p_ref, sh_ref, s_ref, mu_ref, w0_ref, wup_ref, a0_ref, aup_ref, gup_ref, kk_ref, ka_ref,
                        rk_ref, lng_ref, lnb_ref, y_ref, so_ref, v_scr, y_scr):
    h = pl.program_id(0)
    W = MIX_W

    def xs_rows(lo, size, dyn):
        idx = pl.ds(pl.multiple_of(lo + h * HEAD_DIM, HEAD_DIM), size) if dyn else pl.ds(lo, size)
        pv = p_ref[idx, :]
        return pv + mu_ref[idx, :] * (sh_ref[idx, :] - pv)

    r = xs_rows(0, HEAD_DIM, True)
    k = xs_rows(W, HEAD_DIM, True)
    v = xs_rows(2 * W, HEAD_DIM, True)
    xw = xs_rows(3 * W, RWKV_W_RANK, False)
    xa = xs_rows(3 * W + RWKV_W_RANK, RWKV_A_RANK, False)
    xg = xs_rows(3 * W + RWKV_W_RANK + RWKV_A_RANK, RWKV_G_RANK, False)
    w = w0_ref[...] + _dot(wup_ref[...], jnp.tanh(xw), HI)
    wd = jnp.exp(-jnp.exp(-jax.nn.softplus(-w) - 0.5))
    a = jax.nn.sigmoid(a0_ref[...] + _dot(aup_ref[...], xa, HI))
    g = _dot(gup_ref[...], jax.nn.sigmoid(xg), HI)
    kk = k * kk_ref[...]
    kk = kk / jnp.maximum(jnp.sqrt(jnp.sum(kk * kk, axis=0, keepdims=True)), 1e-12)
    kq = k * (1.0 + (a - 1.0) * ka_ref[...])
    bb = kk * a
    v_scr[...] = v

    def body(i, carry):
        sv = s_ref[i]
        sa = -jnp.sum(sv * kk, axis=0, keepdims=True)
        snew = sv * wd + sa * bb + v_scr[pl.ds(i, 1), :] * kq
        so_ref[i] = snew
        y_scr[pl.ds(i, 1), :] = jnp.sum(snew * r, axis=0, keepdims=True)
        return carry

    lax.fori_loop(0, HEAD_DIM, body, 0)
    y = _head_norm_cols(y_scr[...], RWKV_GN_EPS, True) * lng_ref[...] + lnb_ref[...]
    y = y + jnp.sum(r * kq * rk_ref[...], axis=0, keepdims=True) * v
    y_ref[...] = y * g


def _head_block(cols):
    return pl.BlockSpec((HEAD_DIM, cols), lambda h: (h, 0))


def _state_block(bsz):
    return pl.BlockSpec((None, HEAD_DIM, HEAD_DIM, bsz), lambda h: (h, 0, 0, 0))


def _rwkv_sample_call(p_t, shift_t, s_t, prm):
    bsz = p_t.shape[1]
    return pl.pallas_call(
        _rwkv_sample_kernel,
        grid=(N_HEADS,),
        in_specs=[_full(p_t.shape), _full(shift_t.shape), _state_block(bsz), _full((RWKV_COLS, 1)),
                  _head_block(1), _head_block(RWKV_W_RANK), _head_block(1), _head_block(RWKV_A_RANK),
                  _head_block(RWKV_G_RANK), _head_block(1), _head_block(1), _head_block(1), _head_block(1),
                  _head_block(1)],
        out_specs=[_head_block(bsz), _state_block(bsz)],
        out_shape=[jax.ShapeDtypeStruct((MIX_W, bsz), F32), jax.ShapeDtypeStruct(s_t.shape, F32)],
        scratch_shapes=[pltpu.VMEM((HEAD_DIM, bsz), F32), pltpu.VMEM((HEAD_DIM, bsz), F32)],
        compiler_params=_params("parallel"),
        name="rwkv_sample",
    )(p_t, shift_t, s_t, prm["mu_c"], prm["w0_c"], prm["wup_t"], prm["a0_c"], prm["aup_t"], prm["gup_t"],
      prm["kk_c"], prm["ka_c"], prm["rk_c"], prm["lng_c"], prm["lnb_c"])


def _kv_state_step(s_ref, so_ref, q_scr, k_scr, f_scr, v, decay):
    def body(i, o):
        dec = f_scr[pl.ds(i, 1), :] if decay is None else decay
        snew = s_ref[i] * dec + k_scr[pl.ds(i, 1), :] * v
        so_ref[i] = snew
        return o + q_scr[pl.ds(i, 1), :] * snew

    return lax.fori_loop(0, HEAD_DIM, body, jnp.zeros_like(v))


def _hgrn_sample_kernel(layer, p_ref, s_ref, lg_ref, ng_ref, y_ref, so_ref, q_scr, k_scr, f_scr):
    h = pl.program_id(0)
    W = MIX_W
    lg = lg_ref[...]
    l0, l1 = lg[:, 0:1], lg[:, 1:2]
    m = jnp.maximum(l0, l1)
    e0, e1 = jnp.exp(l0 - m), jnp.exp(l1 - m)
    z = e0 + e1
    lb0, lb1 = e0 / z, e1 / z
    lb = (lb0 - lb0) if layer == 0 else ((lb0 + lb1) - lb0)
    log_f = _hgrn_log_f(_head_rows(p_ref, W, h), lb)
    f = jnp.exp(log_f)
    q_scr[...] = jax.nn.silu(_head_rows(p_ref, 0, h))
    k_scr[...] = 1.0 - f
    f_scr[...] = f
    v = _head_rows(p_ref, 2 * W, h)
    g = _head_rows(p_ref, 3 * W, h)
    o = _kv_state_step(s_ref, so_ref, q_scr, k_scr, f_scr, v, None)
    y_ref[...] = _head_norm_cols(o, HEAD_NORM_EPS, False) * ng_ref[...] * jax.nn.silu(g)


def _hgrn_sample_call(p_t, s_t, logits_c, norm_g_c, layer):
    bsz = p_t.shape[1]
    return pl.pallas_call(
        functools.partial(_hgrn_sample_kernel, layer),
        grid=(N_HEADS,),
        in_specs=[_full(p_t.shape), _state_block(bsz), _head_block(2), _head_block(1)],
        out_specs=[_head_block(bsz), _state_block(bsz)],
        out_shape=[jax.ShapeDtypeStruct((MIX_W, bsz), F32), jax.ShapeDtypeStruct(s_t.shape, F32)],
        scratch_shapes=[pltpu.VMEM((HEAD_DIM, bsz), F32)] * 3,
        compiler_params=_params("parallel"),
        name="hgrn_sample",
    )(p_t, s_t, logits_c, norm_g_c)


def _ret_sample_kernel(pos0, p_ref, s_ref, inv_ref, y_ref, so_ref, q_scr, k_scr):
    h = pl.program_id(0)
    W = MIX_W
    ang = jnp.float32(pos0) * inv_ref[...]
    cosv, sinv = jnp.cos(ang), jnp.sin(ang)

    def rot(z):
        z1, z2 = z[0:HALF_DIM, :], z[HALF_DIM:HEAD_DIM, :]
        return jnp.concatenate([z1 * cosv - z2 * sinv, z1 * sinv + z2 * cosv], axis=0)

    q_scr[...] = rot(_head_rows(p_ref, 0, h))
    k_scr[...] = rot(_head_rows(p_ref, W, h)) * HEAD_DIM ** -0.5
    v = _head_rows(p_ref, 2 * W, h)
    g = _head_rows(p_ref, 3 * W, h)
    gamma = jnp.float32(math.exp(RET_LOG_GAMMA[N_HEADS - 1]))
    for hh in range(N_HEADS - 2, -1, -1):
        gamma = jnp.where(h == hh, jnp.float32(math.exp(RET_LOG_GAMMA[hh])), gamma)
    o = _kv_state_step(s_ref, so_ref, q_scr, k_scr, None, v, gamma)
    y_ref[...] = _head_norm_cols(o, HEAD_NORM_EPS, False) * jax.nn.silu(g)


def _ret_sample_call(p_t, s_t, pos0):
    bsz = p_t.shape[1]
    inv = RET_THETA ** (-np.arange(HALF_DIM).astype(np.float64) / HALF_DIM)
    inv = jnp.asarray(inv.astype(np.float32).reshape(HALF_DIM, 1))
    return pl.pallas_call(
        functools.partial(_ret_sample_kernel, pos0),
        grid=(N_HEADS,),
        in_specs=[_full(p_t.shape), _state_block(bsz), _full(inv.shape)],
        out_specs=[_head_block(bsz), _state_block(bsz)],
        out_shape=[jax.ShapeDtypeStruct((MIX_W, bsz), F32), jax.ShapeDtypeStruct(s_t.shape, F32)],
        scratch_shapes=[pltpu.VMEM((HEAD_DIM, bsz), F32)] * 2,
        compiler_params=_params("parallel"),
        name="ret_sample",
    )(p_t, s_t, inv)


def _mlstm_sample_kernel(p_ref, pg_ref, conv_ref, c_ref, n_ref, m_ref, cw_ref, cb_ref, igb_ref, fgb_ref, ng_ref,
                         y_ref, co_ref, no_ref, mo_ref, q_scr, k_scr):
    h = pl.program_id(0)
    W = MIX_W

    def conv_rows(base):
        idx = pl.ds(pl.multiple_of(base + h * HEAD_DIM, HEAD_DIM), HEAD_DIM)
        cw = cw_ref[idx, :]
        acc = cw[:, MLSTM_CONV - 1:MLSTM_CONV] * p_ref[idx, :]
        for j in range(MLSTM_CONV - 1):
            acc = acc + cw[:, j:j + 1] * conv_ref[j, idx, :]
        return jax.nn.silu(acc + cb_ref[idx, :])

    q = conv_rows(0)
    k = conv_rows(W) * HEAD_DIM ** -0.5
    v = _head_rows(p_ref, 2 * W, h)
    og = _head_rows(p_ref, 3 * W, h)
    ig = pg_ref[pl.ds(h, 1), :] + igb_ref[pl.ds(h, 1), :]
    lf = jax.nn.log_sigmoid(pg_ref[pl.ds(N_HEADS + h, 1), :] + fgb_ref[pl.ds(h, 1), :])
    m_prev = m_ref[pl.ds(h, 1), :]
    n0 = n_ref[...]
    m_inter = lf + m_prev
    m_t = jnp.maximum(m_inter, ig)
    sc = jnp.sum(q * k, axis=0, keepdims=True) * jnp.exp(ig - m_t)
    w_inter = jnp.exp(m_inter - m_t)
    a_prev = jnp.exp(lf + m_prev - m_t)
    a_s = jnp.exp(ig + lf - lf - m_t)
    q_scr[...] = q
    k_scr[...] = k * a_s

    def body(i, qc):
        c0 = c_ref[i]
        co_ref[i] = c0 * a_prev + k_scr[pl.ds(i, 1), :] * v
        return qc + q_scr[pl.ds(i, 1), :] * c0

    qc = lax.fori_loop(0, HEAD_DIM, body, jnp.zeros_like(v))
    num = sc * v + w_inter * qc
    den = sc + w_inter * jnp.sum(q * n0, axis=0, keepdims=True)
    hh = num / jnp.maximum(jnp.abs(den), jnp.exp(-m_t))
    no_ref[...] = n0 * a_prev + k * a_s
    mo_ref[pl.ds(h, 1), :] = m_t
    y_ref[...] = jax.nn.sigmoid(og) * (_head_norm_cols(hh, HEAD_NORM_EPS, True) * ng_ref[...])


def _mlstm_sample_call(p_t, pg_t, conv_t, c_t, n_t, m_t, prm):
    bsz = p_t.shape[1]
    nblk = pl.BlockSpec((None, HEAD_DIM, bsz), lambda h: (h, 0, 0))
    return pl.pallas_call(
        _mlstm_sample_kernel,
        grid=(N_HEADS,),
        in_specs=[_full(p_t.shape), _full(pg_t.shape), _full(conv_t.shape), _state_block(bsz), nblk,
                  _full(m_t.shape), _full((2 * MIX_W, MLSTM_CONV)), _full((2 * MIX_W, 1)),
                  _full((N_HEADS, 1)), _full((N_HEADS, 1)), _head_block(1)],
        out_specs=[_head_block(bsz), _state_block(bsz), nblk, _full(m_t.shape)],
        out_shape=[jax.ShapeDtypeStruct((MIX_W, bsz), F32), jax.ShapeDtypeStruct(c_t.shape, F32),
                   jax.ShapeDtypeStruct(n_t.shape, F32), jax.ShapeDtypeStruct(m_t.shape, F32)],
        scratch_shapes=[pltpu.VMEM((HEAD_DIM, bsz), F32)] * 2,
        compiler_params=_params("arbitrary"),
        name="mlstm_sample",
    )(p_t, pg_t, conv_t, c_t, n_t, m_t, prm["cw_t"], prm["cb_c"], prm["igb_c"], prm["fgb_c"], prm["ng_c"])


def _merge_kernel(x_ref, xn_ref, ya_ref, yb_ref, yc_ref, yd_ref, wg_ref, wb_ref, wo_ref, g2_ref, x1_ref, xn2_ref):
    gates = jax.nn.sigmoid(_dot(xn_ref[...], wg_ref[...]))
    merged = None
    for m, y_ref in enumerate((ya_ref, yb_ref, yc_ref, yd_ref)):
        br = _dot(y_ref[...].astype(BF16), wb_ref[m])
        term = gates[:, m * D_MODEL:(m + 1) * D_MODEL] * br
        merged = term if merged is None else merged + term
    x1 = x_ref[...] + _dot(merged.astype(BF16), wo_ref[...])
    x1_ref[...] = x1
    xn2 = x1 * lax.rsqrt(jnp.mean(x1 * x1, axis=-1, keepdims=True) + NORM_EPS) * g2_ref[...]
    xn2_ref[...] = xn2.T.astype(BF16)


def _merge_call(x2d, xn, ys, w_gate, w_branch, w_out, g2, tm):
    n = x2d.shape[0]
    row = lambda i: (i, 0)
    return pl.pallas_call(
        _merge_kernel,
        grid=(n // tm,),
        in_specs=[pl.BlockSpec((tm, D_MODEL), row)] * 2 + [pl.BlockSpec((tm, MIX_W), row)] * 4
        + [_full(w_gate.shape), _full(w_branch.shape), _full(w_out.shape), _full((1, D_MODEL))],
        out_specs=[pl.BlockSpec((tm, D_MODEL), row), pl.BlockSpec((D_MODEL, tm), lambda i: (0, i))],
        out_shape=[jax.ShapeDtypeStruct((n, D_MODEL), F32), jax.ShapeDtypeStruct((D_MODEL, n), BF16)],
        compiler_params=_params("parallel"),
        name="merge_out",
    )(x2d, xn, *ys, w_gate, w_branch, w_out, g2.reshape(1, D_MODEL))


def _top_ranked(s, count, break_ties):
    idx = _iota(s.shape, 0)
    rank = jnp.full(s.shape, float(count), F32)
    vals = []
    for r in range(count):
        mx = jnp.max(s, axis=0, keepdims=True)
        sel = s == mx
        if break_ties:
            sel = idx == jnp.min(jnp.where(sel, idx, s.shape[0]), axis=0, keepdims=True)
        rank = jnp.where(sel, float(r), rank)
        s = jnp.where(sel, NEG_BIG, s)
        vals.append(mx)
    ranked = jnp.sum((rank < float(count)).astype(F32), axis=0, keepdims=True)
    return vals, rank, ranked


_LATTICE_WIDTHS = tuple(PEER_TOPK // (i + 1) for i in range(PEER_TOPK))


def _peer_head_stats(s1, s2, break_ties):
    v1, rank1, n1 = _top_ranked(s1, PEER_TOPK, break_ties)
    v2, rank2, n2 = _top_ranked(s2, PEER_TOPK, break_ties)
    rows = [v1[i] + v2[j] for i in range(PEER_TOPK) for j in range(_LATTICE_WIDTHS[i])]
    pad = (-len(rows)) % 8
    cand = jnp.concatenate(rows + [jnp.full((pad, s1.shape[1]), NEG_BIG, F32)], axis=0)
    _, crank, _ = _top_ranked(cand, PEER_TOPK, True)
    chosen = crank < float(PEER_TOPK)
    z = jnp.sum(jnp.where(chosen, jnp.exp(cand - rows[0]), 0.0), axis=0, keepdims=True)
    taken = chosen.astype(F32)
    lim = jnp.zeros_like(s1)
    off = 0
    for i in range(PEER_TOPK):
        count_i = jnp.sum(taken[off:off + _LATTICE_WIDTHS[i], :], axis=0, keepdims=True)
        lim = jnp.where(rank1 == float(i), count_i, lim)
        off += _LATTICE_WIDTHS[i]
    tied = jnp.maximum(n1, n2) - float(PEER_TOPK)
    stats = (lim, jnp.exp(s1 - v1[0]) / z, rank2.astype(BF16), jnp.exp(s2 - v2[0]).astype(BF16))
    return stats, tied


def _peer_score_kernel(xn_ref, wq_ref, keys_ref, lim_ref, cf_ref, r2_ref, e2_ref, s_scr):
    q_t = _dot(wq_ref[...], xn_ref[...])
    out_refs = (lim_ref, cf_ref, r2_ref, e2_ref)
    tied = jnp.zeros((1, q_t.shape[1]), F32)
    q_hi = q_t.astype(BF16)
    q_lo = (q_t - q_hi.astype(F32)).astype(BF16)

    def scores(h, c):
        rows = slice((2 * h + c) * PEER_QHALF, (2 * h + c + 1) * PEER_QHALF)
        return _dot(keys_ref[h, c], jnp.concatenate([q_hi[rows], q_hi[rows], q_lo[rows]], axis=0))

    for h in range(PEER_HEADS):
        s1 = scores(h, 0)
        s2 = scores(h, 1)
        s_scr[h, 0] = s1
        s_scr[h, 1] = s2
        stats, tied_h = _peer_head_stats(s1, s2, False)
        for ref, val in zip(out_refs, stats):
            ref[h] = val
        tied = jnp.maximum(tied, tied_h)

    @pl.when(jnp.max(tied) > 0.0)
    def _():
        for h in range(PEER_HEADS):
            exact, _ = _peer_head_stats(s_scr[h, 0], s_scr[h, 1], True)
            for ref, val in zip(out_refs, exact):
                ref[h] = val


def _peer_score_call(xn2_t, wq_t, keys, tm):
    n = xn2_t.shape[1]
    blk = pl.BlockSpec((PEER_HEADS, PEER_N_KEYS, tm), lambda i: (0, 0, i))
    big = lambda dt: jax.ShapeDtypeStruct((PEER_HEADS, PEER_N_KEYS, n), dt)
    return pl.pallas_call(
        _peer_score_kernel,
        grid=(n // tm,),
        in_specs=[pl.BlockSpec((D_MODEL, tm), lambda i: (0, i)), _full(wq_t.shape), _full(keys.shape)],
        out_specs=[blk] * 4,
        out_shape=[big(F32), big(F32), big(BF16), big(BF16)],
        scratch_shapes=[pltpu.VMEM((PEER_HEADS, 2, PEER_N_KEYS, tm), F32)],
        compiler_params=_params("parallel"),
        name="peer_scores",
    )(xn2_t, wq_t, keys)


def _peer_expert_kernel(xn_ref, x1_ref, u_ref, vt_ref, lim_ref, cf_ref, r2_ref, e2_ref, *rest):
    gain_ref = rest[0] if len(rest) == 3 else None
    o_ref, acc_ref = rest[-2:]
    j = pl.program_id(1)

    @pl.when(j == 0)
    def _():
        acc_ref[...] = jnp.zeros_like(acc_ref)

    te = u_ref.shape[0]
    tm = xn_ref.shape[1]
    per_tile = te // PEER_N_KEYS
    pieces = []
    for al in range(per_tile):
        a = j * per_tile + al
        hid = _dot(u_ref[al * PEER_N_KEYS:(al + 1) * PEER_N_KEYS, :], xn_ref[...])
        act = hid * (1.0 + lax.erf(hid * INV_SQRT2))
        grouped = (PEER_N_KEYS // BF16_ROWS, BF16_ROWS, tm)
        gate = jnp.zeros(grouped, BF16)
        for h in range(PEER_HEADS):
            lim = jnp.broadcast_to(lim_ref[h, pl.ds(a, 1), :], (BF16_ROWS, tm)).astype(BF16)
            cf = jnp.broadcast_to(0.5 * cf_ref[h, pl.ds(a, 1), :], (BF16_ROWS, tm)).astype(BF16)
            r2 = r2_ref[h].reshape(grouped)
            e2 = e2_ref[h].reshape(grouped)
            gate = gate + jnp.where(r2 < lim[None], e2, jnp.zeros((), BF16)) * cf[None]
        pieces.append((act * gate.reshape(PEER_N_KEYS, tm).astype(F32)).astype(BF16))
    acc_ref[...] += _dot(vt_ref[...], jnp.concatenate(pieces, axis=0))

    @pl.when(j == pl.num_programs(1) - 1)
    def _():
        x = x1_ref[...] + acc_ref[...].T
        if gain_ref is not None:
            x = x * lax.rsqrt(jnp.mean(x * x, axis=-1, keepdims=True) + NORM_EPS) * gain_ref[...]
        o_ref[...] = x


def _peer_expert_call(xn2_t, x1, u, v_t, stats, tm, final_gain=None):
    n = xn2_t.shape[1]
    n_tiles, _, te = v_t.shape
    tok = pl.BlockSpec((tm, D_MODEL), lambda i, j: (i, 0))
    blk = pl.BlockSpec((PEER_HEADS, PEER_N_KEYS, tm), lambda i, j: (0, 0, i))
    gain = [] if final_gain is None else [final_gain.reshape(1, D_MODEL)]
    return pl.pallas_call(
        _peer_expert_kernel,
        grid=(n // tm, n_tiles),
        in_specs=[pl.BlockSpec((D_MODEL, tm), lambda i, j: (0, i)), tok,
                  pl.BlockSpec((te, D_MODEL), lambda i, j: (j, 0)),
                  pl.BlockSpec((None, D_MODEL, te), lambda i, j: (j, 0, 0)),
                  blk, blk, blk, blk] + [_full((1, D_MODEL))] * len(gain),
        out_specs=tok,
        out_shape=jax.ShapeDtypeStruct((n, D_MODEL), F32),
        scratch_shapes=[pltpu.VMEM((D_MODEL, tm), F32)],
        compiler_params=_params("parallel", "arbitrary"),
        name="peer_experts",
    )(xn2_t, x1, u, v_t, *stats, *gain)


def _hi_lo_hi(w):
    hi = w.astype(BF16)
    lo = (w - hi.astype(F32)).astype(BF16)
    return jnp.concatenate([hi, lo, hi], axis=-1)


def _layer_weights(l, w_in, rw, hg, ml, w_branch, w_out, ln1_g, ln2_g, peer_w_q, peer_keys, peer_u, peer_v):
    o3 = RWKV_COLS + 2 * 4 * MIX_W
    o4 = o3 + MLSTM_COLS
    gcol = o3 + 3 * MIX_W
    w = w_in[l]
    w_main = jnp.concatenate([w[:, 0:gcol], w[:, gcol + 2 * N_HEADS:o4]], axis=1).astype(BF16)
    w_mlg = jnp.pad(w[:, gcol:gcol + 2 * N_HEADS], ((0, 0), (0, LANES - 2 * N_HEADS)))
    w_mlg = _hi_lo_hi(w_mlg.T).T
    row = lambda a: a[l].reshape(1, -1)
    col = lambda a: a[l].reshape(-1, 1)
    rwkv = dict(mu=row(rw["mu"]), w0=row(rw["w0"]), wup=rw["w_up"][l], a0=row(rw["a0"]), aup=rw["a_up"][l],
                gup=rw["g_up"][l], kk=row(rw["k_k"]), ka=row(rw["k_a"]), rk=row(rw["r_k"]),
                lng=row(rw["ln_g"]), lnb=row(rw["ln_b"]),
                mu_c=col(rw["mu"]), w0_c=col(rw["w0"]), wup_t=rw["w_up"][l].T, a0_c=col(rw["a0"]),
                aup_t=rw["a_up"][l].T, gup_t=rw["g_up"][l].T, kk_c=col(rw["k_k"]), ka_c=col(rw["k_a"]),
                rk_c=col(rw["r_k"]), lng_c=col(rw["ln_g"]), lnb_c=col(rw["ln_b"]))
    gate_b = jnp.pad(jnp.concatenate([ml["ig_b"][l], ml["fg_b"][l]]), (0, LANES - 2 * N_HEADS)).reshape(1, LANES)
    mlstm = dict(cw=ml["conv_w"][l], cb=row(ml["conv_b"]), gate_b=gate_b, ng=row(ml["norm_g"]),
                 cw_t=ml["conv_w"][l].T, cb_c=col(ml["conv_b"]), igb_c=col(ml["ig_b"]), fgb_c=col(ml["fg_b"]),
                 ng_c=col(ml["norm_g"]))
    return dict(
        ln1=ln1_g[l], ln2=ln2_g[l], w_main=w_main, w_mlg=w_mlg, w_gate=w[:, o4:].astype(BF16),
        w_branch=w_branch[l].astype(BF16), w_out=w_out[l].astype(BF16),
        rwkv=rwkv, hgrn_logits=hg["lb_logits"], hgrn_logits_c=hg["lb_logits"].T,
        hgrn_ng=row(hg["norm_g"]), hgrn_ng_c=col(hg["norm_g"]), mlstm=mlstm,
        wq_t=peer_w_q[l].T.astype(BF16), keys=_hi_lo_hi(peer_keys[l]), u=peer_u[l].astype(BF16),
        v_t=jnp.swapaxes(peer_v[l].astype(BF16).reshape(-1, PEER_EXPERT_TILE, D_MODEL), 1, 2))


def _lanes_last(a):
    return jnp.moveaxis(a, 0, -1)


def _prompt_mixers(proj, bsz, lw, layer):
    _, pr, ph, pt, pm, pg = proj
    ya, shift, s_a = _rwkv_prompt_call(pr, bsz, lw["rwkv"])
    yb, s_b = _hgrn_prompt_call(ph, bsz, lw["hgrn_logits"], lw["hgrn_ng"], layer)
    yc, s_c = _ret_prompt_call(pt, bsz, 0)
    m = lw["mlstm"]
    yd, conv, c_d, n_d, m_d = _mlstm_prompt_call(pm, pg, bsz, m["cw"], m["cb"], m["gate_b"], m["ng"])
    states = (shift[:, 0, :], s_a, s_b, s_c, conv[:, 8 - (MLSTM_CONV - 1):, :], c_d, n_d, m_d[:, 0, :N_HEADS])
    return (ya, yb, yc, yd), states


def _sample_mixers(proj, st, lw, layer, pos0):
    _, pr, ph, pt, pm, pg = proj
    shift, s_a, s_b, s_c, conv, c_d, n_d, m_d = st
    to_t = lambda a: jnp.moveaxis(a, 0, -1)
    from_t = lambda a: jnp.moveaxis(a, -1, 0)
    ya, s_a2 = _rwkv_sample_call(pr.T, shift.T, to_t(s_a), lw["rwkv"])
    yb, s_b2 = _hgrn_sample_call(ph.T, to_t(s_b), lw["hgrn_logits_c"], lw["hgrn_ng_c"], layer)
    yc, s_c2 = _ret_sample_call(pt.T, to_t(s_c), pos0)
    yd, c_d2, n_d2, m_d2 = _mlstm_sample_call(pm.T, pg.T, to_t(conv), to_t(c_d), to_t(n_d), m_d.T, lw["mlstm"])
    conv2 = jnp.concatenate([conv[:, 1:, :], pm[:, None, 0:2 * MIX_W]], axis=1)
    states = (pr, from_t(s_a2), from_t(s_b2), from_t(s_c2), conv2, from_t(c_d2), from_t(n_d2), m_d2.T)
    return (ya.T, yb.T, yc.T, yd.T), states


def _token_tiles(n):
    pick = lambda want: want if n % want == 0 else n
    return pick(256), pick(256), pick(512)


def _run_trunk(x, states, pos0, weights, lnf_g, prompt):
    bsz, t, _ = x.shape
    n = bsz * t
    x2d = x.reshape(n, D_MODEL)
    tm, tm_score, tm_peer = _token_tiles(n)
    new_states = []
    for layer, lw in enumerate(weights):
        proj = _proj_call(x2d, lw["ln1"], lw["w_main"], lw["w_mlg"], tm)
        if prompt:
            ys, st_new = _prompt_mixers(proj, bsz, lw, layer)
        else:
            ys, st_new = _sample_mixers(proj, tuple(s[layer] for s in states), lw, layer, pos0)
        new_states.append(st_new)
        x1, xn2 = _merge_call(x2d, proj[0], ys, lw["w_gate"], lw["w_branch"], lw["w_out"], lw["ln2"], tm)
        stats = _peer_score_call(xn2, lw["wq_t"], lw["keys"], tm_score)
        last = layer == len(weights) - 1
        x2d = _peer_expert_call(xn2, x1, lw["u"], lw["v_t"], stats, tm_peer, lnf_g if last else None)
    y = x2d.reshape(bsz, t, D_MODEL)
    stacked = tuple(jnp.stack([s[i] for s in new_states], axis=0) for i in range(8))
    return y, stacked


def kernel(x_prompt, x_sample, state_rwkv_shift, state_rwkv_S, state_hgrn_S, state_ret_S, state_mlstm_conv, state_mlstm_C, state_mlstm_n, state_mlstm_m, ln1_g, w_in, rwkv_mu, rwkv_w0, rwkv_w_up, rwkv_a0, rwkv_a_up, rwkv_g_up, rwkv_k_k, rwkv_k_a, rwkv_r_k, rwkv_ln_g, rwkv_ln_b, hgrn_lb_logits, hgrn_norm_g, mlstm_conv_w, mlstm_conv_b, mlstm_ig_b, mlstm_fg_b, mlstm_norm_g, w_branch, w_out, ln2_g, peer_w_q, peer_keys, peer_u, peer_v, lnf_g):
    depth = w_in.shape[0]
    rw = dict(mu=rwkv_mu, w0=rwkv_w0, w_up=rwkv_w_up, a0=rwkv_a0, a_up=rwkv_a_up, g_up=rwkv_g_up,
              k_k=rwkv_k_k, k_a=rwkv_k_a, r_k=rwkv_r_k, ln_g=rwkv_ln_g, ln_b=rwkv_ln_b)
    hg = dict(lb_logits=hgrn_lb_logits, norm_g=hgrn_norm_g)
    ml = dict(conv_w=mlstm_conv_w, conv_b=mlstm_conv_b, ig_b=mlstm_ig_b, fg_b=mlstm_fg_b, norm_g=mlstm_norm_g)
    weights = [_layer_weights(l, w_in, rw, hg, ml, w_branch, w_out, ln1_g, ln2_g, peer_w_q, peer_keys,
                              peer_u, peer_v) for l in range(depth)]
    past_len = 16384
    y_prompt, p_states = _run_trunk(x_prompt, None, 0, weights, lnf_g, True)
    sample_init = (state_rwkv_shift, state_rwkv_S, state_hgrn_S, state_ret_S, state_mlstm_conv,
                   state_mlstm_C, state_mlstm_n, state_mlstm_m)
    y_sample, s_states = _run_trunk(x_sample, sample_init, past_len, weights, lnf_g, False)
    return (y_prompt, y_sample) + p_states + s_states
```

```python
import functools
import math

import numpy as np
import jax
import jax.numpy as jnp
from jax import lax
from jax.experimental import pallas as pl
from jax.experimental.pallas import tpu as pltpu

F32 = jnp.float32
BF16 = jnp.bfloat16
HI = lax.Precision.HIGHEST

D_MODEL = 1024
N_HEADS = 4
HEAD_DIM = 64
HALF_DIM = HEAD_DIM // 2
MIX_W = N_HEADS * HEAD_DIM
RWKV_W_RANK = 64
RWKV_A_RANK = 64
RWKV_G_RANK = 128
RWKV_COLS = 3 * MIX_W + RWKV_W_RANK + RWKV_A_RANK + RWKV_G_RANK
MLSTM_COLS = 4 * MIX_W + 2 * N_HEADS
GATE_COLS = 4 * D_MODEL
MLSTM_CONV = 4
CHUNK = 64
SUB = 16
RET_THETA = 10000.0
PEER_HEADS = 8
PEER_N_KEYS = 128
PEER_TOPK = 16
PEER_QHALF = 64
NORM_EPS = 1e-6
RWKV_GN_EPS = 64e-5
HEAD_NORM_EPS = 1e-5
MASK_NEG = -1e30
LB_FLOOR = 1e-30
NEG_BIG = -1e30
RANK_EXP0 = 100
INV_SQRT2 = 1.0 / math.sqrt(2.0)
RET_LOG_GAMMA = tuple(math.log(1.0 - 2.0 ** (-5.0 - h)) for h in range(N_HEADS))
LANES = 128
BF16_ROWS = 16
PEER_EXPERT_TILE = 2048
VMEM_LIMIT = 56 * 1024 * 1024


X1 = "bf16"


def _dot_dims(a, b, dims, prec):
    if prec == X1:
        return lax.dot_general(a.astype(BF16), b.astype(BF16), dims, preferred_element_type=F32)
    return lax.dot_general(a, b, dims, precision=prec, preferred_element_type=F32)


def _dot(a, b, prec=None):
    return _dot_dims(a, b, (((1,), (0,)), ((), ())), prec)


def _dot_nt(a, b, prec=None):
    return _dot_dims(a, b, (((1,), (1,)), ((), ())), prec)


def _dot_tn(a, b, prec=None):
    return _dot_dims(a, b, (((0,), (0,)), ((), ())), prec)


def _params(*sem):
    return pltpu.CompilerParams(dimension_semantics=sem, vmem_limit_bytes=VMEM_LIMIT)


def _full(shape):
    nd = len(shape)
    return pl.BlockSpec(shape, lambda *_: (0,) * nd)


def _iota(shape, axis):
    return lax.broadcasted_iota(jnp.int32, shape, axis)


def _shift_rows(cur, prev, j):
    row = _iota(cur.shape, 0)
    return jnp.where(row >= j, pltpu.roll(cur, j, 0), pltpu.roll(prev, j, 0))


def _head_norm_rows(y, eps, center):
    if center:
        y = y - jnp.mean(y, axis=1, keepdims=True)
    return y * lax.rsqrt(jnp.mean(y * y, axis=1, keepdims=True) + eps)


def _same_head_mask(width):
    return (_iota((width, width), 0) // HEAD_DIM) == (_iota((width, width), 1) // HEAD_DIM)


def _per_head_row(values, width):
    head = _iota((1, width), 1) // HEAD_DIM
    row = jnp.full((1, width), values[-1], F32)
    for h in range(len(values) - 2, -1, -1):
        row = jnp.where(head == h, values[h], row)
    return row


def _block_diag(x, same_head):
    reps = same_head.shape[0] // x.shape[0]
    return jnp.where(same_head, jnp.concatenate([x] * reps, axis=0), 0.0)


def _head_mean_lanes(y, same_head, split=True):
    e = jnp.where(same_head, 1.0 / HEAD_DIM, 0.0).astype(BF16)
    hi = y.astype(BF16)
    if not split:
        return _dot(hi, e)
    lo = (y - hi.astype(F32)).astype(BF16)
    return _dot(jnp.concatenate([hi, lo], axis=1), jnp.concatenate([e, e], axis=0))


def _head_norm_lanes(y, same_head, eps, center):
    if center:
        y = y - _head_mean_lanes(y, same_head)
    return y * lax.rsqrt(_head_mean_lanes(y * y, same_head) + eps)


def _head_norm_cols(y, eps, center):
    if center:
        y = y - jnp.mean(y, axis=0, keepdims=True)
    return y * lax.rsqrt(jnp.mean(y * y, axis=0, keepdims=True) + eps)


def _proj_kernel(x_ref, g_ref, w_ref, wg_ref, xn_ref, pr_ref, ph_ref, pt_ref, pm_ref, pg_ref):
    x = x_ref[...]
    xn = x * lax.rsqrt(jnp.mean(x * x, axis=-1, keepdims=True) + NORM_EPS) * g_ref[...]
    xb = xn.astype(BF16)
    xn_ref[...] = xb
    p = _dot(xb, w_ref[...])
    pr_ref[...] = p[:, 0:1024]
    ph_ref[...] = p[:, 1024:2048]
    pt_ref[...] = p[:, 2048:3072]
    pm_ref[...] = p[:, 3072:4096]
    x_lo = (xn - xb.astype(F32)).astype(BF16)
    pg_ref[...] = _dot(jnp.concatenate([xb, xb, x_lo], axis=1), wg_ref[...])


def _proj_call(x2d, g, w_main, w_mlg, tm):
    n = x2d.shape[0]
    row = lambda i: (i, 0)
    out = lambda c, dt: jax.ShapeDtypeStruct((n, c), dt)
    return pl.pallas_call(
        _proj_kernel,
        grid=(n // tm,),
        in_specs=[pl.BlockSpec((tm, D_MODEL), row), _full((1, D_MODEL)),
                  _full((D_MODEL, 4096)), _full((3 * D_MODEL, LANES))],
        out_specs=[pl.BlockSpec((tm, D_MODEL), row)] * 5 + [pl.BlockSpec((tm, LANES), row)],
        out_shape=[out(D_MODEL, BF16)] + [out(D_MODEL, F32)] * 4 + [out(LANES, F32)],
        compiler_params=_params("parallel"),
        name="norm_proj",
    )(x2d, g.reshape(1, D_MODEL), w_main, w_mlg)


def _cumsum_rows(x):
    row = _iota(x.shape, 0)
    shift = 1
    while shift < x.shape[0]:
        x = x + jnp.where(row >= shift, pltpu.roll(x, shift, 0), 0.0)
        shift *= 2
    return x


def _rwkv_prompt_kernel(p_ref, mu_ref, w0_ref, wup_ref, a0_ref, aup_ref, gup_ref, kk_ref, ka_ref, rk_ref,
                        lng_ref, lnb_ref, y_ref, shift_ref, s_ref, prow_ref, sbd_ref):
    p = p_ref[...]
    L = p.shape[0]
    row = _iota(p.shape, 0)
    prev = jnp.where(row >= 1, pltpu.roll(p, 1, 0), jnp.broadcast_to(prow_ref[...], p.shape))
    last = p[L - 1:L, :]
    prow_ref[...] = last
    shift_ref[...] = last
    xs = p + mu_ref[...] * (prev - p)
    W = MIX_W
    r, k, v = xs[:, 0:W], xs[:, W:2 * W], xs[:, 2 * W:3 * W]
    o = 3 * W
    xw = xs[:, o:o + RWKV_W_RANK]
    o += RWKV_W_RANK
    xa = xs[:, o:o + RWKV_A_RANK]
    o += RWKV_A_RANK
    xg = xs[:, o:o + RWKV_G_RANK]
    w = w0_ref[...] + _dot(jnp.tanh(xw), wup_ref[...], X1)
    logw = -jnp.exp(-jax.nn.softplus(-w) - 0.5)
    a = jax.nn.sigmoid(a0_ref[...] + _dot(xa, aup_ref[...], X1))
    g = _dot(jax.nn.sigmoid(xg), gup_ref[...], X1)
    same_head = _same_head_mask(W)
    kk = k * kk_ref[...]
    kk = kk / jnp.maximum(jnp.sqrt(HEAD_DIM * _head_mean_lanes(kk * kk, same_head)), 1e-12)
    kq = k * (1.0 + (a - 1.0) * ka_ref[...])
    bb = kk * a

    cum = _cumsum_rows(logw)
    cum_l = cum[L - 1:L, :]
    e_neg = jnp.exp(-cum)
    e_tail = jnp.exp(cum_l - cum)
    lhs = jnp.concatenate([kk * jnp.exp(cum - logw), r * jnp.exp(cum)], axis=0)
    s0 = sbd_ref[...]
    yield
    gb = _dot_nt(lhs, _block_diag(bb * e_neg, same_head), X1)
    yield
    gk = _dot_nt(lhs, _block_diag(kq * e_neg, same_head), X1)
    qs = _dot_nt(lhs, s0, X1)
    t_idx = _iota((L, W), 0)
    s_idx = _iota((L, W), 1) % HEAD_DIM
    strict = t_idx > s_idx
    incl = t_idx >= s_idx
    n_k = jnp.where(strict, gk[0:L], 0.0)
    n_b = jnp.where(strict, gb[0:L], 0.0)
    m_k = jnp.where(incl, gk[L:2 * L], 0.0)
    m_b = jnp.where(incl, gb[L:2 * L], 0.0)
    tinv = (t_idx == s_idx).astype(F32) - n_b
    pw = n_b
    for _ in range(5):
        yield
        pw = _dot(pw, _block_diag(pw, same_head), X1)
        yield
        tinv = tinv + _dot(tinv, _block_diag(pw, same_head), X1)
    kv = _dot(jnp.concatenate([n_k, m_k], axis=0), _block_diag(v, same_head), X1)
    yield
    u = _dot(tinv, _block_diag(qs[0:L] + kv[0:L], same_head), X1)
    yield
    y = qs[L:2 * L] + kv[L:2 * L] - _dot(m_b, _block_diag(u, same_head), X1)
    upd = _dot_tn(jnp.concatenate([v, -u], axis=0),
                  jnp.concatenate([kq * e_tail, bb * e_tail], axis=0), X1)
    yield
    s_new = s0 * jnp.exp(cum_l) + jnp.where(same_head, upd, 0.0)
    sbd_ref[...] = s_new
    for h in range(N_HEADS):
        sl = slice(h * HEAD_DIM, (h + 1) * HEAD_DIM)
        s_ref[h] = s_new[sl, sl]
    y = _head_norm_lanes(y, same_head, RWKV_GN_EPS, True) * lng_ref[...] + lnb_ref[...]
    y = y + HEAD_DIM * _head_mean_lanes(r * kq * rk_ref[...], same_head) * v
    y_ref[...] = y * g


def _per_sequence(body, batched, carried):
    def kernel(*refs):
        @pl.when(pl.program_id(1) == 0)
        def _():
            for i in carried:
                refs[i][...] = jnp.zeros_like(refs[i])

        group = next(r.shape[0] for r, b in zip(refs, batched) if b)
        runs = [body(*[r.at[gi] if b else r for r, b in zip(refs, batched)]) for gi in range(group)]
        runs = [r for r in runs if r is not None]
        while runs:
            for r in list(runs):
                if next(r, _DONE) is _DONE:
                    runs.remove(r)

    return kernel


_DONE = object()


def _seq_group(bsz, want=4):
    return next(g for g in (8, 4, 2, 1) if g <= want and bsz % g == 0)


def _chunk_specs(grp, width):
    return pl.BlockSpec((grp, CHUNK, width), lambda b, c: (b, c, 0))


def _per_seq_spec(grp, *dims):
    return pl.BlockSpec((grp,) + dims, lambda b, c: (b,) + (0,) * len(dims))


def _state_spec(grp):
    return _per_seq_spec(grp, N_HEADS, HEAD_DIM, HEAD_DIM)


def _rwkv_prompt_call(p, bsz, prm):
    n = p.shape[0]
    t = n // bsz
    grp = _seq_group(bsz)
    small = [prm[k] for k in ("mu", "w0", "wup", "a0", "aup", "gup", "kk", "ka", "rk", "lng", "lnb")]
    y, shift, s = pl.pallas_call(
        _per_sequence(_rwkv_prompt_kernel, (True,) + (False,) * len(small) + (True,) * 5,
                      carried=(len(small) + 4, len(small) + 5)),
        grid=(bsz // grp, t // CHUNK),
        in_specs=[_chunk_specs(grp, RWKV_COLS)] + [_full(a.shape) for a in small],
        out_specs=[_chunk_specs(grp, MIX_W), _per_seq_spec(grp, 1, RWKV_COLS), _state_spec(grp)],
        out_shape=[jax.ShapeDtypeStruct((bsz, t, MIX_W), F32),
                   jax.ShapeDtypeStruct((bsz, 1, RWKV_COLS), F32),
                   jax.ShapeDtypeStruct((bsz, N_HEADS, HEAD_DIM, HEAD_DIM), F32)],
        scratch_shapes=[pltpu.VMEM((grp, 1, RWKV_COLS), F32), pltpu.VMEM((grp, MIX_W, MIX_W), F32)],
        compiler_params=_params("parallel", "arbitrary"),
        name="rwkv_prompt",
    )(p.reshape(bsz, t, RWKV_COLS), *small)
    return y.reshape(n, MIX_W), shift, s


def _hgrn_lower_bound(logits, layer):
    l0, l1 = logits[0:1, :], logits[1:2, :]
    m = jnp.maximum(l0, l1)
    e0, e1 = jnp.exp(l0 - m), jnp.exp(l1 - m)
    z = e0 + e1
    lb0, lb1 = e0 / z, e1 / z
    return (lb0 - lb0) if layer == 0 else ((lb0 + lb1) - lb0)


def _hgrn_log_f(z, lb):
    return jnp.logaddexp(jnp.log(jnp.maximum(lb, LB_FLOOR)), jnp.log1p(-lb) + jax.nn.log_sigmoid(z))


def _hgrn_prompt_kernel(layer, p_ref, lg_ref, ng_ref, y_ref, s_ref, st_ref):
    p = p_ref[...]
    L = p.shape[0]
    W = MIX_W
    q = jax.nn.silu(p[:, 0:W])
    lb = _hgrn_lower_bound(lg_ref[...], layer)
    log_f = _hgrn_log_f(p[:, W:2 * W], lb)
    k = 1.0 - jnp.exp(log_f)
    v = p[:, 2 * W:3 * W]
    g = p[:, 3 * W:4 * W]
    same_head = _same_head_mask(W)
    row = _iota((L, W), 0) % SUB
    b = log_f
    shift = 1
    while shift < SUB:
        b = b + jnp.where(row >= shift, pltpu.roll(b, shift, 0), 0.0)
        shift *= 2
    qe = q * jnp.exp(b)
    intra = jnp.zeros((L, W), F32)
    for lag in range(SUB):
        valid = row >= lag
        kr = k if lag == 0 else pltpu.roll(k, lag, 0)
        br = b if lag == 0 else pltpu.roll(b, lag, 0)
        vr = v if lag == 0 else pltpu.roll(v, lag, 0)
        term = jnp.where(valid, q * kr * jnp.exp(jnp.where(valid, b - br, 0.0)), 0.0)
        intra = intra + HEAD_DIM * _head_mean_lanes(term, same_head, split=False) * vr
        if lag % 4 == 3:
            yield
    st = st_ref[...]
    pieces = []
    for j in range(L // SUB):
        rs = slice(j * SUB, (j + 1) * SUB)
        pieces.append(_dot_nt(qe[rs], st, X1))
        b_l = b[(j + 1) * SUB - 1:(j + 1) * SUB]
        kd = k[rs] * jnp.exp(b_l - b[rs])
        st = st * jnp.exp(b_l) + jnp.where(same_head, _dot_tn(v[rs], kd, X1), 0.0)
        yield
    st_ref[...] = st
    for h in range(N_HEADS):
        sl = slice(h * HEAD_DIM, (h + 1) * HEAD_DIM)
        s_ref[h] = st[sl, sl].T
    o = jnp.concatenate(pieces, axis=0) + intra
    y_ref[...] = _head_norm_lanes(o, same_head, HEAD_NORM_EPS, False) * ng_ref[...] * jax.nn.silu(g)


def _hgrn_prompt_call(p, bsz, logits, norm_g, layer):
    n = p.shape[0]
    t = n // bsz
    grp = _seq_group(bsz)
    y, s = pl.pallas_call(
        _per_sequence(functools.partial(_hgrn_prompt_kernel, layer), (True, False, False, True, True, True),
                      carried=(5,)),
        grid=(bsz // grp, t // CHUNK),
        in_specs=[_chunk_specs(grp, 4 * MIX_W), _full(logits.shape), _full(norm_g.shape)],
        out_specs=[_chunk_specs(grp, MIX_W), _state_spec(grp)],
        out_shape=[jax.ShapeDtypeStruct((bsz, t, MIX_W), F32),
                   jax.ShapeDtypeStruct((bsz, N_HEADS, HEAD_DIM, HEAD_DIM), F32)],
        scratch_shapes=[pltpu.VMEM((grp, MIX_W, MIX_W), F32)],
        compiler_params=_params("parallel", "arbitrary"),
        name="hgrn_prompt",
    )(p.reshape(bsz, t, 4 * MIX_W), logits, norm_g)
    return y.reshape(n, MIX_W), s


def _ret_prompt_kernel(pos0, p_ref, inv_ref, y_ref, s_ref, sbd_ref):
    c = pl.program_id(1)
    p = p_ref[...]
    L = p.shape[0]
    W = MIX_W
    pos = (pos0 + c * L + _iota((L, W), 0)).astype(F32)
    ang = pos * inv_ref[...]
    cosv, sinv = jnp.cos(ang), jnp.sin(ang)
    first = (_iota((L, W), 1) % HEAD_DIM) < HALF_DIM

    def rot(z):
        partner = jnp.where(first, -pltpu.roll(z, W - HALF_DIM, 1), pltpu.roll(z, HALF_DIM, 1))
        return z * cosv + partner * sinv

    q = rot(p[:, 0:W])
    k = rot(p[:, W:2 * W]) * HEAD_DIM ** -0.5
    v = p[:, 2 * W:3 * W]
    g = p[:, 3 * W:4 * W]
    same_head = _same_head_mask(W)
    lg = _per_head_row(RET_LOG_GAMMA, W)
    t_row = _iota((L, W), 0).astype(F32)
    rel = t_row - (_iota((L, W), 1) % HEAD_DIM).astype(F32)
    dmat = jnp.where(rel >= 0, jnp.exp(lg * jnp.maximum(rel, 0.0)), 0.0)
    d_in = jnp.exp(lg * (t_row + 1.0))
    d_st = jnp.exp(lg * (L - 1.0 - t_row))
    s0 = sbd_ref[...]
    yield
    att = _dot_nt(q, _block_diag(k, same_head), X1) * dmat
    inter = _dot(q, s0, X1) * d_in
    yield
    o = _dot(att, _block_diag(v, same_head), X1) + inter
    s_new = s0 * jnp.exp(lg * float(L)) + jnp.where(same_head, _dot_tn(k * d_st, v, X1), 0.0)
    yield
    sbd_ref[...] = s_new
    for h in range(N_HEADS):
        sl = slice(h * HEAD_DIM, (h + 1) * HEAD_DIM)
        s_ref[h] = s_new[sl, sl]
    y_ref[...] = _head_norm_lanes(o, same_head, HEAD_NORM_EPS, False) * jax.nn.silu(g)


def _rotary_inv_row():
    inv = RET_THETA ** (-(np.arange(MIX_W) % HALF_DIM).astype(np.float64) / HALF_DIM)
    return jnp.asarray(inv.astype(np.float32).reshape(1, MIX_W))


def _ret_prompt_call(p, bsz, pos0):
    n = p.shape[0]
    t = n // bsz
    grp = _seq_group(bsz, 8)
    inv = _rotary_inv_row()
    y, s = pl.pallas_call(
        _per_sequence(functools.partial(_ret_prompt_kernel, pos0), (True, False, True, True, True), carried=(4,)),
        grid=(bsz // grp, t // CHUNK),
        in_specs=[_chunk_specs(grp, 4 * MIX_W), _full(inv.shape)],
        out_specs=[_chunk_specs(grp, MIX_W), _state_spec(grp)],
        out_shape=[jax.ShapeDtypeStruct((bsz, t, MIX_W), F32),
                   jax.ShapeDtypeStruct((bsz, N_HEADS, HEAD_DIM, HEAD_DIM), F32)],
        scratch_shapes=[pltpu.VMEM((grp, MIX_W, MIX_W), F32)],
        compiler_params=_params("parallel", "arbitrary"),
        name="ret_prompt",
    )(p.reshape(bsz, t, 4 * MIX_W), inv)
    return y.reshape(n, MIX_W), s


def _mlstm_prompt_kernel(p_ref, pg_ref, cw_ref, cb_ref, gb_ref, ng_ref,
                         y_ref, conv_ref, c_ref, n_ref, m_ref, prev_ref, cbd_ref, nrow_ref, mrow_ref):
    p = p_ref[...]
    L = p.shape[0]
    W = MIX_W
    raw = p[:, 0:2 * W]
    prev = prev_ref[...]
    acc = cw_ref[MLSTM_CONV - 1:MLSTM_CONV, :] * raw
    for j in range(1, MLSTM_CONV):
        acc = acc + cw_ref[MLSTM_CONV - 1 - j:MLSTM_CONV - j, :] * _shift_rows(raw, prev, j)
    prev_ref[...] = raw
    conv_ref[...] = raw[L - 8:L, :]
    qk = jax.nn.silu(acc + cb_ref[...])
    q_all = qk[:, 0:W]
    k_all = qk[:, W:2 * W] * HEAD_DIM ** -0.5
    v_all = p[:, 2 * W:3 * W]
    og = p[:, 3 * W:4 * W]

    gb = pg_ref[...] + gb_ref[...]
    bcum = _cumsum_rows(jax.nn.log_sigmoid(gb))
    gb_t = gb.T
    bcum_t = bcum.T
    spread = lambda a, base: jnp.concatenate(
        [jnp.broadcast_to(a[:, base + h:base + h + 1], (L, HEAD_DIM)) for h in range(N_HEADS)], axis=1)
    rows = lambda a_t, base: jnp.concatenate([a_t[base + h:base + h + 1, :] for h in range(N_HEADS)], axis=1)
    ig_col, b_col = spread(gb, 0), spread(bcum, N_HEADS)
    ig_row, b_row = rows(gb_t, 0), rows(bcum_t, N_HEADS)
    same_head = _same_head_mask(W)
    causal = _iota((L, W), 0) >= (_iota((L, W), 1) % HEAD_DIM)
    m_prev = mrow_ref[...]
    log_d = jnp.where(causal, b_col - b_row + ig_row, MASK_NEG)
    m_inter = b_col + m_prev
    row_max = jnp.concatenate(
        [jnp.broadcast_to(jnp.max(log_d[:, h * HEAD_DIM:(h + 1) * HEAD_DIM], axis=1, keepdims=True), (L, HEAD_DIM))
         for h in range(N_HEADS)], axis=1)
    m_t = jnp.maximum(m_inter, row_max)
    c0 = cbd_ref[...]
    n0 = nrow_ref[...]
    yield
    sc = _dot_nt(q_all, _block_diag(k_all, same_head), X1) * jnp.exp(log_d - m_t)
    w_inter = jnp.exp(m_inter - m_t)
    inter = w_inter * _dot(q_all, c0, X1)
    m_l = m_t[L - 1:L, :]
    b_l = b_col[L - 1:L, :]
    a_prev = jnp.exp(b_l + m_prev - m_l)
    ks = k_all * jnp.exp(ig_col + b_l - b_col - m_l)
    yield
    num = _dot(sc, _block_diag(v_all, same_head), X1) + inter
    c_new = c0 * a_prev + jnp.where(same_head, _dot_tn(ks, v_all, X1), 0.0)
    yield
    den = HEAD_DIM * (_head_mean_lanes(sc, same_head) + w_inter * _head_mean_lanes(q_all * n0, same_head))
    hh = num / jnp.maximum(jnp.abs(den), jnp.exp(-m_t))
    n_new = n0 * a_prev + jnp.sum(ks, axis=0, keepdims=True)
    yield
    cbd_ref[...] = c_new
    nrow_ref[...] = n_new
    mrow_ref[...] = m_l
    m_out = jnp.zeros(m_ref.shape, F32)
    m_first = jnp.logical_and(_iota(m_ref.shape, 0) == 0, _iota(m_ref.shape, 1) < N_HEADS)
    for h in range(N_HEADS):
        sl = slice(h * HEAD_DIM, (h + 1) * HEAD_DIM)
        c_ref[h] = c_new[sl, sl]
        n_ref[h:h + 1, :] = n_new[:, sl]
        m_out = jnp.where(jnp.logical_and(m_first, _iota(m_ref.shape, 1) == h), m_l[:, h * HEAD_DIM:h * HEAD_DIM + 1], m_out)
    m_ref[...] = m_out
    hn = _head_norm_lanes(hh, same_head, HEAD_NORM_EPS, True) * ng_ref[...]
    y_ref[...] = jax.nn.sigmoid(og) * hn


def _mlstm_prompt_call(p, pg, bsz, conv_w, conv_b, gate_b, norm_g):
    n = p.shape[0]
    t = n // bsz
    grp = _seq_group(bsz, 8)
    outs = pl.pallas_call(
        _per_sequence(_mlstm_prompt_kernel, (True, True) + (False,) * 4 + (True,) * 9, carried=(11, 12, 13, 14)),
        grid=(bsz // grp, t // CHUNK),
        in_specs=[_chunk_specs(grp, 4 * MIX_W), _chunk_specs(grp, LANES), _full(conv_w.shape),
                  _full(conv_b.shape), _full(gate_b.shape), _full(norm_g.shape)],
        out_specs=[_chunk_specs(grp, MIX_W),
                   _per_seq_spec(grp, 8, 2 * MIX_W),
                   _state_spec(grp),
                   _per_seq_spec(grp, N_HEADS, HEAD_DIM),
                   _per_seq_spec(grp, 8, LANES)],
        out_shape=[jax.ShapeDtypeStruct((bsz, t, MIX_W), F32),
                   jax.ShapeDtypeStruct((bsz, 8, 2 * MIX_W), F32),
                   jax.ShapeDtypeStruct((bsz, N_HEADS, HEAD_DIM, HEAD_DIM), F32),
                   jax.ShapeDtypeStruct((bsz, N_HEADS, HEAD_DIM), F32),
                   jax.ShapeDtypeStruct((bsz, 8, LANES), F32)],
        scratch_shapes=[pltpu.VMEM((grp, CHUNK, 2 * MIX_W), F32), pltpu.VMEM((grp, MIX_W, MIX_W), F32),
                        pltpu.VMEM((grp, 1, MIX_W), F32), pltpu.VMEM((grp, 1, MIX_W), F32)],
        compiler_params=_params("parallel", "arbitrary"),
        name="mlstm_prompt",
    )(p.reshape(bsz, t, 4 * MIX_W), pg.reshape(bsz, t, LANES), conv_w, conv_b, gate_b, norm_g)
    return (outs[0].reshape(n, MIX_W),) + tuple(outs[1:])


def _head_rows(ref, base, h):
    return ref[pl.ds(pl.multiple_of(base + h * HEAD_DIM, HEAD_DIM), HEAD_DIM), :]


def _rwkv_sample_kernel(p_ref, sh_ref, s_ref, mu_ref, w0_ref, wup_ref, a0_ref, aup_ref, gup_ref, kk_ref, ka_ref,
                        rk_ref, lng_ref, lnb_ref, y_ref, so_ref, v_scr, y_scr):
    h = pl.program_id(0)
    W = MIX_W

    def xs_rows(lo, size, dyn):
        idx = pl.ds(pl.multiple_of(lo + h * HEAD_DIM, HEAD_DIM), size) if dyn else pl.ds(lo, size)
        pv = p_ref[idx, :]
        return pv + mu_ref[idx, :] * (sh_ref[idx, :] - pv)

    r = xs_rows(0, HEAD_DIM, True)
    k = xs_rows(W, HEAD_DIM, True)
    v = xs_rows(2 * W, HEAD_DIM, True)
    xw = xs_rows(3 * W, RWKV_W_RANK, False)
    xa = xs_rows(3 * W + RWKV_W_RANK, RWKV_A_RANK, False)
    xg = xs_rows(3 * W + RWKV_W_RANK + RWKV_A_RANK, RWKV_G_RANK, False)
    w = w0_ref[...] + _dot(wup_ref[...], jnp.tanh(xw), HI)
    wd = jnp.exp(-jnp.exp(-jax.nn.softplus(-w) - 0.5))
    a = jax.nn.sigmoid(a0_ref[...] + _dot(aup_ref[...], xa, HI))
    g = _dot(gup_ref[...], jax.nn.sigmoid(xg), HI)
    kk = k * kk_ref[...]
    kk = kk / jnp.maximum(jnp.sqrt(jnp.sum(kk * kk, axis=0, keepdims=True)), 1e-12)
    kq = k * (1.0 + (a - 1.0) * ka_ref[...])
    bb = kk * a
    v_scr[...] = v

    def body(i, carry):
        sv = s_ref[i]
        sa = -jnp.sum(sv * kk, axis=0, keepdims=True)
        snew = sv * wd + sa * bb + v_scr[pl.ds(i, 1), :] * kq
        so_ref[i] = snew
        y_scr[pl.ds(i, 1), :] = jnp.sum(snew * r, axis=0, keepdims=True)
        return carry

    lax.fori_loop(0, HEAD_DIM, body, 0)
    y = _head_norm_cols(y_scr[...], RWKV_GN_EPS, True) * lng_ref[...] + lnb_ref[...]
    y = y + jnp.sum(r * kq * rk_ref[...], axis=0, keepdims=True) * v
    y_ref[...] = y * g


def _head_block(cols):
    return pl.BlockSpec((HEAD_DIM, cols), lambda h: (h, 0))


def _state_block(bsz):
    return pl.BlockSpec((None, HEAD_DIM, HEAD_DIM, bsz), lambda h: (h, 0, 0, 0))


def _rwkv_sample_call(p_t, shift_t, s_t, prm):
    bsz = p_t.shape[1]
    return pl.pallas_call(
        _rwkv_sample_kernel,
        grid=(N_HEADS,),
        in_specs=[_full(p_t.shape), _full(shift_t.shape), _state_block(bsz), _full((RWKV_COLS, 1)),
                  _head_block(1), _head_block(RWKV_W_RANK), _head_block(1), _head_block(RWKV_A_RANK),
                  _head_block(RWKV_G_RANK), _head_block(1), _head_block(1), _head_block(1), _head_block(1),
                  _head_block(1)],
        out_specs=[_head_block(bsz), _state_block(bsz)],
        out_shape=[jax.ShapeDtypeStruct((MIX_W, bsz), F32), jax.ShapeDtypeStruct(s_t.shape, F32)],
        scratch_shapes=[pltpu.VMEM((HEAD_DIM, bsz), F32), pltpu.VMEM((HEAD_DIM, bsz), F32)],
        compiler_params=_params("parallel"),
        name="rwkv_sample",
    )(p_t, shift_t, s_t, prm["mu_c"], prm["w0_c"], prm["wup_t"], prm["a0_c"], prm["aup_t"], prm["gup_t"],
      prm["kk_c"], prm["ka_c"], prm["rk_c"], prm["lng_c"], prm["lnb_c"])


def _kv_state_step(s_ref, so_ref, q_scr, k_scr, f_scr, v, decay):
    def body(i, o):
        dec = f_scr[pl.ds(i, 1), :] if decay is None else decay
        snew = s_ref[i] * dec + k_scr[pl.ds(i, 1), :] * v
        so_ref[i] = snew
        return o + q_scr[pl.ds(i, 1), :] * snew

    return lax.fori_loop(0, HEAD_DIM, body, jnp.zeros_like(v))


def _hgrn_sample_kernel(layer, p_ref, s_ref, lg_ref, ng_ref, y_ref, so_ref, q_scr, k_scr, f_scr):
    h = pl.program_id(0)
    W = MIX_W
    lg = lg_ref[...]
    l0, l1 = lg[:, 0:1], lg[:, 1:2]
    m = jnp.maximum(l0, l1)
    e0, e1 = jnp.exp(l0 - m), jnp.exp(l1 - m)
    z = e0 + e1
    lb0, lb1 = e0 / z, e1 / z
    lb = (lb0 - lb0) if layer == 0 else ((lb0 + lb1) - lb0)
    log_f = _hgrn_log_f(_head_rows(p_ref, W, h), lb)
    f = jnp.exp(log_f)
    q_scr[...] = jax.nn.silu(_head_rows(p_ref, 0, h))
    k_scr[...] = 1.0 - f
    f_scr[...] = f
    v = _head_rows(p_ref, 2 * W, h)
    g = _head_rows(p_ref, 3 * W, h)
    o = _kv_state_step(s_ref, so_ref, q_scr, k_scr, f_scr, v, None)
    y_ref[...] = _head_norm_cols(o, HEAD_NORM_EPS, False) * ng_ref[...] * jax.nn.silu(g)


def _hgrn_sample_call(p_t, s_t, logits_c, norm_g_c, layer):
    bsz = p_t.shape[1]
    return pl.pallas_call(
        functools.partial(_hgrn_sample_kernel, layer),
        grid=(N_HEADS,),
        in_specs=[_full(p_t.shape), _state_block(bsz), _head_block(2), _head_block(1)],
        out_specs=[_head_block(bsz), _state_block(bsz)],
        out_shape=[jax.ShapeDtypeStruct((MIX_W, bsz), F32), jax.ShapeDtypeStruct(s_t.shape, F32)],
        scratch_shapes=[pltpu.VMEM((HEAD_DIM, bsz), F32)] * 3,
        compiler_params=_params("parallel"),
        name="hgrn_sample",
    )(p_t, s_t, logits_c, norm_g_c)


def _ret_sample_kernel(pos0, p_ref, s_ref, inv_ref, y_ref, so_ref, q_scr, k_scr):
    h = pl.program_id(0)
    W = MIX_W
    ang = jnp.float32(pos0) * inv_ref[...]
    cosv, sinv = jnp.cos(ang), jnp.sin(ang)

    def rot(z):
        z1, z2 = z[0:HALF_DIM, :], z[HALF_DIM:HEAD_DIM, :]
        return jnp.concatenate([z1 * cosv - z2 * sinv, z1 * sinv + z2 * cosv], axis=0)

    q_scr[...] = rot(_head_rows(p_ref, 0, h))
    k_scr[...] = rot(_head_rows(p_ref, W, h)) * HEAD_DIM ** -0.5
    v = _head_rows(p_ref, 2 * W, h)
    g = _head_rows(p_ref, 3 * W, h)
    gamma = jnp.float32(math.exp(RET_LOG_GAMMA[N_HEADS - 1]))
    for hh in range(N_HEADS - 2, -1, -1):
        gamma = jnp.where(h == hh, jnp.float32(math.exp(RET_LOG_GAMMA[hh])), gamma)
    o = _kv_state_step(s_ref, so_ref, q_scr, k_scr, None, v, gamma)
    y_ref[...] = _head_norm_cols(o, HEAD_NORM_EPS, False) * jax.nn.silu(g)


def _ret_sample_call(p_t, s_t, pos0):
    bsz = p_t.shape[1]
    inv = RET_THETA ** (-np.arange(HALF_DIM).astype(np.float64) / HALF_DIM)
    inv = jnp.asarray(inv.astype(np.float32).reshape(HALF_DIM, 1))
    return pl.pallas_call(
        functools.partial(_ret_sample_kernel, pos0),
        grid=(N_HEADS,),
        in_specs=[_full(p_t.shape), _state_block(bsz), _full(inv.shape)],
        out_specs=[_head_block(bsz), _state_block(bsz)],
        out_shape=[jax.ShapeDtypeStruct((MIX_W, bsz), F32), jax.ShapeDtypeStruct(s_t.shape, F32)],
        scratch_shapes=[pltpu.VMEM((HEAD_DIM, bsz), F32)] * 2,
        compiler_params=_params("parallel"),
        name="ret_sample",
    )(p_t, s_t, inv)


def _mlstm_sample_kernel(p_ref, pg_ref, conv_ref, c_ref, n_ref, m_ref, cw_ref, cb_ref, igb_ref, fgb_ref, ng_ref,
                         y_ref, co_ref, no_ref, mo_ref, q_scr, k_scr):
    h = pl.program_id(0)
    W = MIX_W

    def conv_rows(base):
        idx = pl.ds(pl.multiple_of(base + h * HEAD_DIM, HEAD_DIM), HEAD_DIM)
        cw = cw_ref[idx, :]
        acc = cw[:, MLSTM_CONV - 1:MLSTM_CONV] * p_ref[idx, :]
        for j in range(MLSTM_CONV - 1):
            acc = acc + cw[:, j:j + 1] * conv_ref[j, idx, :]
        return jax.nn.silu(acc + cb_ref[idx, :])

    q = conv_rows(0)
    k = conv_rows(W) * HEAD_DIM ** -0.5
    v = _head_rows(p_ref, 2 * W, h)
    og = _head_rows(p_ref, 3 * W, h)
    ig = pg_ref[pl.ds(h, 1), :] + igb_ref[pl.ds(h, 1), :]
    lf = jax.nn.log_sigmoid(pg_ref[pl.ds(N_HEADS + h, 1), :] + fgb_ref[pl.ds(h, 1), :])
    m_prev = m_ref[pl.ds(h, 1), :]
    n0 = n_ref[...]
    m_inter = lf + m_prev
    m_t = jnp.maximum(m_inter, ig)
    sc = jnp.sum(q * k, axis=0, keepdims=True) * jnp.exp(ig - m_t)
    w_inter = jnp.exp(m_inter - m_t)
    a_prev = jnp.exp(lf + m_prev - m_t)
    a_s = jnp.exp(ig + lf - lf - m_t)
    q_scr[...] = q
    k_scr[...] = k * a_s

    def body(i, qc):
        c0 = c_ref[i]
        co_ref[i] = c0 * a_prev + k_scr[pl.ds(i, 1), :] * v
        return qc + q_scr[pl.ds(i, 1), :] * c0

    qc = lax.fori_loop(0, HEAD_DIM, body, jnp.zeros_like(v))
    num = sc * v + w_inter * qc
    den = sc + w_inter * jnp.sum(q * n0, axis=0, keepdims=True)
    hh = num / jnp.maximum(jnp.abs(den), jnp.exp(-m_t))
    no_ref[...] = n0 * a_prev + k * a_s
    mo_ref[pl.ds(h, 1), :] = m_t
    y_ref[...] = jax.nn.sigmoid(og) * (_head_norm_cols(hh, HEAD_NORM_EPS, True) * ng_ref[...])


def _mlstm_sample_call(p_t, pg_t, conv_t, c_t, n_t, m_t, prm):
    bsz = p_t.shape[1]
    nblk = pl.BlockSpec((None, HEAD_DIM, bsz), lambda h: (h, 0, 0))
    return pl.pallas_call(
        _mlstm_sample_kernel,
        grid=(N_HEADS,),
        in_specs=[_full(p_t.shape), _full(pg_t.shape), _full(conv_t.shape), _state_block(bsz), nblk,
                  _full(m_t.shape), _full((2 * MIX_W, MLSTM_CONV)), _full((2 * MIX_W, 1)),
                  _full((N_HEADS, 1)), _full((N_HEADS, 1)), _head_block(1)],
        out_specs=[_head_block(bsz), _state_block(bsz), nblk, _full(m_t.shape)],
        out_shape=[jax.ShapeDtypeStruct((MIX_W, bsz), F32), jax.ShapeDtypeStruct(c_t.shape, F32),
                   jax.ShapeDtypeStruct(n_t.shape, F32), jax.ShapeDtypeStruct(m_t.shape, F32)],
        scratch_shapes=[pltpu.VMEM((HEAD_DIM, bsz), F32)] * 2,
        compiler_params=_params("arbitrary"),
        name="mlstm_sample",
    )(p_t, pg_t, conv_t, c_t, n_t, m_t, prm["cw_t"], prm["cb_c"], prm["igb_c"], prm["fgb_c"], prm["ng_c"])


def _merge_kernel(x_ref, xn_ref, ya_ref, yb_ref, yc_ref, yd_ref, wg_ref, wb_ref, wo_ref, g2_ref, x1_ref, xn2_ref):
    gates = jax.nn.sigmoid(_dot(xn_ref[...], wg_ref[...]))
    merged = None
    for m, y_ref in enumerate((ya_ref, yb_ref, yc_ref, yd_ref)):
        br = _dot(y_ref[...].astype(BF16), wb_ref[m])
        term = gates[:, m * D_MODEL:(m + 1) * D_MODEL] * br
        merged = term if merged is None else merged + term
    x1 = x_ref[...] + _dot(merged.astype(BF16), wo_ref[...])
    x1_ref[...] = x1
    xn2 = x1 * lax.rsqrt(jnp.mean(x1 * x1, axis=-1, keepdims=True) + NORM_EPS) * g2_ref[...]
    xn2_ref[...] = xn2.T.astype(BF16)


def _merge_call(x2d, xn, ys, w_gate, w_branch, w_out, g2, tm):
    n = x2d.shape[0]
    row = lambda i: (i, 0)
    return pl.pallas_call(
        _merge_kernel,
        grid=(n // tm,),
        in_specs=[pl.BlockSpec((tm, D_MODEL), row)] * 2 + [pl.BlockSpec((tm, MIX_W), row)] * 4
        + [_full(w_gate.shape), _full(w_branch.shape), _full(w_out.shape), _full((1, D_MODEL))],
        out_specs=[pl.BlockSpec((tm, D_MODEL), row), pl.BlockSpec((D_MODEL, tm), lambda i: (0, i))],
        out_shape=[jax.ShapeDtypeStruct((n, D_MODEL), F32), jax.ShapeDtypeStruct((D_MODEL, n), BF16)],
        compiler_params=_params("parallel"),
        name="merge_out",
    )(x2d, xn, *ys, w_gate, w_branch, w_out, g2.reshape(1, D_MODEL))


def _top_ranked(s, count, break_ties):
    idx = _iota(s.shape, 0)
    vals = []
    for r in range(count):
        mx = jnp.max(s, axis=0, keepdims=True)
        sel = s == mx
        if break_ties:
            sel = idx == jnp.min(jnp.where(sel, idx, s.shape[0]), axis=0, keepdims=True)
        s = jnp.where(sel, -(2.0 ** (RANK_EXP0 + r)), s)
        vals.append(mx)
    taken = s <= -(2.0 ** RANK_EXP0)
    exponent = (pltpu.bitcast(s, jnp.int32) >> 23) & 0xFF
    rank = jnp.where(taken, (exponent - (127 + RANK_EXP0)).astype(F32), float(count))
    ranked = jnp.sum(taken.astype(F32), axis=0, keepdims=True)
    return vals, rank, ranked


_LATTICE_WIDTHS = tuple(PEER_TOPK // (i + 1) for i in range(PEER_TOPK))


def _peer_head_stats(s1, s2, break_ties):
    v1, rank1, n1 = _top_ranked(s1, PEER_TOPK, break_ties)
    v2, rank2, n2 = _top_ranked(s2, PEER_TOPK, break_ties)
    rows = [v1[i] + v2[j] for i in range(PEER_TOPK) for j in range(_LATTICE_WIDTHS[i])]
    pad = (-len(rows)) % 8
    cand = jnp.concatenate(rows + [jnp.full((pad, s1.shape[1]), NEG_BIG, F32)], axis=0)
    _, crank, _ = _top_ranked(cand, PEER_TOPK, True)
    chosen = crank < float(PEER_TOPK)
    z = jnp.sum(jnp.where(chosen, jnp.exp(cand - rows[0]), 0.0), axis=0, keepdims=True)
    taken = chosen.astype(F32)
    lim = jnp.zeros_like(s1)
    off = 0
    for i in range(PEER_TOPK):
        count_i = jnp.sum(taken[off:off + _LATTICE_WIDTHS[i], :], axis=0, keepdims=True)
        lim = jnp.where(rank1 == float(i), count_i, lim)
        off += _LATTICE_WIDTHS[i]
    tied = jnp.maximum(n1, n2) - float(PEER_TOPK)
    stats = (lim, jnp.exp(s1 - v1[0]) / z, rank2.astype(BF16), jnp.exp(s2 - v2[0]).astype(BF16))
    return stats, tied


def _peer_score_kernel(xn_ref, wq_ref, keys_ref, lim_ref, cf_ref, r2_ref, e2_ref, s_scr):
    q_t = _dot(wq_ref[...], xn_ref[...])
    out_refs = (lim_ref, cf_ref, r2_ref, e2_ref)
    tied = jnp.zeros((1, q_t.shape[1]), F32)
    q_hi = q_t.astype(BF16)
    q_lo = (q_t - q_hi.astype(F32)).astype(BF16)

    def scores(h, c):
        rows = slice((2 * h + c) * PEER_QHALF, (2 * h + c + 1) * PEER_QHALF)
        return _dot(keys_ref[h, c], jnp.concatenate([q_hi[rows], q_hi[rows], q_lo[rows]], axis=0))

    for h in range(PEER_HEADS):
        s1 = scores(h, 0)
        s2 = scores(h, 1)
        s_scr[h, 0] = s1
        s_scr[h, 1] = s2
        stats, tied_h = _peer_head_stats(s1, s2, False)
        for ref, val in zip(out_refs, stats):
            ref[h] = val
        tied = jnp.maximum(tied, tied_h)

    @pl.when(jnp.max(tied) > 0.0)
    def _():
        for h in range(PEER_HEADS):
            exact, _ = _peer_head_stats(s_scr[h, 0], s_scr[h, 1], True)
            for ref, val in zip(out_refs, exact):
                ref[h] = val


def _peer_score_call(xn2_t, wq_t, keys, tm):
    n = xn2_t.shape[1]
    blk = pl.BlockSpec((PEER_HEADS, PEER_N_KEYS, tm), lambda i: (0, 0, i))
    big = lambda dt: jax.ShapeDtypeStruct((PEER_HEADS, PEER_N_KEYS, n), dt)
    return pl.pallas_call(
        _peer_score_kernel,
        grid=(n // tm,),
        in_specs=[pl.BlockSpec((D_MODEL, tm), lambda i: (0, i)), _full(wq_t.shape), _full(keys.shape)],
        out_specs=[blk] * 4,
        out_shape=[big(F32), big(F32), big(BF16), big(BF16)],
        scratch_shapes=[pltpu.VMEM((PEER_HEADS, 2, PEER_N_KEYS, tm), F32)],
        compiler_params=_params("parallel"),
        name="peer_scores",
    )(xn2_t, wq_t, keys)


def _peer_expert_kernel(xn_ref, x1_ref, u_ref, vt_ref, lim_ref, cf_ref, r2_ref, e2_ref, *rest):
    gain_ref = rest[0] if len(rest) == 3 else None
    o_ref, acc_ref = rest[-2:]
    j = pl.program_id(1)

    @pl.when(j == 0)
    def _():
        acc_ref[...] = jnp.zeros_like(acc_ref)

    te = u_ref.shape[0]
    tm = xn_ref.shape[1]
    per_tile = te // PEER_N_KEYS
    pieces = []
    for al in range(per_tile):
        a = j * per_tile + al
        hid = _dot(u_ref[al * PEER_N_KEYS:(al + 1) * PEER_N_KEYS, :], xn_ref[...])
        act = hid * (1.0 + lax.erf(hid * INV_SQRT2))
        grouped = (PEER_N_KEYS // BF16_ROWS, BF16_ROWS, tm)
        gate = jnp.zeros(grouped, BF16)
        for h in range(PEER_HEADS):
            lim = jnp.broadcast_to(lim_ref[h, pl.ds(a, 1), :], (BF16_ROWS, tm)).astype(BF16)
            cf = jnp.broadcast_to(0.5 * cf_ref[h, pl.ds(a, 1), :], (BF16_ROWS, tm)).astype(BF16)
            r2 = r2_ref[h].reshape(grouped)
            e2 = e2_ref[h].reshape(grouped)
            gate = gate + jnp.where(r2 < lim[None], e2, jnp.zeros((), BF16)) * cf[None]
        pieces.append((act * gate.reshape(PEER_N_KEYS, tm).astype(F32)).astype(BF16))
    acc_ref[...] += _dot(vt_ref[...], jnp.concatenate(pieces, axis=0))

    @pl.when(j == pl.num_programs(1) - 1)
    def _():
        x = x1_ref[...] + acc_ref[...].T
        if gain_ref is not None:
            x = x * lax.rsqrt(jnp.mean(x * x, axis=-1, keepdims=True) + NORM_EPS) * gain_ref[...]
        o_ref[...] = x


def _peer_expert_call(xn2_t, x1, u, v_t, stats, tm, final_gain=None):
    n = xn2_t.shape[1]
    n_tiles, _, te = v_t.shape
    tok = pl.BlockSpec((tm, D_MODEL), lambda i, j: (i, 0))
    blk = pl.BlockSpec((PEER_HEADS, PEER_N_KEYS, tm), lambda i, j: (0, 0, i))
    gain = [] if final_gain is None else [final_gain.reshape(1, D_MODEL)]
    return pl.pallas_call(
        _peer_expert_kernel,
        grid=(n // tm, n_tiles),
        in_specs=[pl.BlockSpec((D_MODEL, tm), lambda i, j: (0, i)), tok,
                  pl.BlockSpec((te, D_MODEL), lambda i, j: (j, 0)),
                  pl.BlockSpec((None, D_MODEL, te), lambda i, j: (j, 0, 0)),
                  blk, blk, blk, blk] + [_full((1, D_MODEL))] * len(gain),
        out_specs=tok,
        out_shape=jax.ShapeDtypeStruct((n, D_MODEL), F32),
        scratch_shapes=[pltpu.VMEM((D_MODEL, tm), F32)],
        compiler_params=_params("parallel", "arbitrary"),
        name="peer_experts",
    )(xn2_t, x1, u, v_t, *stats, *gain)


def _hi_lo_hi(w):
    hi = w.astype(BF16)
    lo = (w - hi.astype(F32)).astype(BF16)
    return jnp.concatenate([hi, lo, hi], axis=-1)


def _layer_weights(l, w_in, rw, hg, ml, w_branch, w_out, ln1_g, ln2_g, peer_w_q, peer_keys, peer_u, peer_v):
    o3 = RWKV_COLS + 2 * 4 * MIX_W
    o4 = o3 + MLSTM_COLS
    gcol = o3 + 3 * MIX_W
    w = w_in[l]
    w_main = jnp.concatenate([w[:, 0:gcol], w[:, gcol + 2 * N_HEADS:o4]], axis=1).astype(BF16)
    w_mlg = jnp.pad(w[:, gcol:gcol + 2 * N_HEADS], ((0, 0), (0, LANES - 2 * N_HEADS)))
    w_mlg = _hi_lo_hi(w_mlg.T).T
    row = lambda a: a[l].reshape(1, -1)
    col = lambda a: a[l].reshape(-1, 1)
    rwkv = dict(mu=row(rw["mu"]), w0=row(rw["w0"]), wup=rw["w_up"][l], a0=row(rw["a0"]), aup=rw["a_up"][l],
                gup=rw["g_up"][l], kk=row(rw["k_k"]), ka=row(rw["k_a"]), rk=row(rw["r_k"]),
                lng=row(rw["ln_g"]), lnb=row(rw["ln_b"]),
                mu_c=col(rw["mu"]), w0_c=col(rw["w0"]), wup_t=rw["w_up"][l].T, a0_c=col(rw["a0"]),
                aup_t=rw["a_up"][l].T, gup_t=rw["g_up"][l].T, kk_c=col(rw["k_k"]), ka_c=col(rw["k_a"]),
                rk_c=col(rw["r_k"]), lng_c=col(rw["ln_g"]), lnb_c=col(rw["ln_b"]))
    gate_b = jnp.pad(jnp.concatenate([ml["ig_b"][l], ml["fg_b"][l]]), (0, LANES - 2 * N_HEADS)).reshape(1, LANES)
    mlstm = dict(cw=ml["conv_w"][l], cb=row(ml["conv_b"]), gate_b=gate_b, ng=row(ml["norm_g"]),
                 cw_t=ml["conv_w"][l].T, cb_c=col(ml["conv_b"]), igb_c=col(ml["ig_b"]), fgb_c=col(ml["fg_b"]),
                 ng_c=col(ml["norm_g"]))
    return dict(
        ln1=ln1_g[l], ln2=ln2_g[l], w_main=w_main, w_mlg=w_mlg, w_gate=w[:, o4:].astype(BF16),
        w_branch=w_branch[l].astype(BF16), w_out=w_out[l].astype(BF16),
        rwkv=rwkv, hgrn_logits=hg["lb_logits"], hgrn_logits_c=hg["lb_logits"].T,
        hgrn_ng=row(hg["norm_g"]), hgrn_ng_c=col(hg["norm_g"]), mlstm=mlstm,
        wq_t=peer_w_q[l].T.astype(BF16), keys=_hi_lo_hi(peer_keys[l]), u=peer_u[l].astype(BF16),
        v_t=jnp.swapaxes(peer_v[l].astype(BF16).reshape(-1, PEER_EXPERT_TILE, D_MODEL), 1, 2))


def _lanes_last(a):
    return jnp.moveaxis(a, 0, -1)


def _prompt_mixers(proj, bsz, lw, layer):
    _, pr, ph, pt, pm, pg = proj
    ya, shift, s_a = _rwkv_prompt_call(pr, bsz, lw["rwkv"])
    yb, s_b = _hgrn_prompt_call(ph, bsz, lw["hgrn_logits"], lw["hgrn_ng"], layer)
    yc, s_c = _ret_prompt_call(pt, bsz, 0)
    m = lw["mlstm"]
    yd, conv, c_d, n_d, m_d = _mlstm_prompt_call(pm, pg, bsz, m["cw"], m["cb"], m["gate_b"], m["ng"])
    states = (shift[:, 0, :], s_a, s_b, s_c, conv[:, 8 - (MLSTM_CONV - 1):, :], c_d, n_d, m_d[:, 0, :N_HEADS])
    return (ya, yb, yc, yd), states


def _sample_mixers(proj, st, lw, layer, pos0):
    _, pr, ph, pt, pm, pg = proj
    shift, s_a, s_b, s_c, conv, c_d, n_d, m_d = st
    to_t = lambda a: jnp.moveaxis(a, 0, -1)
    from_t = lambda a: jnp.moveaxis(a, -1, 0)
    ya, s_a2 = _rwkv_sample_call(pr.T, shift.T, to_t(s_a), lw["rwkv"])
    yb, s_b2 = _hgrn_sample_call(ph.T, to_t(s_b), lw["hgrn_logits_c"], lw["hgrn_ng_c"], layer)
    yc, s_c2 = _ret_sample_call(pt.T, to_t(s_c), pos0)
    yd, c_d2, n_d2, m_d2 = _mlstm_sample_call(pm.T, pg.T, to_t(conv), to_t(c_d), to_t(n_d), m_d.T, lw["mlstm"])
    conv2 = jnp.concatenate([conv[:, 1:, :], pm[:, None, 0:2 * MIX_W]], axis=1)
    states = (pr, from_t(s_a2), from_t(s_b2), from_t(s_c2), conv2, from_t(c_d2), from_t(n_d2), m_d2.T)
    return (ya.T, yb.T, yc.T, yd.T), states


def _token_tiles(n):
    pick = lambda want: want if n % want == 0 else n
    return pick(256), pick(256), pick(512)


def _run_trunk(x, states, pos0, weights, lnf_g, prompt):
    bsz, t, _ = x.shape
    n = bsz * t
    x2d = x.reshape(n, D_MODEL)
    tm, tm_score, tm_peer = _token_tiles(n)
    new_states = []
    for layer, lw in enumerate(weights):
        proj = _proj_call(x2d, lw["ln1"], lw["w_main"], lw["w_mlg"], tm)
        if prompt:
            ys, st_new = _prompt_mixers(proj, bsz, lw, layer)
        else:
            ys, st_new = _sample_mixers(proj, tuple(s[layer] for s in states), lw, layer, pos0)
        new_states.append(st_new)
        x1, xn2 = _merge_call(x2d, proj[0], ys, lw["w_gate"], lw["w_branch"], lw["w_out"], lw["ln2"], tm)
        stats = _peer_score_call(xn2, lw["wq_t"], lw["keys"], tm_score)
        last = layer == len(weights) - 1
        x2d = _peer_expert_call(xn2, x1, lw["u"], lw["v_t"], stats, tm_peer, lnf_g if last else None)
    y = x2d.reshape(bsz, t, D_MODEL)
    stacked = tuple(jnp.stack([s[i] for s in new_states], axis=0) for i in range(8))
    return y, stacked


def kernel(x_prompt, x_sample, state_rwkv_shift, state_rwkv_S, state_hgrn_S, state_ret_S, state_mlstm_conv, state_mlstm_C, state_mlstm_n, state_mlstm_m, ln1_g, w_in, rwkv_mu, rwkv_w0, rwkv_w_up, rwkv_a0, rwkv_a_up, rwkv_g_up, rwkv_k_k, rwkv_k_a, rwkv_r_k, rwkv_ln_g, rwkv_ln_b, hgrn_lb_logits, hgrn_norm_g, mlstm_conv_w, mlstm_conv_b, mlstm_ig_b, mlstm_fg_b, mlstm_norm_g, w_branch, w_out, ln2_g, peer_w_q, peer_keys, peer_u, peer_v, lnf_g):
    depth = w_in.shape[0]
    rw = dict(mu=rwkv_mu, w0=rwkv_w0, w_up=rwkv_w_up, a0=rwkv_a0, a_up=rwkv_a_up, g_up=rwkv_g_up,
              k_k=rwkv_k_k, k_a=rwkv_k_a, r_k=rwkv_r_k, ln_g=rwkv_ln_g, ln_b=rwkv_ln_b)
    hg = dict(lb_logits=hgrn_lb_logits, norm_g=hgrn_norm_g)
    ml = dict(conv_w=mlstm_conv_w, conv_b=mlstm_conv_b, ig_b=mlstm_ig_b, fg_b=mlstm_fg_b, norm_g=mlstm_norm_g)
    weights = [_layer_weights(l, w_in, rw, hg, ml, w_branch, w_out, ln1_g, ln2_g, peer_w_q, peer_keys,
                              peer_u, peer_v) for l in range(depth)]
    past_len = 16384
    y_prompt, p_states = _run_trunk(x_prompt, None, 0, weights, lnf_g, True)
    sample_init = (state_rwkv_shift, state_rwkv_S, state_hgrn_S, state_ret_S, state_mlstm_conv,
                   state_mlstm_C, state_mlstm_n, state_mlstm_m)
    y_sample, s_states = _run_trunk(x_sample, sample_init, past_len, weights, lnf_g, False)
    return (y_prompt, y_sample) + p_states + s_states
```

```python
import functools
import math

import jax
import jax.numpy as jnp
from jax import lax
from jax.experimental import pallas as pl
from jax.experimental.pallas import tpu as pltpu

F32 = jnp.float32
BF16 = jnp.bfloat16
HI = lax.Precision.HIGHEST

D_MODEL = 1024
N_HEADS = 4
HEAD_DIM = 64
HALF_DIM = HEAD_DIM // 2
MIX_W = N_HEADS * HEAD_DIM
RWKV_W_RANK = 64
RWKV_A_RANK = 64
RWKV_G_RANK = 128
RWKV_COLS = 3 * MIX_W + RWKV_W_RANK + RWKV_A_RANK + RWKV_G_RANK
MLSTM_COLS = 4 * MIX_W + 2 * N_HEADS
MLSTM_CONV = 4
CHUNK = 64
SUB = 16
RET_THETA = 10000.0
PAST_LEN = 16384
PEER_HEADS = 8
PEER_N_KEYS = 128
PEER_TOPK = 16
PEER_QHALF = 64
NORM_EPS = 1e-6
RWKV_GN_EPS = 64e-5
HEAD_NORM_EPS = 1e-5
MASK_NEG = -1e30
LB_FLOOR = 1e-30
NEG_BIG = -1e30
RANK_EXP0 = 100
INV_SQRT2 = 1.0 / math.sqrt(2.0)
RET_LOG_GAMMA = tuple(math.log(1.0 - 2.0 ** (-5.0 - h)) for h in range(N_HEADS))
LANES = 128
BF16_ROWS = 16
PEER_EXPERT_TILE = 2048
VMEM_LIMIT = 56 * 1024 * 1024


X1 = "bf16"


def _dot_dims(a, b, dims, prec):
    if prec == X1:
        return lax.dot_general(a.astype(BF16), b.astype(BF16), dims, preferred_element_type=F32)
    return lax.dot_general(a, b, dims, precision=prec, preferred_element_type=F32)


def _dot(a, b, prec=None):
    return _dot_dims(a, b, (((1,), (0,)), ((), ())), prec)


def _dot_nt(a, b, prec=None):
    return _dot_dims(a, b, (((1,), (1,)), ((), ())), prec)


def _dot_tn(a, b, prec=None):
    return _dot_dims(a, b, (((0,), (0,)), ((), ())), prec)


def _params(*sem):
    return pltpu.CompilerParams(dimension_semantics=sem, vmem_limit_bytes=VMEM_LIMIT)


def _full(shape):
    nd = len(shape)
    return pl.BlockSpec(shape, lambda *_: (0,) * nd)


def _iota(shape, axis):
    return lax.broadcasted_iota(jnp.int32, shape, axis)


def _shift_rows(cur, prev, j):
    row = _iota(cur.shape, 0)
    return jnp.where(row >= j, pltpu.roll(cur, j, 0), pltpu.roll(prev, j, 0))


def _same_head_mask(width):
    return (_iota((width, width), 0) // HEAD_DIM) == (_iota((width, width), 1) // HEAD_DIM)


def _per_head_row(values, width):
    head = _iota((1, width), 1) // HEAD_DIM
    row = jnp.full((1, width), values[-1], F32)
    for h in range(len(values) - 2, -1, -1):
        row = jnp.where(head == h, values[h], row)
    return row


def _block_diag(x, same_head):
    reps = same_head.shape[0] // x.shape[0]
    return jnp.where(same_head, jnp.concatenate([x] * reps, axis=0), 0.0)


def _head_mean_lanes(y, same_head, split=True):
    e = jnp.where(same_head, 1.0 / HEAD_DIM, 0.0).astype(BF16)
    hi = y.astype(BF16)
    if not split:
        return _dot(hi, e)
    lo = (y - hi.astype(F32)).astype(BF16)
    return _dot(jnp.concatenate([hi, lo], axis=1), jnp.concatenate([e, e], axis=0))


def _head_norm_lanes(y, same_head, eps, center):
    if center:
        y = y - _head_mean_lanes(y, same_head)
    return y * lax.rsqrt(_head_mean_lanes(y * y, same_head) + eps)


def _head_norm_cols(y, eps, center):
    if center:
        y = y - jnp.mean(y, axis=0, keepdims=True)
    return y * lax.rsqrt(jnp.mean(y * y, axis=0, keepdims=True) + eps)


def _proj_kernel(x_ref, g_ref, w_ref, wg_ref, xn_ref, pr_ref, ph_ref, pt_ref, pm_ref, pg_ref):
    x = x_ref[...]
    xn = x * lax.rsqrt(jnp.mean(x * x, axis=-1, keepdims=True) + NORM_EPS) * g_ref[...]
    xb = xn.astype(BF16)
    xn_ref[...] = xb
    p = _dot(xb, w_ref[...])
    pr_ref[...] = p[:, 0:1024]
    ph_ref[...] = p[:, 1024:2048]
    pt_ref[...] = p[:, 2048:3072]
    pm_ref[...] = p[:, 3072:4096]
    x_lo = (xn - xb.astype(F32)).astype(BF16)
    pg_ref[...] = _dot(jnp.concatenate([xb, xb, x_lo], axis=1), wg_ref[...])


def _proj_call(x2d, g, w_main, w_mlg, tm):
    n = x2d.shape[0]
    row = lambda i: (i, 0)
    out = lambda c, dt: jax.ShapeDtypeStruct((n, c), dt)
    return pl.pallas_call(
        _proj_kernel,
        grid=(n // tm,),
        in_specs=[pl.BlockSpec((tm, D_MODEL), row), _full((1, D_MODEL)),
                  _full((D_MODEL, 4096)), _full((3 * D_MODEL, LANES))],
        out_specs=[pl.BlockSpec((tm, D_MODEL), row)] * 5 + [pl.BlockSpec((tm, LANES), row)],
        out_shape=[out(D_MODEL, BF16)] + [out(D_MODEL, F32)] * 4 + [out(LANES, F32)],
        compiler_params=_params("parallel"),
        name="norm_proj",
    )(x2d, g.reshape(1, D_MODEL), w_main, w_mlg)


def _cumsum_rows(x):
    row = _iota(x.shape, 0)
    shift = 1
    while shift < x.shape[0]:
        x = x + jnp.where(row >= shift, pltpu.roll(x, shift, 0), 0.0)
        shift *= 2
    return x


def _rwkv_prompt_kernel(p_ref, mu_ref, w0_ref, wup_ref, a0_ref, aup_ref, gup_ref, kk_ref, ka_ref, rk_ref,
                        lng_ref, lnb_ref, y_ref, shift_ref, s_ref, prow_ref, sbd_ref):
    p = p_ref[...]
    L = p.shape[0]
    row = _iota(p.shape, 0)
    prev = jnp.where(row >= 1, pltpu.roll(p, 1, 0), jnp.broadcast_to(prow_ref[...], p.shape))
    last = p[L - 1:L, :]
    prow_ref[...] = last
    shift_ref[...] = last
    xs = p + mu_ref[...] * (prev - p)
    W = MIX_W
    r, k, v = xs[:, 0:W], xs[:, W:2 * W], xs[:, 2 * W:3 * W]
    o = 3 * W
    xw = xs[:, o:o + RWKV_W_RANK]
    o += RWKV_W_RANK
    xa = xs[:, o:o + RWKV_A_RANK]
    o += RWKV_A_RANK
    xg = xs[:, o:o + RWKV_G_RANK]
    w = w0_ref[...] + _dot(jnp.tanh(xw), wup_ref[...], X1)
    logw = -jnp.exp(-jax.nn.softplus(-w) - 0.5)
    a = jax.nn.sigmoid(a0_ref[...] + _dot(xa, aup_ref[...], X1))
    g = _dot(jax.nn.sigmoid(xg), gup_ref[...], X1)
    same_head = _same_head_mask(W)
    kk = k * kk_ref[...]
    kk = kk / jnp.maximum(jnp.sqrt(HEAD_DIM * _head_mean_lanes(kk * kk, same_head)), 1e-12)
    kq = k * (1.0 + (a - 1.0) * ka_ref[...])
    bb = kk * a

    cum = _cumsum_rows(logw)
    cum_l = cum[L - 1:L, :]
    e_neg = jnp.exp(-cum)
    e_tail = jnp.exp(cum_l - cum)
    lhs = jnp.concatenate([kk * jnp.exp(cum - logw), r * jnp.exp(cum)], axis=0)
    s0 = sbd_ref[...]
    yield
    gb = _dot_nt(lhs, _block_diag(bb * e_neg, same_head), X1)
    yield
    gk = _dot_nt(lhs, _block_diag(kq * e_neg, same_head), X1)
    qs = _dot_nt(lhs, s0, X1)
    t_idx = _iota((L, W), 0)
    s_idx = _iota((L, W), 1) % HEAD_DIM
    strict = t_idx > s_idx
    incl = t_idx >= s_idx
    n_k = jnp.where(strict, gk[0:L], 0.0)
    n_b = jnp.where(strict, gb[0:L], 0.0)
    m_k = jnp.where(incl, gk[L:2 * L], 0.0)
    m_b = jnp.where(incl, gb[L:2 * L], 0.0)
    tinv = (t_idx == s_idx).astype(F32) - n_b
    pw = n_b
    for _ in range(5):
        yield
        pw = _dot(pw, _block_diag(pw, same_head), X1)
        yield
        tinv = tinv + _dot(tinv, _block_diag(pw, same_head), X1)
    kv = _dot(jnp.concatenate([n_k, m_k], axis=0), _block_diag(v, same_head), X1)
    yield
    u = _dot(tinv, _block_diag(qs[0:L] + kv[0:L], same_head), X1)
    yield
    y = qs[L:2 * L] + kv[L:2 * L] - _dot(m_b, _block_diag(u, same_head), X1)
    upd = _dot_tn(jnp.concatenate([v, -u], axis=0),
                  jnp.concatenate([kq * e_tail, bb * e_tail], axis=0), X1)
    yield
    s_new = s0 * jnp.exp(cum_l) + jnp.where(same_head, upd, 0.0)
    sbd_ref[...] = s_new
    for h in range(N_HEADS):
        sl = slice(h * HEAD_DIM, (h + 1) * HEAD_DIM)
        s_ref[h] = s_new[sl, sl]
    y = _head_norm_lanes(y, same_head, RWKV_GN_EPS, True) * lng_ref[...] + lnb_ref[...]
    y = y + HEAD_DIM * _head_mean_lanes(r * kq * rk_ref[...], same_head) * v
    y_ref[...] = y * g


def _per_sequence(body, batched, carried):
    def kernel(*refs):
        @pl.when(pl.program_id(1) == 0)
        def _():
            for i in carried:
                refs[i][...] = jnp.zeros_like(refs[i])

        group = next(r.shape[0] for r, b in zip(refs, batched) if b)
        runs = [body(*[r.at[gi] if b else r for r, b in zip(refs, batched)]) for gi in range(group)]
        runs = [r for r in runs if r is not None]
        while runs:
            for r in list(runs):
                if next(r, _DONE) is _DONE:
                    runs.remove(r)

    return kernel


_DONE = object()


def _seq_group(bsz):
    return next(g for g in (8, 4, 2, 1) if bsz % g == 0)


def _chunk_specs(grp, width):
    return pl.BlockSpec((grp, CHUNK, width), lambda b, c: (b, c, 0))


def _per_seq_spec(grp, *dims):
    return pl.BlockSpec((grp,) + dims, lambda b, c: (b,) + (0,) * len(dims))


def _state_spec(grp):
    return _per_seq_spec(grp, N_HEADS, HEAD_DIM, HEAD_DIM)


def _rwkv_prompt_call(p, bsz, prm):
    n = p.shape[0]
    t = n // bsz
    grp = _seq_group(bsz)
    small = [prm[k] for k in ("mu", "w0", "wup", "a0", "aup", "gup", "kk", "ka", "rk", "lng", "lnb")]
    y, shift, s = pl.pallas_call(
        _per_sequence(_rwkv_prompt_kernel, (True,) + (False,) * len(small) + (True,) * 5,
                      carried=(len(small) + 4, len(small) + 5)),
        grid=(bsz // grp, t // CHUNK),
        in_specs=[_chunk_specs(grp, RWKV_COLS)] + [_full(a.shape) for a in small],
        out_specs=[_chunk_specs(grp, MIX_W), _per_seq_spec(grp, 1, RWKV_COLS), _state_spec(grp)],
        out_shape=[jax.ShapeDtypeStruct((bsz, t, MIX_W), F32),
                   jax.ShapeDtypeStruct((bsz, 1, RWKV_COLS), F32),
                   jax.ShapeDtypeStruct((bsz, N_HEADS, HEAD_DIM, HEAD_DIM), F32)],
        scratch_shapes=[pltpu.VMEM((grp, 1, RWKV_COLS), F32), pltpu.VMEM((grp, MIX_W, MIX_W), F32)],
        compiler_params=_params("parallel", "arbitrary"),
        name="rwkv_prompt",
    )(p.reshape(bsz, t, RWKV_COLS), *small)
    return y.reshape(n, MIX_W), shift, s


def _hgrn_lower_bound(logits, layer):
    l0, l1 = logits[0:1, :], logits[1:2, :]
    m = jnp.maximum(l0, l1)
    e0, e1 = jnp.exp(l0 - m), jnp.exp(l1 - m)
    z = e0 + e1
    lb0, lb1 = e0 / z, e1 / z
    return (lb0 - lb0) if layer == 0 else ((lb0 + lb1) - lb0)


def _hgrn_log_f(z, lb):
    return jnp.logaddexp(jnp.log(jnp.maximum(lb, LB_FLOOR)), jnp.log1p(-lb) + jax.nn.log_sigmoid(z))


def _hgrn_prompt_kernel(layer, p_ref, lg_ref, ng_ref, y_ref, s_ref, st_ref):
    p = p_ref[...]
    L = p.shape[0]
    W = MIX_W
    q = jax.nn.silu(p[:, 0:W])
    lb = _hgrn_lower_bound(lg_ref[...], layer)
    log_f = _hgrn_log_f(p[:, W:2 * W], lb)
    k = 1.0 - jnp.exp(log_f)
    v = p[:, 2 * W:3 * W]
    g = p[:, 3 * W:4 * W]
    same_head = _same_head_mask(W)
    row = _iota((L, W), 0) % SUB
    b = log_f
    shift = 1
    while shift < SUB:
        b = b + jnp.where(row >= shift, pltpu.roll(b, shift, 0), 0.0)
        shift *= 2
    qe = q * jnp.exp(b)
    intra = jnp.zeros((L, W), F32)
    for lag in range(SUB):
        valid = row >= lag
        kr = k if lag == 0 else pltpu.roll(k, lag, 0)
        br = b if lag == 0 else pltpu.roll(b, lag, 0)
        vr = v if lag == 0 else pltpu.roll(v, lag, 0)
        term = jnp.where(valid, q * kr * jnp.exp(jnp.where(valid, b - br, 0.0)), 0.0)
        intra = intra + HEAD_DIM * _head_mean_lanes(term, same_head, split=False) * vr
        if lag % 4 == 3:
            yield
    st = st_ref[...]
    pieces = []
    for j in range(L // SUB):
        rs = slice(j * SUB, (j + 1) * SUB)
        pieces.append(_dot_nt(qe[rs], st, X1))
        b_l = b[(j + 1) * SUB - 1:(j + 1) * SUB]
        kd = k[rs] * jnp.exp(b_l - b[rs])
        st = st * jnp.exp(b_l) + jnp.where(same_head, _dot_tn(v[rs], kd, X1), 0.0)
        yield
    st_ref[...] = st
    for h in range(N_HEADS):
        sl = slice(h * HEAD_DIM, (h + 1) * HEAD_DIM)
        s_ref[h] = st[sl, sl].T
    o = jnp.concatenate(pieces, axis=0) + intra
    y_ref[...] = _head_norm_lanes(o, same_head, HEAD_NORM_EPS, False) * ng_ref[...] * jax.nn.silu(g)


def _hgrn_prompt_call(p, bsz, logits, norm_g, layer):
    n = p.shape[0]
    t = n // bsz
    grp = _seq_group(bsz)
    y, s = pl.pallas_call(
        _per_sequence(functools.partial(_hgrn_prompt_kernel, layer), (True, False, False, True, True, True),
                      carried=(5,)),
        grid=(bsz // grp, t // CHUNK),
        in_specs=[_chunk_specs(grp, 4 * MIX_W), _full(logits.shape), _full(norm_g.shape)],
        out_specs=[_chunk_specs(grp, MIX_W), _state_spec(grp)],
        out_shape=[jax.ShapeDtypeStruct((bsz, t, MIX_W), F32),
                   jax.ShapeDtypeStruct((bsz, N_HEADS, HEAD_DIM, HEAD_DIM), F32)],
        scratch_shapes=[pltpu.VMEM((grp, MIX_W, MIX_W), F32)],
        compiler_params=_params("parallel", "arbitrary"),
        name="hgrn_prompt",
    )(p.reshape(bsz, t, 4 * MIX_W), logits, norm_g)
    return y.reshape(n, MIX_W), s


def _ret_prompt_kernel(pos0, p_ref, inv_ref, y_ref, s_ref, sbd_ref):
    c = pl.program_id(1)
    p = p_ref[...]
    L = p.shape[0]
    W = MIX_W
    pos = (pos0 + c * L + _iota((L, W), 0)).astype(F32)
    ang = pos * inv_ref[...]
    cosv, sinv = jnp.cos(ang), jnp.sin(ang)
    first = (_iota((L, W), 1) % HEAD_DIM) < HALF_DIM

    def rot(z):
        partner = jnp.where(first, -pltpu.roll(z, W - HALF_DIM, 1), pltpu.roll(z, HALF_DIM, 1))
        return z * cosv + partner * sinv

    q = rot(p[:, 0:W])
    k = rot(p[:, W:2 * W]) * HEAD_DIM ** -0.5
    v = p[:, 2 * W:3 * W]
    g = p[:, 3 * W:4 * W]
    same_head = _same_head_mask(W)
    lg = _per_head_row(RET_LOG_GAMMA, W)
    t_row = _iota((L, W), 0).astype(F32)
    rel = t_row - (_iota((L, W), 1) % HEAD_DIM).astype(F32)
    dmat = jnp.where(rel >= 0, jnp.exp(lg * jnp.maximum(rel, 0.0)), 0.0)
    d_in = jnp.exp(lg * (t_row + 1.0))
    d_st = jnp.exp(lg * (L - 1.0 - t_row))
    s0 = sbd_ref[...]
    yield
    att = _dot_nt(q, _block_diag(k, same_head), X1) * dmat
    inter = _dot(q, s0, X1) * d_in
    yield
    o = _dot(att, _block_diag(v, same_head), X1) + inter
    s_new = s0 * jnp.exp(lg * float(L)) + jnp.where(same_head, _dot_tn(k * d_st, v, X1), 0.0)
    yield
    sbd_ref[...] = s_new
    for h in range(N_HEADS):
        sl = slice(h * HEAD_DIM, (h + 1) * HEAD_DIM)
        s_ref[h] = s_new[sl, sl]
    y_ref[...] = _head_norm_lanes(o, same_head, HEAD_NORM_EPS, False) * jax.nn.silu(g)


def _rotary_inv():
    return RET_THETA ** (-jnp.arange(HALF_DIM, dtype=F32) / HALF_DIM)


def _rotary_inv_row():
    return jnp.tile(_rotary_inv(), MIX_W // HALF_DIM).reshape(1, MIX_W)


def _ret_prompt_call(p, bsz, pos0):
    n = p.shape[0]
    t = n // bsz
    grp = _seq_group(bsz)
    inv = _rotary_inv_row()
    y, s = pl.pallas_call(
        _per_sequence(functools.partial(_ret_prompt_kernel, pos0), (True, False, True, True, True), carried=(4,)),
        grid=(bsz // grp, t // CHUNK),
        in_specs=[_chunk_specs(grp, 4 * MIX_W), _full(inv.shape)],
        out_specs=[_chunk_specs(grp, MIX_W), _state_spec(grp)],
        out_shape=[jax.ShapeDtypeStruct((bsz, t, MIX_W), F32),
                   jax.ShapeDtypeStruct((bsz, N_HEADS, HEAD_DIM, HEAD_DIM), F32)],
        scratch_shapes=[pltpu.VMEM((grp, MIX_W, MIX_W), F32)],
        compiler_params=_params("parallel", "arbitrary"),
        name="ret_prompt",
    )(p.reshape(bsz, t, 4 * MIX_W), inv)
    return y.reshape(n, MIX_W), s


def _mlstm_prompt_kernel(p_ref, pg_ref, cw_ref, cb_ref, gb_ref, ng_ref,
                         y_ref, conv_ref, c_ref, n_ref, m_ref, prev_ref, cbd_ref, nrow_ref, mrow_ref):
    p = p_ref[...]
    L = p.shape[0]
    W = MIX_W
    raw = p[:, 0:2 * W]
    prev = prev_ref[...]
    acc = cw_ref[MLSTM_CONV - 1:MLSTM_CONV, :] * raw
    for j in range(1, MLSTM_CONV):
        acc = acc + cw_ref[MLSTM_CONV - 1 - j:MLSTM_CONV - j, :] * _shift_rows(raw, prev, j)
    prev_ref[...] = raw
    conv_ref[...] = raw[L - 8:L, :]
    qk = jax.nn.silu(acc + cb_ref[...])
    q_all = qk[:, 0:W]
    k_all = qk[:, W:2 * W] * HEAD_DIM ** -0.5
    v_all = p[:, 2 * W:3 * W]
    og = p[:, 3 * W:4 * W]

    gb = pg_ref[...] + gb_ref[...]
    bcum = _cumsum_rows(jax.nn.log_sigmoid(gb))
    gb_t = gb.T
    bcum_t = bcum.T
    spread = lambda a, base: jnp.concatenate(
        [jnp.broadcast_to(a[:, base + h:base + h + 1], (L, HEAD_DIM)) for h in range(N_HEADS)], axis=1)
    rows = lambda a_t, base: jnp.concatenate([a_t[base + h:base + h + 1, :] for h in range(N_HEADS)], axis=1)
    ig_col, b_col = spread(gb, 0), spread(bcum, N_HEADS)
    ig_row, b_row = rows(gb_t, 0), rows(bcum_t, N_HEADS)
    same_head = _same_head_mask(W)
    causal = _iota((L, W), 0) >= (_iota((L, W), 1) % HEAD_DIM)
    m_prev = mrow_ref[...]
    log_d = jnp.where(causal, b_col - b_row + ig_row, MASK_NEG)
    m_inter = b_col + m_prev
    row_max = jnp.concatenate(
        [jnp.broadcast_to(jnp.max(log_d[:, h * HEAD_DIM:(h + 1) * HEAD_DIM], axis=1, keepdims=True), (L, HEAD_DIM))
         for h in range(N_HEADS)], axis=1)
    m_t = jnp.maximum(m_inter, row_max)
    c0 = cbd_ref[...]
    n0 = nrow_ref[...]
    yield
    sc = _dot_nt(q_all, _block_diag(k_all, same_head), X1) * jnp.exp(log_d - m_t)
    w_inter = jnp.exp(m_inter - m_t)
    inter = w_inter * _dot(q_all, c0, X1)
    m_l = m_t[L - 1:L, :]
    b_l = b_col[L - 1:L, :]
    a_prev = jnp.exp(b_l + m_prev - m_l)
    ks = k_all * jnp.exp(ig_col + b_l - b_col - m_l)
    yield
    num = _dot(sc, _block_diag(v_all, same_head), X1) + inter
    c_new = c0 * a_prev + jnp.where(same_head, _dot_tn(ks, v_all, X1), 0.0)
    yield
    den = HEAD_DIM * (_head_mean_lanes(sc, same_head) + w_inter * _head_mean_lanes(q_all * n0, same_head))
    hh = num / jnp.maximum(jnp.abs(den), jnp.exp(-m_t))
    n_new = n0 * a_prev + jnp.sum(ks, axis=0, keepdims=True)
    yield
    cbd_ref[...] = c_new
    nrow_ref[...] = n_new
    mrow_ref[...] = m_l
    m_out = jnp.zeros(m_ref.shape, F32)
    m_first = jnp.logical_and(_iota(m_ref.shape, 0) == 0, _iota(m_ref.shape, 1) < N_HEADS)
    for h in range(N_HEADS):
        sl = slice(h * HEAD_DIM, (h + 1) * HEAD_DIM)
        c_ref[h] = c_new[sl, sl]
        n_ref[h:h + 1, :] = n_new[:, sl]
        m_out = jnp.where(jnp.logical_and(m_first, _iota(m_ref.shape, 1) == h), m_l[:, h * HEAD_DIM:h * HEAD_DIM + 1], m_out)
    m_ref[...] = m_out
    hn = _head_norm_lanes(hh, same_head, HEAD_NORM_EPS, True) * ng_ref[...]
    y_ref[...] = jax.nn.sigmoid(og) * hn


def _mlstm_prompt_call(p, pg, bsz, conv_w, conv_b, gate_b, norm_g):
    n = p.shape[0]
    t = n // bsz
    grp = _seq_group(bsz)
    outs = pl.pallas_call(
        _per_sequence(_mlstm_prompt_kernel, (True, True) + (False,) * 4 + (True,) * 9, carried=(11, 12, 13, 14)),
        grid=(bsz // grp, t // CHUNK),
        in_specs=[_chunk_specs(grp, 4 * MIX_W), _chunk_specs(grp, LANES), _full(conv_w.shape),
                  _full(conv_b.shape), _full(gate_b.shape), _full(norm_g.shape)],
        out_specs=[_chunk_specs(grp, MIX_W),
                   _per_seq_spec(grp, 8, 2 * MIX_W),
                   _state_spec(grp),
                   _per_seq_spec(grp, N_HEADS, HEAD_DIM),
                   _per_seq_spec(grp, 8, LANES)],
        out_shape=[jax.ShapeDtypeStruct((bsz, t, MIX_W), F32),
                   jax.ShapeDtypeStruct((bsz, 8, 2 * MIX_W), F32),
                   jax.ShapeDtypeStruct((bsz, N_HEADS, HEAD_DIM, HEAD_DIM), F32),
                   jax.ShapeDtypeStruct((bsz, N_HEADS, HEAD_DIM), F32),
                   jax.ShapeDtypeStruct((bsz, 8, LANES), F32)],
        scratch_shapes=[pltpu.VMEM((grp, CHUNK, 2 * MIX_W), F32), pltpu.VMEM((grp, MIX_W, MIX_W), F32),
                        pltpu.VMEM((grp, 1, MIX_W), F32), pltpu.VMEM((grp, 1, MIX_W), F32)],
        compiler_params=_params("parallel", "arbitrary"),
        name="mlstm_prompt",
    )(p.reshape(bsz, t, 4 * MIX_W), pg.reshape(bsz, t, LANES), conv_w, conv_b, gate_b, norm_g)
    return (outs[0].reshape(n, MIX_W),) + tuple(outs[1:])


def _head_rows(ref, base, h):
    return ref[pl.ds(pl.multiple_of(base + h * HEAD_DIM, HEAD_DIM), HEAD_DIM), :]


def _rwkv_sample_kernel(p_ref, sh_ref, s_ref, mu_ref, w0_ref, wup_ref, a0_ref, aup_ref, gup_ref, kk_ref, ka_ref,
                        rk_ref, lng_ref, lnb_ref, y_ref, so_ref, v_scr, y_scr):
    h = pl.program_id(0)
    W = MIX_W

    def xs_rows(lo, size, dyn):
        idx = pl.ds(pl.multiple_of(lo + h * HEAD_DIM, HEAD_DIM), size) if dyn else pl.ds(lo, size)
        pv = p_ref[idx, :]
        return pv + mu_ref[idx, :] * (sh_ref[idx, :] - pv)

    r = xs_rows(0, HEAD_DIM, True)
    k = xs_rows(W, HEAD_DIM, True)
    v = xs_rows(2 * W, HEAD_DIM, True)
    xw = xs_rows(3 * W, RWKV_W_RANK, False)
    xa = xs_rows(3 * W + RWKV_W_RANK, RWKV_A_RANK, False)
    xg = xs_rows(3 * W + RWKV_W_RANK + RWKV_A_RANK, RWKV_G_RANK, False)
    w = w0_ref[...] + _dot(wup_ref[...], jnp.tanh(xw), HI)
    wd = jnp.exp(-jnp.exp(-jax.nn.softplus(-w) - 0.5))
    a = jax.nn.sigmoid(a0_ref[...] + _dot(aup_ref[...], xa, HI))
    g = _dot(gup_ref[...], jax.nn.sigmoid(xg), HI)
    kk = k * kk_ref[...]
    kk = kk / jnp.maximum(jnp.sqrt(jnp.sum(kk * kk, axis=0, keepdims=True)), 1e-12)
    kq = k * (1.0 + (a - 1.0) * ka_ref[...])
    bb = kk * a
    v_scr[...] = v

    def body(i, carry):
        sv = s_ref[i]
        sa = -jnp.sum(sv * kk, axis=0, keepdims=True)
        snew = sv * wd + sa * bb + v_scr[pl.ds(i, 1), :] * kq
        so_ref[i] = snew
        y_scr[pl.ds(i, 1), :] = jnp.sum(snew * r, axis=0, keepdims=True)
        return carry

    lax.fori_loop(0, HEAD_DIM, body, 0)
    y = _head_norm_cols(y_scr[...], RWKV_GN_EPS, True) * lng_ref[...] + lnb_ref[...]
    y = y + jnp.sum(r * kq * rk_ref[...], axis=0, keepdims=True) * v
    y_ref[...] = y * g


def _head_block(cols):
    return pl.BlockSpec((HEAD_DIM, cols), lambda h: (h, 0))


def _state_block(bsz):
    return pl.BlockSpec((None, HEAD_DIM, HEAD_DIM, bsz), lambda h: (h, 0, 0, 0))


def _rwkv_sample_call(p_t, shift_t, s_t, prm):
    bsz = p_t.shape[1]
    return pl.pallas_call(
        _rwkv_sample_kernel,
        grid=(N_HEADS,),
        in_specs=[_full(p_t.shape), _full(shift_t.shape), _state_block(bsz), _full((RWKV_COLS, 1)),
                  _head_block(1), _head_block(RWKV_W_RANK), _head_block(1), _head_block(RWKV_A_RANK),
                  _head_block(RWKV_G_RANK), _head_block(1), _head_block(1), _head_block(1), _head_block(1),
                  _head_block(1)],
        out_specs=[_head_block(bsz), _state_block(bsz)],
        out_shape=[jax.ShapeDtypeStruct((MIX_W, bsz), F32), jax.ShapeDtypeStruct(s_t.shape, F32)],
        scratch_shapes=[pltpu.VMEM((HEAD_DIM, bsz), F32), pltpu.VMEM((HEAD_DIM, bsz), F32)],
        compiler_params=_params("parallel"),
        name="rwkv_sample",
    )(p_t, shift_t, s_t, prm["mu_c"], prm["w0_c"], prm["wup_t"], prm["a0_c"], prm["aup_t"], prm["gup_t"],
      prm["kk_c"], prm["ka_c"], prm["rk_c"], prm["lng_c"], prm["lnb_c"])


def _kv_state_step(s_ref, so_ref, q_scr, k_scr, f_scr, v, decay):
    def body(i, o):
        dec = f_scr[pl.ds(i, 1), :] if decay is None else decay
        snew = s_ref[i] * dec + k_scr[pl.ds(i, 1), :] * v
        so_ref[i] = snew
        return o + q_scr[pl.ds(i, 1), :] * snew

    return lax.fori_loop(0, HEAD_DIM, body, jnp.zeros_like(v))


def _hgrn_sample_kernel(layer, p_ref, s_ref, lg_ref, ng_ref, y_ref, so_ref, q_scr, k_scr, f_scr):
    h = pl.program_id(0)
    W = MIX_W
    lg = lg_ref[...]
    l0, l1 = lg[:, 0:1], lg[:, 1:2]
    m = jnp.maximum(l0, l1)
    e0, e1 = jnp.exp(l0 - m), jnp.exp(l1 - m)
    z = e0 + e1
    lb0, lb1 = e0 / z, e1 / z
    lb = (lb0 - lb0) if layer == 0 else ((lb0 + lb1) - lb0)
    log_f = _hgrn_log_f(_head_rows(p_ref, W, h), lb)
    f = jnp.exp(log_f)
    q_scr[...] = jax.nn.silu(_head_rows(p_ref, 0, h))
    k_scr[...] = 1.0 - f
    f_scr[...] = f
    v = _head_rows(p_ref, 2 * W, h)
    g = _head_rows(p_ref, 3 * W, h)
    o = _kv_state_step(s_ref, so_ref, q_scr, k_scr, f_scr, v, None)
    y_ref[...] = _head_norm_cols(o, HEAD_NORM_EPS, False) * ng_ref[...] * jax.nn.silu(g)


def _hgrn_sample_call(p_t, s_t, logits_c, norm_g_c, layer):
    bsz = p_t.shape[1]
    return pl.pallas_call(
        functools.partial(_hgrn_sample_kernel, layer),
        grid=(N_HEADS,),
        in_specs=[_full(p_t.shape), _state_block(bsz), _head_block(2), _head_block(1)],
        out_specs=[_head_block(bsz), _state_block(bsz)],
        out_shape=[jax.ShapeDtypeStruct((MIX_W, bsz), F32), jax.ShapeDtypeStruct(s_t.shape, F32)],
        scratch_shapes=[pltpu.VMEM((HEAD_DIM, bsz), F32)] * 3,
        compiler_params=_params("parallel"),
        name="hgrn_sample",
    )(p_t, s_t, logits_c, norm_g_c)


def _ret_sample_kernel(pos0, p_ref, s_ref, inv_ref, y_ref, so_ref, q_scr, k_scr):
    h = pl.program_id(0)
    W = MIX_W
    ang = jnp.float32(pos0) * inv_ref[...]
    cosv, sinv = jnp.cos(ang), jnp.sin(ang)

    def rot(z):
        z1, z2 = z[0:HALF_DIM, :], z[HALF_DIM:HEAD_DIM, :]
        return jnp.concatenate([z1 * cosv - z2 * sinv, z1 * sinv + z2 * cosv], axis=0)

    q_scr[...] = rot(_head_rows(p_ref, 0, h))
    k_scr[...] = rot(_head_rows(p_ref, W, h)) * HEAD_DIM ** -0.5
    v = _head_rows(p_ref, 2 * W, h)
    g = _head_rows(p_ref, 3 * W, h)
    gamma = jnp.float32(math.exp(RET_LOG_GAMMA[N_HEADS - 1]))
    for hh in range(N_HEADS - 2, -1, -1):
        gamma = jnp.where(h == hh, jnp.float32(math.exp(RET_LOG_GAMMA[hh])), gamma)
    o = _kv_state_step(s_ref, so_ref, q_scr, k_scr, None, v, gamma)
    y_ref[...] = _head_norm_cols(o, HEAD_NORM_EPS, False) * jax.nn.silu(g)


def _ret_sample_call(p_t, s_t, pos0):
    bsz = p_t.shape[1]
    inv = _rotary_inv().reshape(HALF_DIM, 1)
    return pl.pallas_call(
        functools.partial(_ret_sample_kernel, pos0),
        grid=(N_HEADS,),
        in_specs=[_full(p_t.shape), _state_block(bsz), _full(inv.shape)],
        out_specs=[_head_block(bsz), _state_block(bsz)],
        out_shape=[jax.ShapeDtypeStruct((MIX_W, bsz), F32), jax.ShapeDtypeStruct(s_t.shape, F32)],
        scratch_shapes=[pltpu.VMEM((HEAD_DIM, bsz), F32)] * 2,
        compiler_params=_params("parallel"),
        name="ret_sample",
    )(p_t, s_t, inv)


def _mlstm_sample_kernel(p_ref, pg_ref, conv_ref, c_ref, n_ref, m_ref, cw_ref, cb_ref, igb_ref, fgb_ref, ng_ref,
                         y_ref, co_ref, no_ref, mo_ref, q_scr, k_scr):
    h = pl.program_id(0)
    W = MIX_W

    def conv_rows(base):
        idx = pl.ds(pl.multiple_of(base + h * HEAD_DIM, HEAD_DIM), HEAD_DIM)
        cw = cw_ref[idx, :]
        acc = cw[:, MLSTM_CONV - 1:MLSTM_CONV] * p_ref[idx, :]
        for j in range(MLSTM_CONV - 1):
            acc = acc + cw[:, j:j + 1] * conv_ref[j, idx, :]
        return jax.nn.silu(acc + cb_ref[idx, :])

    q = conv_rows(0)
    k = conv_rows(W) * HEAD_DIM ** -0.5
    v = _head_rows(p_ref, 2 * W, h)
    og = _head_rows(p_ref, 3 * W, h)
    ig = pg_ref[pl.ds(h, 1), :] + igb_ref[pl.ds(h, 1), :]
    lf = jax.nn.log_sigmoid(pg_ref[pl.ds(N_HEADS + h, 1), :] + fgb_ref[pl.ds(h, 1), :])
    m_prev = m_ref[pl.ds(h, 1), :]
    n0 = n_ref[...]
    m_inter = lf + m_prev
    m_t = jnp.maximum(m_inter, ig)
    sc = jnp.sum(q * k, axis=0, keepdims=True) * jnp.exp(ig - m_t)
    w_inter = jnp.exp(m_inter - m_t)
    a_prev = jnp.exp(lf + m_prev - m_t)
    a_s = jnp.exp(ig + lf - lf - m_t)
    q_scr[...] = q
    k_scr[...] = k * a_s

    def body(i, qc):
        c0 = c_ref[i]
        co_ref[i] = c0 * a_prev + k_scr[pl.ds(i, 1), :] * v
        return qc + q_scr[pl.ds(i, 1), :] * c0

    qc = lax.fori_loop(0, HEAD_DIM, body, jnp.zeros_like(v))
    num = sc * v + w_inter * qc
    den = sc + w_inter * jnp.sum(q * n0, axis=0, keepdims=True)
    hh = num / jnp.maximum(jnp.abs(den), jnp.exp(-m_t))
    no_ref[...] = n0 * a_prev + k * a_s
    mo_ref[pl.ds(h, 1), :] = m_t
    y_ref[...] = jax.nn.sigmoid(og) * (_head_norm_cols(hh, HEAD_NORM_EPS, True) * ng_ref[...])


def _mlstm_sample_call(p_t, pg_t, conv_t, c_t, n_t, m_t, prm):
    bsz = p_t.shape[1]
    nblk = pl.BlockSpec((None, HEAD_DIM, bsz), lambda h: (h, 0, 0))
    return pl.pallas_call(
        _mlstm_sample_kernel,
        grid=(N_HEADS,),
        in_specs=[_full(p_t.shape), _full(pg_t.shape), _full(conv_t.shape), _state_block(bsz), nblk,
                  _full(m_t.shape), _full((2 * MIX_W, MLSTM_CONV)), _full((2 * MIX_W, 1)),
                  _full((N_HEADS, 1)), _full((N_HEADS, 1)), _head_block(1)],
        out_specs=[_head_block(bsz), _state_block(bsz), nblk, _full(m_t.shape)],
        out_shape=[jax.ShapeDtypeStruct((MIX_W, bsz), F32), jax.ShapeDtypeStruct(c_t.shape, F32),
                   jax.ShapeDtypeStruct(n_t.shape, F32), jax.ShapeDtypeStruct(m_t.shape, F32)],
        scratch_shapes=[pltpu.VMEM((HEAD_DIM, bsz), F32)] * 2,
        compiler_params=_params("arbitrary"),
        name="mlstm_sample",
    )(p_t, pg_t, conv_t, c_t, n_t, m_t, prm["cw_t"], prm["cb_c"], prm["igb_c"], prm["fgb_c"], prm["ng_c"])


def _merge_kernel(x_ref, xn_ref, ya_ref, yb_ref, yc_ref, yd_ref, wg_ref, wb_ref, wo_ref, g2_ref, x1_ref, xn2_ref):
    gates = jax.nn.sigmoid(_dot(xn_ref[...], wg_ref[...]))
    merged = None
    for m, y_ref in enumerate((ya_ref, yb_ref, yc_ref, yd_ref)):
        br = _dot(y_ref[...].astype(BF16), wb_ref[m])
        term = gates[:, m * D_MODEL:(m + 1) * D_MODEL] * br
        merged = term if merged is None else merged + term
    x1 = x_ref[...] + _dot(merged.astype(BF16), wo_ref[...])
    x1_ref[...] = x1
    xn2 = x1 * lax.rsqrt(jnp.mean(x1 * x1, axis=-1, keepdims=True) + NORM_EPS) * g2_ref[...]
    xn2_ref[...] = xn2.T.astype(BF16)


def _merge_call(x2d, xn, ys, w_gate, w_branch, w_out, g2, tm):
    n = x2d.shape[0]
    row = lambda i: (i, 0)
    return pl.pallas_call(
        _merge_kernel,
        grid=(n // tm,),
        in_specs=[pl.BlockSpec((tm, D_MODEL), row)] * 2 + [pl.BlockSpec((tm, MIX_W), row)] * 4
        + [_full(w_gate.shape), _full(w_branch.shape), _full(w_out.shape), _full((1, D_MODEL))],
        out_specs=[pl.BlockSpec((tm, D_MODEL), row), pl.BlockSpec((D_MODEL, tm), lambda i: (0, i))],
        out_shape=[jax.ShapeDtypeStruct((n, D_MODEL), F32), jax.ShapeDtypeStruct((D_MODEL, n), BF16)],
        compiler_params=_params("parallel"),
        name="merge_out",
    )(x2d, xn, *ys, w_gate, w_branch, w_out, g2.reshape(1, D_MODEL))


def _top_ranked(s, count, break_ties):
    idx = _iota(s.shape, 0)
    vals = []
    for r in range(count):
        mx = jnp.max(s, axis=0, keepdims=True)
        sel = s == mx
        if break_ties:
            sel = idx == jnp.min(jnp.where(sel, idx, s.shape[0]), axis=0, keepdims=True)
        s = jnp.where(sel, -(2.0 ** (RANK_EXP0 + r)), s)
        vals.append(mx)
    taken = s <= -(2.0 ** RANK_EXP0)
    exponent = (pltpu.bitcast(s, jnp.int32) >> 23) & 0xFF
    rank = jnp.where(taken, (exponent - (127 + RANK_EXP0)).astype(F32), float(count))
    ranked = jnp.sum(taken.astype(F32), axis=0, keepdims=True)
    return vals, rank, ranked


_LATTICE_WIDTHS = tuple(PEER_TOPK // (i + 1) for i in range(PEER_TOPK))


def _peer_head_stats(s1, s2, break_ties):
    v1, rank1, n1 = _top_ranked(s1, PEER_TOPK, break_ties)
    v2, rank2, n2 = _top_ranked(s2, PEER_TOPK, break_ties)
    rows = [v1[i] + v2[j] for i in range(PEER_TOPK) for j in range(_LATTICE_WIDTHS[i])]
    pad = (-len(rows)) % 8
    cand = jnp.concatenate(rows + [jnp.full((pad, s1.shape[1]), NEG_BIG, F32)], axis=0)
    _, crank, _ = _top_ranked(cand, PEER_TOPK, True)
    chosen = crank < float(PEER_TOPK)
    z = jnp.sum(jnp.where(chosen, jnp.exp(cand - rows[0]), 0.0), axis=0, keepdims=True)
    taken = chosen.astype(F32)
    lim = jnp.zeros_like(s1)
    off = 0
    for i in range(PEER_TOPK):
        count_i = jnp.sum(taken[off:off + _LATTICE_WIDTHS[i], :], axis=0, keepdims=True)
        lim = jnp.where(rank1 == float(i), count_i, lim)
        off += _LATTICE_WIDTHS[i]
    tied = jnp.maximum(n1, n2) - float(PEER_TOPK)
    stats = (lim, jnp.exp(s1 - v1[0]) / z, rank2.astype(BF16), jnp.exp(s2 - v2[0]).astype(BF16))
    return stats, tied


def _peer_score_kernel(xn_ref, wq_ref, keys_ref, lim_ref, cf_ref, r2_ref, e2_ref, s_scr):
    q_t = _dot(wq_ref[...], xn_ref[...])
    out_refs = (lim_ref, cf_ref, r2_ref, e2_ref)
    tied = jnp.zeros((1, q_t.shape[1]), F32)
    q_hi = q_t.astype(BF16)
    q_lo = (q_t - q_hi.astype(F32)).astype(BF16)

    def scores(h, c):
        rows = slice((2 * h + c) * PEER_QHALF, (2 * h + c + 1) * PEER_QHALF)
        return _dot(keys_ref[h, c], jnp.concatenate([q_hi[rows], q_hi[rows], q_lo[rows]], axis=0))

    for h in range(PEER_HEADS):
        s1 = scores(h, 0)
        s2 = scores(h, 1)
        s_scr[h, 0] = s1
        s_scr[h, 1] = s2
        stats, tied_h = _peer_head_stats(s1, s2, False)
        for ref, val in zip(out_refs, stats):
            ref[h] = val
        tied = jnp.maximum(tied, tied_h)

    @pl.when(jnp.max(tied) > 0.0)
    def _():
        for h in range(PEER_HEADS):
            exact, _ = _peer_head_stats(s_scr[h, 0], s_scr[h, 1], True)
            for ref, val in zip(out_refs, exact):
                ref[h] = val


def _peer_score_call(xn2_t, wq_t, keys, tm):
    n = xn2_t.shape[1]
    blk = pl.BlockSpec((PEER_HEADS, PEER_N_KEYS, tm), lambda i: (0, 0, i))
    big = lambda dt: jax.ShapeDtypeStruct((PEER_HEADS, PEER_N_KEYS, n), dt)
    return pl.pallas_call(
        _peer_score_kernel,
        grid=(n // tm,),
        in_specs=[pl.BlockSpec((D_MODEL, tm), lambda i: (0, i)), _full(wq_t.shape), _full(keys.shape)],
        out_specs=[blk] * 4,
        out_shape=[big(F32), big(F32), big(BF16), big(BF16)],
        scratch_shapes=[pltpu.VMEM((PEER_HEADS, 2, PEER_N_KEYS, tm), F32)],
        compiler_params=_params("parallel"),
        name="peer_scores",
    )(xn2_t, wq_t, keys)


def _peer_expert_kernel(xn_ref, x1_ref, u_ref, vt_ref, lim_ref, cf_ref, r2_ref, e2_ref, *rest):
    gain_ref = rest[0] if len(rest) == 3 else None
    o_ref, acc_ref = rest[-2:]
    j = pl.program_id(1)

    @pl.when(j == 0)
    def _():
        acc_ref[...] = jnp.zeros_like(acc_ref)

    te = u_ref.shape[0]
    tm = xn_ref.shape[1]
    per_tile = te // PEER_N_KEYS
    pieces = []
    for al in range(per_tile):
        a = j * per_tile + al
        hid = _dot(u_ref[al * PEER_N_KEYS:(al + 1) * PEER_N_KEYS, :], xn_ref[...])
        act = hid * (1.0 + lax.erf(hid * INV_SQRT2))
        grouped = (PEER_N_KEYS // BF16_ROWS, BF16_ROWS, tm)
        gate = jnp.zeros(grouped, BF16)
        for h in range(PEER_HEADS):
            lim = jnp.broadcast_to(lim_ref[h, pl.ds(a, 1), :], (BF16_ROWS, tm)).astype(BF16)
            cf = jnp.broadcast_to(0.5 * cf_ref[h, pl.ds(a, 1), :], (BF16_ROWS, tm)).astype(BF16)
            r2 = r2_ref[h].reshape(grouped)
            e2 = e2_ref[h].reshape(grouped)
            gate = gate + jnp.where(r2 < lim[None], e2, jnp.zeros((), BF16)) * cf[None]
        pieces.append(act.astype(BF16) * gate.reshape(PEER_N_KEYS, tm))
    acc_ref[...] += _dot(vt_ref[...], jnp.concatenate(pieces, axis=0))

    @pl.when(j == pl.num_programs(1) - 1)
    def _():
        x = x1_ref[...] + acc_ref[...].T
        if gain_ref is not None:
            x = x * lax.rsqrt(jnp.mean(x * x, axis=-1, keepdims=True) + NORM_EPS) * gain_ref[...]
        o_ref[...] = x


def _peer_expert_call(xn2_t, x1, u, v_t, stats, tm, final_gain=None):
    n = xn2_t.shape[1]
    n_tiles, _, te = v_t.shape
    tok = pl.BlockSpec((tm, D_MODEL), lambda i, j: (i, 0))
    blk = pl.BlockSpec((PEER_HEADS, PEER_N_KEYS, tm), lambda i, j: (0, 0, i))
    gain = [] if final_gain is None else [final_gain.reshape(1, D_MODEL)]
    return pl.pallas_call(
        _peer_expert_kernel,
        grid=(n // tm, n_tiles),
        in_specs=[pl.BlockSpec((D_MODEL, tm), lambda i, j: (0, i)), tok,
                  pl.BlockSpec((te, D_MODEL), lambda i, j: (j, 0)),
                  pl.BlockSpec((None, D_MODEL, te), lambda i, j: (j, 0, 0)),
                  blk, blk, blk, blk] + [_full((1, D_MODEL))] * len(gain),
        out_specs=tok,
        out_shape=jax.ShapeDtypeStruct((n, D_MODEL), F32),
        scratch_shapes=[pltpu.VMEM((D_MODEL, tm), F32)],
        compiler_params=_params("parallel", "arbitrary"),
        name="peer_experts",
    )(xn2_t, x1, u, v_t, *stats, *gain)


def _hi_lo_hi(w):
    hi = w.astype(BF16)
    lo = (w - hi.astype(F32)).astype(BF16)
    return jnp.concatenate([hi, lo, hi], axis=-1)


def _layer_weights(l, w_in, rw, hg, ml, w_branch, w_out, ln1_g, ln2_g, peer_w_q, peer_keys, peer_u, peer_v):
    o3 = RWKV_COLS + 2 * 4 * MIX_W
    o4 = o3 + MLSTM_COLS
    gcol = o3 + 3 * MIX_W
    w = w_in[l]
    w_main = jnp.concatenate([w[:, 0:gcol], w[:, gcol + 2 * N_HEADS:o4]], axis=1).astype(BF16)
    w_mlg = jnp.pad(w[:, gcol:gcol + 2 * N_HEADS], ((0, 0), (0, LANES - 2 * N_HEADS)))
    w_mlg = _hi_lo_hi(w_mlg.T).T
    row = lambda a: a[l].reshape(1, -1)
    col = lambda a: a[l].reshape(-1, 1)
    rwkv = dict(mu=row(rw["mu"]), w0=row(rw["w0"]), wup=rw["w_up"][l], a0=row(rw["a0"]), aup=rw["a_up"][l],
                gup=rw["g_up"][l], kk=row(rw["k_k"]), ka=row(rw["k_a"]), rk=row(rw["r_k"]),
                lng=row(rw["ln_g"]), lnb=row(rw["ln_b"]),
                mu_c=col(rw["mu"]), w0_c=col(rw["w0"]), wup_t=rw["w_up"][l].T, a0_c=col(rw["a0"]),
                aup_t=rw["a_up"][l].T, gup_t=rw["g_up"][l].T, kk_c=col(rw["k_k"]), ka_c=col(rw["k_a"]),
                rk_c=col(rw["r_k"]), lng_c=col(rw["ln_g"]), lnb_c=col(rw["ln_b"]))
    gate_b = jnp.pad(jnp.concatenate([ml["ig_b"][l], ml["fg_b"][l]]), (0, LANES - 2 * N_HEADS)).reshape(1, LANES)
    mlstm = dict(cw=ml["conv_w"][l], cb=row(ml["conv_b"]), gate_b=gate_b, ng=row(ml["norm_g"]),
                 cw_t=ml["conv_w"][l].T, cb_c=col(ml["conv_b"]), igb_c=col(ml["ig_b"]), fgb_c=col(ml["fg_b"]),
                 ng_c=col(ml["norm_g"]))
    return dict(
        ln1=ln1_g[l], ln2=ln2_g[l], w_main=w_main, w_mlg=w_mlg, w_gate=w[:, o4:].astype(BF16),
        w_branch=w_branch[l].astype(BF16), w_out=w_out[l].astype(BF16),
        rwkv=rwkv, hgrn_logits=hg["lb_logits"], hgrn_logits_c=hg["lb_logits"].T,
        hgrn_ng=row(hg["norm_g"]), hgrn_ng_c=col(hg["norm_g"]), mlstm=mlstm,
        wq_t=peer_w_q[l].T.astype(BF16), keys=_hi_lo_hi(peer_keys[l]), u=peer_u[l].astype(BF16),
        v_t=jnp.swapaxes(peer_v[l].astype(BF16).reshape(-1, PEER_EXPERT_TILE, D_MODEL), 1, 2))


def _prompt_mixers(proj, bsz, lw, layer):
    _, pr, ph, pt, pm, pg = proj
    ya, shift, s_a = _rwkv_prompt_call(pr, bsz, lw["rwkv"])
    yb, s_b = _hgrn_prompt_call(ph, bsz, lw["hgrn_logits"], lw["hgrn_ng"], layer)
    yc, s_c = _ret_prompt_call(pt, bsz, 0)
    m = lw["mlstm"]
    yd, conv, c_d, n_d, m_d = _mlstm_prompt_call(pm, pg, bsz, m["cw"], m["cb"], m["gate_b"], m["ng"])
    states = (shift[:, 0, :], s_a, s_b, s_c, conv[:, 8 - (MLSTM_CONV - 1):, :], c_d, n_d, m_d[:, 0, :N_HEADS])
    return (ya, yb, yc, yd), states


def _sample_mixers(proj, st, lw, layer, pos0):
    _, pr, ph, pt, pm, pg = proj
    shift, s_a, s_b, s_c, conv, c_d, n_d, m_d = st
    to_t = lambda a: jnp.moveaxis(a, 0, -1)
    from_t = lambda a: jnp.moveaxis(a, -1, 0)
    ya, s_a2 = _rwkv_sample_call(pr.T, shift.T, to_t(s_a), lw["rwkv"])
    yb, s_b2 = _hgrn_sample_call(ph.T, to_t(s_b), lw["hgrn_logits_c"], lw["hgrn_ng_c"], layer)
    yc, s_c2 = _ret_sample_call(pt.T, to_t(s_c), pos0)
    yd, c_d2, n_d2, m_d2 = _mlstm_sample_call(pm.T, pg.T, to_t(conv), to_t(c_d), to_t(n_d), m_d.T, lw["mlstm"])
    conv2 = jnp.concatenate([conv[:, 1:, :], pm[:, None, 0:2 * MIX_W]], axis=1)
    states = (pr, from_t(s_a2), from_t(s_b2), from_t(s_c2), conv2, from_t(c_d2), from_t(n_d2), m_d2.T)
    return (ya.T, yb.T, yc.T, yd.T), states


def _token_tiles(n):
    pick = lambda want: want if n % want == 0 else n
    return pick(256), pick(256), pick(512)


def _run_trunk(x, states, pos0, weights, lnf_g, prompt):
    bsz, t, _ = x.shape
    n = bsz * t
    x2d = x.reshape(n, D_MODEL)
    tm, tm_score, tm_peer = _token_tiles(n)
    new_states = []
    for layer, lw in enumerate(weights):
        proj = _proj_call(x2d, lw["ln1"], lw["w_main"], lw["w_mlg"], tm)
        if prompt:
            ys, st_new = _prompt_mixers(proj, bsz, lw, layer)
        else:
            ys, st_new = _sample_mixers(proj, tuple(s[layer] for s in states), lw, layer, pos0)
        new_states.append(st_new)
        x1, xn2 = _merge_call(x2d, proj[0], ys, lw["w_gate"], lw["w_branch"], lw["w_out"], lw["ln2"], tm)
        stats = _peer_score_call(xn2, lw["wq_t"], lw["keys"], tm_score)
        last = layer == len(weights) - 1
        x2d = _peer_expert_call(xn2, x1, lw["u"], lw["v_t"], stats, tm_peer, lnf_g if last else None)
    y = x2d.reshape(bsz, t, D_MODEL)
    stacked = tuple(jnp.stack([s[i] for s in new_states], axis=0) for i in range(8))
    return y, stacked


def kernel(x_prompt, x_sample, state_rwkv_shift, state_rwkv_S, state_hgrn_S, state_ret_S, state_mlstm_conv, state_mlstm_C, state_mlstm_n, state_mlstm_m, ln1_g, w_in, rwkv_mu, rwkv_w0, rwkv_w_up, rwkv_a0, rwkv_a_up, rwkv_g_up, rwkv_k_k, rwkv_k_a, rwkv_r_k, rwkv_ln_g, rwkv_ln_b, hgrn_lb_logits, hgrn_norm_g, mlstm_conv_w, mlstm_conv_b, mlstm_ig_b, mlstm_fg_b, mlstm_norm_g, w_branch, w_out, ln2_g, peer_w_q, peer_keys, peer_u, peer_v, lnf_g):
    depth = w_in.shape[0]
    rw = dict(mu=rwkv_mu, w0=rwkv_w0, w_up=rwkv_w_up, a0=rwkv_a0, a_up=rwkv_a_up, g_up=rwkv_g_up,
              k_k=rwkv_k_k, k_a=rwkv_k_a, r_k=rwkv_r_k, ln_g=rwkv_ln_g, ln_b=rwkv_ln_b)
    hg = dict(lb_logits=hgrn_lb_logits, norm_g=hgrn_norm_g)
    ml = dict(conv_w=mlstm_conv_w, conv_b=mlstm_conv_b, ig_b=mlstm_ig_b, fg_b=mlstm_fg_b, norm_g=mlstm_norm_g)
    weights = [_layer_weights(l, w_in, rw, hg, ml, w_branch, w_out, ln1_g, ln2_g, peer_w_q, peer_keys,
                              peer_u, peer_v) for l in range(depth)]
    y_prompt, p_states = _run_trunk(x_prompt, None, 0, weights, lnf_g, True)
    sample_init = (state_rwkv_shift, state_rwkv_S, state_hgrn_S, state_ret_S, state_mlstm_conv,
                   state_mlstm_C, state_mlstm_n, state_mlstm_m)
    y_sample, s_states = _run_trunk(x_sample, sample_init, PAST_LEN, weights, lnf_g, False)
    return (y_prompt, y_sample) + p_states + s_states
```

```python
import functools
import math

import jax
import jax.numpy as jnp
from jax import lax
from jax.experimental import pallas as pl
from jax.experimental.pallas import tpu as pltpu

F32 = jnp.float32
BF16 = jnp.bfloat16
HI = lax.Precision.HIGHEST

D_MODEL = 1024
N_HEADS = 4
HEAD_DIM = 64
HALF_DIM = HEAD_DIM // 2
MIX_W = N_HEADS * HEAD_DIM
RWKV_W_RANK = 64
RWKV_A_RANK = 64
RWKV_G_RANK = 128
RWKV_COLS = 3 * MIX_W + RWKV_W_RANK + RWKV_A_RANK + RWKV_G_RANK
MLSTM_COLS = 4 * MIX_W + 2 * N_HEADS
MLSTM_CONV = 4
CHUNK = 64
SUB = 16
RET_THETA = 10000.0
PAST_LEN = 16384
PEER_HEADS = 8
PEER_N_KEYS = 128
PEER_TOPK = 16
PEER_QHALF = 64
NORM_EPS = 1e-6
RWKV_GN_EPS = 64e-5
HEAD_NORM_EPS = 1e-5
MASK_NEG = -1e30
LB_FLOOR = 1e-30
NEG_BIG = -1e30
RANK_EXP0 = 100
INV_SQRT2 = 1.0 / math.sqrt(2.0)
RET_LOG_GAMMA = tuple(math.log(1.0 - 2.0 ** (-5.0 - h)) for h in range(N_HEADS))
LANES = 128
BF16_ROWS = 16
PEER_EXPERT_TILE = 2048
VMEM_LIMIT = 56 * 1024 * 1024


X1 = "bf16"


def _dot_dims(a, b, dims, prec):
    if prec == X1:
        return lax.dot_general(a.astype(BF16), b.astype(BF16), dims, preferred_element_type=F32)
    return lax.dot_general(a, b, dims, precision=prec, preferred_element_type=F32)


def _dot(a, b, prec=None):
    return _dot_dims(a, b, (((1,), (0,)), ((), ())), prec)


def _dot_nt(a, b, prec=None):
    return _dot_dims(a, b, (((1,), (1,)), ((), ())), prec)


def _dot_tn(a, b, prec=None):
    return _dot_dims(a, b, (((0,), (0,)), ((), ())), prec)


def _params(*sem):
    return pltpu.CompilerParams(dimension_semantics=sem, vmem_limit_bytes=VMEM_LIMIT)


def _full(shape):
    nd = len(shape)
    return pl.BlockSpec(shape, lambda *_: (0,) * nd)


def _iota(shape, axis):
    return lax.broadcasted_iota(jnp.int32, shape, axis)


def _shift_rows(cur, prev, j):
    row = _iota(cur.shape, 0)
    return jnp.where(row >= j, pltpu.roll(cur, j, 0), pltpu.roll(prev, j, 0))


def _same_head_mask(width):
    return (_iota((width, width), 0) // HEAD_DIM) == (_iota((width, width), 1) // HEAD_DIM)


def _per_head_row(values, width):
    head = _iota((1, width), 1) // HEAD_DIM
    row = jnp.full((1, width), values[-1], F32)
    for h in range(len(values) - 2, -1, -1):
        row = jnp.where(head == h, values[h], row)
    return row


def _block_diag(x, same_head):
    reps = same_head.shape[0] // x.shape[0]
    return jnp.where(same_head, jnp.concatenate([x] * reps, axis=0), 0.0)


def _head_mean_lanes(y, same_head, split=True):
    e = jnp.where(same_head, 1.0 / HEAD_DIM, 0.0).astype(BF16)
    hi = y.astype(BF16)
    if not split:
        return _dot(hi, e)
    lo = (y - hi.astype(F32)).astype(BF16)
    return _dot(jnp.concatenate([hi, lo], axis=1), jnp.concatenate([e, e], axis=0))


def _head_norm_lanes(y, same_head, eps, center):
    if center:
        y = y - _head_mean_lanes(y, same_head)
    return y * lax.rsqrt(_head_mean_lanes(y * y, same_head) + eps)


def _head_norm_cols(y, eps, center):
    if center:
        y = y - jnp.mean(y, axis=0, keepdims=True)
    return y * lax.rsqrt(jnp.mean(y * y, axis=0, keepdims=True) + eps)


def _proj_kernel(x_ref, g_ref, w_ref, wg_ref, xn_ref, pr_ref, ph_ref, pt_ref, pm_ref, pg_ref):
    x = x_ref[...]
    xn = x * lax.rsqrt(jnp.mean(x * x, axis=-1, keepdims=True) + NORM_EPS) * g_ref[...]
    xb = xn.astype(BF16)
    xn_ref[...] = xb
    p = _dot(xb, w_ref[...])
    pr_ref[...] = p[:, 0:1024]
    ph_ref[...] = p[:, 1024:2048]
    pt_ref[...] = p[:, 2048:3072]
    pm_ref[...] = p[:, 3072:4096]
    x_lo = (xn - xb.astype(F32)).astype(BF16)
    pg_ref[...] = _dot(jnp.concatenate([xb, xb, x_lo], axis=1), wg_ref[...])


def _proj_call(x2d, g, w_main, w_mlg, tm):
    n = x2d.shape[0]
    row = lambda i: (i, 0)
    out = lambda c, dt: jax.ShapeDtypeStruct((n, c), dt)
    return pl.pallas_call(
        _proj_kernel,
        grid=(n // tm,),
        in_specs=[pl.BlockSpec((tm, D_MODEL), row), _full((1, D_MODEL)),
                  _full((D_MODEL, 4096)), _full((3 * D_MODEL, LANES))],
        out_specs=[pl.BlockSpec((tm, D_MODEL), row)] * 5 + [pl.BlockSpec((tm, LANES), row)],
        out_shape=[out(D_MODEL, BF16)] + [out(D_MODEL, F32)] * 4 + [out(LANES, F32)],
        compiler_params=_params("parallel"),
        name="norm_proj",
    )(x2d, g.reshape(1, D_MODEL), w_main, w_mlg)


def _cumsum_rows(x):
    row = _iota(x.shape, 0)
    shift = 1
    while shift < x.shape[0]:
        x = x + jnp.where(row >= shift, pltpu.roll(x, shift, 0), 0.0)
        shift *= 2
    return x


def _rwkv_prompt_kernel(p_ref, mu_ref, w0_ref, wup_ref, a0_ref, aup_ref, gup_ref, kk_ref, ka_ref, rk_ref,
                        lng_ref, lnb_ref, y_ref, shift_ref, s_ref, prow_ref, sbd_ref):
    p = p_ref[...]
    L = p.shape[0]
    row = _iota(p.shape, 0)
    prev = jnp.where(row >= 1, pltpu.roll(p, 1, 0), jnp.broadcast_to(prow_ref[...], p.shape))
    last = p[L - 1:L, :]
    prow_ref[...] = last
    shift_ref[...] = last
    xs = p + mu_ref[...] * (prev - p)
    W = MIX_W
    r, k, v = xs[:, 0:W], xs[:, W:2 * W], xs[:, 2 * W:3 * W]
    o = 3 * W
    xw = xs[:, o:o + RWKV_W_RANK]
    o += RWKV_W_RANK
    xa = xs[:, o:o + RWKV_A_RANK]
    o += RWKV_A_RANK
    xg = xs[:, o:o + RWKV_G_RANK]
    w = w0_ref[...] + _dot(jnp.tanh(xw), wup_ref[...], X1)
    logw = -jnp.exp(-jax.nn.softplus(-w) - 0.5)
    a = jax.nn.sigmoid(a0_ref[...] + _dot(xa, aup_ref[...], X1))
    g = _dot(jax.nn.sigmoid(xg), gup_ref[...], X1)
    same_head = _same_head_mask(W)
    kk = k * kk_ref[...]
    kk = kk / jnp.maximum(jnp.sqrt(HEAD_DIM * _head_mean_lanes(kk * kk, same_head)), 1e-12)
    kq = k * (1.0 + (a - 1.0) * ka_ref[...])
    bb = kk * a

    cum = _cumsum_rows(logw)
    cum_l = cum[L - 1:L, :]
    e_neg = jnp.exp(-cum)
    e_tail = jnp.exp(cum_l - cum)
    lhs = jnp.concatenate([kk * jnp.exp(cum - logw), r * jnp.exp(cum)], axis=0)
    s0 = sbd_ref[...]
    yield
    gb = _dot_nt(lhs, _block_diag(bb * e_neg, same_head), X1)
    yield
    gk = _dot_nt(lhs, _block_diag(kq * e_neg, same_head), X1)
    qs = _dot_nt(lhs, s0, X1)
    t_idx = _iota((L, W), 0)
    s_idx = _iota((L, W), 1) % HEAD_DIM
    strict = t_idx > s_idx
    incl = t_idx >= s_idx
    n_k = jnp.where(strict, gk[0:L], 0.0)
    n_b = jnp.where(strict, gb[0:L], 0.0)
    m_k = jnp.where(incl, gk[L:2 * L], 0.0)
    m_b = jnp.where(incl, gb[L:2 * L], 0.0)
    tinv = (t_idx == s_idx).astype(F32) - n_b
    pw = n_b
    for _ in range(5):
        yield
        pw = _dot(pw, _block_diag(pw, same_head), X1)
        yield
        tinv = tinv + _dot(tinv, _block_diag(pw, same_head), X1)
    kv = _dot(jnp.concatenate([n_k, m_k], axis=0), _block_diag(v, same_head), X1)
    yield
    u = _dot(tinv, _block_diag(qs[0:L] + kv[0:L], same_head), X1)
    yield
    y = qs[L:2 * L] + kv[L:2 * L] - _dot(m_b, _block_diag(u, same_head), X1)
    upd = _dot_tn(jnp.concatenate([v, -u], axis=0),
                  jnp.concatenate([kq * e_tail, bb * e_tail], axis=0), X1)
    yield
    s_new = s0 * jnp.exp(cum_l) + jnp.where(same_head, upd, 0.0)
    sbd_ref[...] = s_new
    for h in range(N_HEADS):
        sl = slice(h * HEAD_DIM, (h + 1) * HEAD_DIM)
        s_ref[h] = s_new[sl, sl]
    y = _head_norm_lanes(y, same_head, RWKV_GN_EPS, True) * lng_ref[...] + lnb_ref[...]
    y = y + HEAD_DIM * _head_mean_lanes(r * kq * rk_ref[...], same_head) * v
    y_ref[...] = y * g


def _per_sequence(body, batched, carried):
    def kernel(*refs):
        @pl.when(pl.program_id(1) == 0)
        def _():
            for i in carried:
                refs[i][...] = jnp.zeros_like(refs[i])

        group = next(r.shape[0] for r, b in zip(refs, batched) if b)
        runs = [body(*[r.at[gi] if b else r for r, b in zip(refs, batched)]) for gi in range(group)]
        runs = [r for r in runs if r is not None]
        while runs:
            for r in list(runs):
                if next(r, _DONE) is _DONE:
                    runs.remove(r)

    return kernel


_DONE = object()


def _seq_group(bsz):
    return next(g for g in (8, 4, 2, 1) if bsz % g == 0)


def _chunk_specs(grp, width):
    return pl.BlockSpec((grp, CHUNK, width), lambda b, c: (b, c, 0))


def _per_seq_spec(grp, *dims):
    return pl.BlockSpec((grp,) + dims, lambda b, c: (b,) + (0,) * len(dims))


def _state_spec(grp):
    return _per_seq_spec(grp, N_HEADS, HEAD_DIM, HEAD_DIM)


def _rwkv_prompt_call(p, bsz, prm):
    n = p.shape[0]
    t = n // bsz
    grp = _seq_group(bsz)
    small = [prm[k] for k in ("mu", "w0", "wup", "a0", "aup", "gup", "kk", "ka", "rk", "lng", "lnb")]
    y, shift, s = pl.pallas_call(
        _per_sequence(_rwkv_prompt_kernel, (True,) + (False,) * len(small) + (True,) * 5,
                      carried=(len(small) + 4, len(small) + 5)),
        grid=(bsz // grp, t // CHUNK),
        in_specs=[_chunk_specs(grp, RWKV_COLS)] + [_full(a.shape) for a in small],
        out_specs=[_chunk_specs(grp, MIX_W), _per_seq_spec(grp, 1, RWKV_COLS), _state_spec(grp)],
        out_shape=[jax.ShapeDtypeStruct((bsz, t, MIX_W), F32),
                   jax.ShapeDtypeStruct((bsz, 1, RWKV_COLS), F32),
                   jax.ShapeDtypeStruct((bsz, N_HEADS, HEAD_DIM, HEAD_DIM), F32)],
        scratch_shapes=[pltpu.VMEM((grp, 1, RWKV_COLS), F32), pltpu.VMEM((grp, MIX_W, MIX_W), F32)],
        compiler_params=_params("parallel", "arbitrary"),
        name="rwkv_prompt",
    )(p.reshape(bsz, t, RWKV_COLS), *small)
    return y.reshape(n, MIX_W), shift, s


def _hgrn_lower_bound(logits, layer):
    l0, l1 = logits[0:1, :], logits[1:2, :]
    m = jnp.maximum(l0, l1)
    e0, e1 = jnp.exp(l0 - m), jnp.exp(l1 - m)
    z = e0 + e1
    lb0, lb1 = e0 / z, e1 / z
    return (lb0 - lb0) if layer == 0 else ((lb0 + lb1) - lb0)


def _hgrn_log_f(z, lb):
    return jnp.logaddexp(jnp.log(jnp.maximum(lb, LB_FLOOR)), jnp.log1p(-lb) + jax.nn.log_sigmoid(z))


def _hgrn_prompt_kernel(layer, p_ref, lg_ref, ng_ref, y_ref, s_ref, st_ref):
    p = p_ref[...]
    L = p.shape[0]
    W = MIX_W
    q = jax.nn.silu(p[:, 0:W])
    lb = _hgrn_lower_bound(lg_ref[...], layer)
    log_f = _hgrn_log_f(p[:, W:2 * W], lb)
    k = 1.0 - jnp.exp(log_f)
    v = p[:, 2 * W:3 * W]
    g = p[:, 3 * W:4 * W]
    same_head = _same_head_mask(W)
    row = _iota((L, W), 0) % SUB
    b = log_f
    shift = 1
    while shift < SUB:
        b = b + jnp.where(row >= shift, pltpu.roll(b, shift, 0), 0.0)
        shift *= 2
    qe = q * jnp.exp(b)
    intra = jnp.zeros((L, W), F32)
    for lag in range(SUB):
        valid = row >= lag
        kr = k if lag == 0 else pltpu.roll(k, lag, 0)
        br = b if lag == 0 else pltpu.roll(b, lag, 0)
        vr = v if lag == 0 else pltpu.roll(v, lag, 0)
        term = jnp.where(valid, q * kr * jnp.exp(jnp.where(valid, b - br, 0.0)), 0.0)
        intra = intra + HEAD_DIM * _head_mean_lanes(term, same_head, split=False) * vr
        if lag % 4 == 3:
            yield
    st = st_ref[...]
    pieces = []
    for j in range(L // SUB):
        rs = slice(j * SUB, (j + 1) * SUB)
        pieces.append(_dot_nt(qe[rs], st, X1))
        b_l = b[(j + 1) * SUB - 1:(j + 1) * SUB]
        kd = k[rs] * jnp.exp(b_l - b[rs])
        st = st * jnp.exp(b_l) + jnp.where(same_head, _dot_tn(v[rs], kd, X1), 0.0)
        yield
    st_ref[...] = st
    for h in range(N_HEADS):
        sl = slice(h * HEAD_DIM, (h + 1) * HEAD_DIM)
        s_ref[h] = st[sl, sl].T
    o = jnp.concatenate(pieces, axis=0) + intra
    y_ref[...] = _head_norm_lanes(o, same_head, HEAD_NORM_EPS, False) * ng_ref[...] * jax.nn.silu(g)


def _hgrn_prompt_call(p, bsz, logits, norm_g, layer):
    n = p.shape[0]
    t = n // bsz
    grp = _seq_group(bsz)
    y, s = pl.pallas_call(
        _per_sequence(functools.partial(_hgrn_prompt_kernel, layer), (True, False, False, True, True, True),
                      carried=(5,)),
        grid=(bsz // grp, t // CHUNK),
        in_specs=[_chunk_specs(grp, 4 * MIX_W), _full(logits.shape), _full(norm_g.shape)],
        out_specs=[_chunk_specs(grp, MIX_W), _state_spec(grp)],
        out_shape=[jax.ShapeDtypeStruct((bsz, t, MIX_W), F32),
                   jax.ShapeDtypeStruct((bsz, N_HEADS, HEAD_DIM, HEAD_DIM), F32)],
        scratch_shapes=[pltpu.VMEM((grp, MIX_W, MIX_W), F32)],
        compiler_params=_params("parallel", "arbitrary"),
        name="hgrn_prompt",
    )(p.reshape(bsz, t, 4 * MIX_W), logits, norm_g)
    return y.reshape(n, MIX_W), s


def _ret_prompt_kernel(pos0, p_ref, inv_ref, y_ref, s_ref, sbd_ref):
    c = pl.program_id(1)
    p = p_ref[...]
    L = p.shape[0]
    W = MIX_W
    pos = (pos0 + c * L + _iota((L, W), 0)).astype(F32)
    ang = pos * inv_ref[...]
    cosv, sinv = jnp.cos(ang), jnp.sin(ang)
    first = (_iota((L, W), 1) % HEAD_DIM) < HALF_DIM

    def rot(z):
        partner = jnp.where(first, -pltpu.roll(z, W - HALF_DIM, 1), pltpu.roll(z, HALF_DIM, 1))
        return z * cosv + partner * sinv

    q = rot(p[:, 0:W])
    k = rot(p[:, W:2 * W]) * HEAD_DIM ** -0.5
    v = p[:, 2 * W:3 * W]
    g = p[:, 3 * W:4 * W]
    same_head = _same_head_mask(W)
    lg = _per_head_row(RET_LOG_GAMMA, W)
    t_row = _iota((L, W), 0).astype(F32)
    rel = t_row - (_iota((L, W), 1) % HEAD_DIM).astype(F32)
    dmat = jnp.where(rel >= 0, jnp.exp(lg * jnp.maximum(rel, 0.0)), 0.0)
    d_in = jnp.exp(lg * (t_row + 1.0))
    d_st = jnp.exp(lg * (L - 1.0 - t_row))
    s0 = sbd_ref[...]
    yield
    att = _dot_nt(q, _block_diag(k, same_head), X1) * dmat
    inter = _dot(q, s0, X1) * d_in
    yield
    o = _dot(att, _block_diag(v, same_head), X1) + inter
    s_new = s0 * jnp.exp(lg * float(L)) + jnp.where(same_head, _dot_tn(k * d_st, v, X1), 0.0)
    yield
    sbd_ref[...] = s_new
    for h in range(N_HEADS):
        sl = slice(h * HEAD_DIM, (h + 1) * HEAD_DIM)
        s_ref[h] = s_new[sl, sl]
    y_ref[...] = _head_norm_lanes(o, same_head, HEAD_NORM_EPS, False) * jax.nn.silu(g)


def _rotary_inv():
    return RET_THETA ** (-jnp.arange(HALF_DIM, dtype=F32) / HALF_DIM)


def _rotary_inv_row():
    return jnp.tile(_rotary_inv(), MIX_W // HALF_DIM).reshape(1, MIX_W)


def _ret_prompt_call(p, bsz, pos0):
    n = p.shape[0]
    t = n // bsz
    grp = _seq_group(bsz)
    inv = _rotary_inv_row()
    y, s = pl.pallas_call(
        _per_sequence(functools.partial(_ret_prompt_kernel, pos0), (True, False, True, True, True), carried=(4,)),
        grid=(bsz // grp, t // CHUNK),
        in_specs=[_chunk_specs(grp, 4 * MIX_W), _full(inv.shape)],
        out_specs=[_chunk_specs(grp, MIX_W), _state_spec(grp)],
        out_shape=[jax.ShapeDtypeStruct((bsz, t, MIX_W), F32),
                   jax.ShapeDtypeStruct((bsz, N_HEADS, HEAD_DIM, HEAD_DIM), F32)],
        scratch_shapes=[pltpu.VMEM((grp, MIX_W, MIX_W), F32)],
        compiler_params=_params("parallel", "arbitrary"),
        name="ret_prompt",
    )(p.reshape(bsz, t, 4 * MIX_W), inv)
    return y.reshape(n, MIX_W), s


def _mlstm_prompt_kernel(p_ref, pg_ref, cw_ref, cb_ref, gb_ref, ng_ref,
                         y_ref, conv_ref, c_ref, n_ref, m_ref, prev_ref, cbd_ref, nrow_ref, mrow_ref):
    p = p_ref[...]
    L = p.shape[0]
    W = MIX_W
    raw = p[:, 0:2 * W]
    prev = prev_ref[...]
    acc = cw_ref[MLSTM_CONV - 1:MLSTM_CONV, :] * raw
    for j in range(1, MLSTM_CONV):
        acc = acc + cw_ref[MLSTM_CONV - 1 - j:MLSTM_CONV - j, :] * _shift_rows(raw, prev, j)
    prev_ref[...] = raw
    conv_ref[...] = raw[L - 8:L, :]
    qk = jax.nn.silu(acc + cb_ref[...])
    q_all = qk[:, 0:W]
    k_all = qk[:, W:2 * W] * HEAD_DIM ** -0.5
    v_all = p[:, 2 * W:3 * W]
    og = p[:, 3 * W:4 * W]

    gb = pg_ref[...] + gb_ref[...]
    bcum = _cumsum_rows(jax.nn.log_sigmoid(gb))
    gb_t = gb.T
    bcum_t = bcum.T
    spread = lambda a, base: jnp.concatenate(
        [jnp.broadcast_to(a[:, base + h:base + h + 1], (L, HEAD_DIM)) for h in range(N_HEADS)], axis=1)
    rows = lambda a_t, base: jnp.concatenate([a_t[base + h:base + h + 1, :] for h in range(N_HEADS)], axis=1)
    ig_col, b_col = spread(gb, 0), spread(bcum, N_HEADS)
    ig_row, b_row = rows(gb_t, 0), rows(bcum_t, N_HEADS)
    same_head = _same_head_mask(W)
    causal = _iota((L, W), 0) >= (_iota((L, W), 1) % HEAD_DIM)
    m_prev = mrow_ref[...]
    log_d = jnp.where(causal, b_col - b_row + ig_row, MASK_NEG)
    m_inter = b_col + m_prev
    row_max = jnp.concatenate(
        [jnp.broadcast_to(jnp.max(log_d[:, h * HEAD_DIM:(h + 1) * HEAD_DIM], axis=1, keepdims=True), (L, HEAD_DIM))
         for h in range(N_HEADS)], axis=1)
    m_t = jnp.maximum(m_inter, row_max)
    c0 = cbd_ref[...]
    n0 = nrow_ref[...]
    yield
    sc = _dot_nt(q_all, _block_diag(k_all, same_head), X1) * jnp.exp(log_d - m_t)
    w_inter = jnp.exp(m_inter - m_t)
    inter = w_inter * _dot(q_all, c0, X1)
    m_l = m_t[L - 1:L, :]
    b_l = b_col[L - 1:L, :]
    a_prev = jnp.exp(b_l + m_prev - m_l)
    ks = k_all * jnp.exp(ig_col + b_l - b_col - m_l)
    yield
    num = _dot(sc, _block_diag(v_all, same_head), X1) + inter
    c_new = c0 * a_prev + jnp.where(same_head, _dot_tn(ks, v_all, X1), 0.0)
    yield
    den = HEAD_DIM * (_head_mean_lanes(sc, same_head) + w_inter * _head_mean_lanes(q_all * n0, same_head))
    hh = num / jnp.maximum(jnp.abs(den), jnp.exp(-m_t))
    n_new = n0 * a_prev + jnp.sum(ks, axis=0, keepdims=True)
    yield
    cbd_ref[...] = c_new
    nrow_ref[...] = n_new
    mrow_ref[...] = m_l
    m_out = jnp.zeros(m_ref.shape, F32)
    m_first = jnp.logical_and(_iota(m_ref.shape, 0) == 0, _iota(m_ref.shape, 1) < N_HEADS)
    for h in range(N_HEADS):
        sl = slice(h * HEAD_DIM, (h + 1) * HEAD_DIM)
        c_ref[h] = c_new[sl, sl]
        n_ref[h:h + 1, :] = n_new[:, sl]
        m_out = jnp.where(jnp.logical_and(m_first, _iota(m_ref.shape, 1) == h), m_l[:, h * HEAD_DIM:h * HEAD_DIM + 1], m_out)
    m_ref[...] = m_out
    hn = _head_norm_lanes(hh, same_head, HEAD_NORM_EPS, True) * ng_ref[...]
    y_ref[...] = jax.nn.sigmoid(og) * hn


def _mlstm_prompt_call(p, pg, bsz, conv_w, conv_b, gate_b, norm_g):
    n = p.shape[0]
    t = n // bsz
    grp = _seq_group(bsz)
    outs = pl.pallas_call(
        _per_sequence(_mlstm_prompt_kernel, (True, True) + (False,) * 4 + (True,) * 9, carried=(11, 12, 13, 14)),
        grid=(bsz // grp, t // CHUNK),
        in_specs=[_chunk_specs(grp, 4 * MIX_W), _chunk_specs(grp, LANES), _full(conv_w.shape),
                  _full(conv_b.shape), _full(gate_b.shape), _full(norm_g.shape)],
        out_specs=[_chunk_specs(grp, MIX_W),
                   _per_seq_spec(grp, 8, 2 * MIX_W),
                   _state_spec(grp),
                   _per_seq_spec(grp, N_HEADS, HEAD_DIM),
                   _per_seq_spec(grp, 8, LANES)],
        out_shape=[jax.ShapeDtypeStruct((bsz, t, MIX_W), F32),
                   jax.ShapeDtypeStruct((bsz, 8, 2 * MIX_W), F32),
                   jax.ShapeDtypeStruct((bsz, N_HEADS, HEAD_DIM, HEAD_DIM), F32),
                   jax.ShapeDtypeStruct((bsz, N_HEADS, HEAD_DIM), F32),
                   jax.ShapeDtypeStruct((bsz, 8, LANES), F32)],
        scratch_shapes=[pltpu.VMEM((grp, CHUNK, 2 * MIX_W), F32), pltpu.VMEM((grp, MIX_W, MIX_W), F32),
                        pltpu.VMEM((grp, 1, MIX_W), F32), pltpu.VMEM((grp, 1, MIX_W), F32)],
        compiler_params=_params("parallel", "arbitrary"),
        name="mlstm_prompt",
    )(p.reshape(bsz, t, 4 * MIX_W), pg.reshape(bsz, t, LANES), conv_w, conv_b, gate_b, norm_g)
    return (outs[0].reshape(n, MIX_W),) + tuple(outs[1:])


def _head_rows(ref, base, h):
    return ref[pl.ds(pl.multiple_of(base + h * HEAD_DIM, HEAD_DIM), HEAD_DIM), :]


def _rwkv_sample_kernel(p_ref, sh_ref, s_ref, mu_ref, w0_ref, wup_ref, a0_ref, aup_ref, gup_ref, kk_ref, ka_ref,
                        rk_ref, lng_ref, lnb_ref, y_ref, so_ref, v_scr, y_scr):
    h = pl.program_id(0)
    W = MIX_W

    def xs_rows(lo, size, dyn):
        idx = pl.ds(pl.multiple_of(lo + h * HEAD_DIM, HEAD_DIM), size) if dyn else pl.ds(lo, size)
        pv = p_ref[idx, :]
        return pv + mu_ref[idx, :] * (sh_ref[idx, :] - pv)

    r = xs_rows(0, HEAD_DIM, True)
    k = xs_rows(W, HEAD_DIM, True)
    v = xs_rows(2 * W, HEAD_DIM, True)
    xw = xs_rows(3 * W, RWKV_W_RANK, False)
    xa = xs_rows(3 * W + RWKV_W_RANK, RWKV_A_RANK, False)
    xg = xs_rows(3 * W + RWKV_W_RANK + RWKV_A_RANK, RWKV_G_RANK, False)
    w = w0_ref[...] + _dot(wup_ref[...], jnp.tanh(xw), HI)
    wd = jnp.exp(-jnp.exp(-jax.nn.softplus(-w) - 0.5))
    a = jax.nn.sigmoid(a0_ref[...] + _dot(aup_ref[...], xa, HI))
    g = _dot(gup_ref[...], jax.nn.sigmoid(xg), HI)
    kk = k * kk_ref[...]
    kk = kk / jnp.maximum(jnp.sqrt(jnp.sum(kk * kk, axis=0, keepdims=True)), 1e-12)
    kq = k * (1.0 + (a - 1.0) * ka_ref[...])
    bb = kk * a
    v_scr[...] = v

    def body(i, carry):
        sv = s_ref[i]
        sa = -jnp.sum(sv * kk, axis=0, keepdims=True)
        snew = sv * wd + sa * bb + v_scr[pl.ds(i, 1), :] * kq
        so_ref[i] = snew
        y_scr[pl.ds(i, 1), :] = jnp.sum(snew * r, axis=0, keepdims=True)
        return carry

    lax.fori_loop(0, HEAD_DIM, body, 0)
    y = _head_norm_cols(y_scr[...], RWKV_GN_EPS, True) * lng_ref[...] + lnb_ref[...]
    y = y + jnp.sum(r * kq * rk_ref[...], axis=0, keepdims=True) * v
    y_ref[...] = y * g


def _head_block(cols):
    return pl.BlockSpec((HEAD_DIM, cols), lambda h: (h, 0))


def _state_block(bsz):
    return pl.BlockSpec((None, HEAD_DIM, HEAD_DIM, bsz), lambda h: (h, 0, 0, 0))


def _rwkv_sample_call(p_t, shift_t, s_t, prm):
    bsz = p_t.shape[1]
    return pl.pallas_call(
        _rwkv_sample_kernel,
        grid=(N_HEADS,),
        in_specs=[_full(p_t.shape), _full(shift_t.shape), _state_block(bsz), _full((RWKV_COLS, 1)),
                  _head_block(1), _head_block(RWKV_W_RANK), _head_block(1), _head_block(RWKV_A_RANK),
                  _head_block(RWKV_G_RANK), _head_block(1), _head_block(1), _head_block(1), _head_block(1),
                  _head_block(1)],
        out_specs=[_head_block(bsz), _state_block(bsz)],
        out_shape=[jax.ShapeDtypeStruct((MIX_W, bsz), F32), jax.ShapeDtypeStruct(s_t.shape, F32)],
        scratch_shapes=[pltpu.VMEM((HEAD_DIM, bsz), F32), pltpu.VMEM((HEAD_DIM, bsz), F32)],
        compiler_params=_params("parallel"),
        name="rwkv_sample",
    )(p_t, shift_t, s_t, prm["mu_c"], prm["w0_c"], prm["wup_t"], prm["a0_c"], prm["aup_t"], prm["gup_t"],
      prm["kk_c"], prm["ka_c"], prm["rk_c"], prm["lng_c"], prm["lnb_c"])


def _kv_state_step(s_ref, so_ref, q_scr, k_scr, f_scr, v, decay):
    def body(i, o):
        dec = f_scr[pl.ds(i, 1), :] if decay is None else decay
        snew = s_ref[i] * dec + k_scr[pl.ds(i, 1), :] * v
        so_ref[i] = snew
        return o + q_scr[pl.ds(i, 1), :] * snew

    return lax.fori_loop(0, HEAD_DIM, body, jnp.zeros_like(v))


def _hgrn_sample_kernel(layer, p_ref, s_ref, lg_ref, ng_ref, y_ref, so_ref, q_scr, k_scr, f_scr):
    h = pl.program_id(0)
    W = MIX_W
    lg = lg_ref[...]
    l0, l1 = lg[:, 0:1], lg[:, 1:2]
    m = jnp.maximum(l0, l1)
    e0, e1 = jnp.exp(l0 - m), jnp.exp(l1 - m)
    z = e0 + e1
    lb0, lb1 = e0 / z, e1 / z
    lb = (lb0 - lb0) if layer == 0 else ((lb0 + lb1) - lb0)
    log_f = _hgrn_log_f(_head_rows(p_ref, W, h), lb)
    f = jnp.exp(log_f)
    q_scr[...] = jax.nn.silu(_head_rows(p_ref, 0, h))
    k_scr[...] = 1.0 - f
    f_scr[...] = f
    v = _head_rows(p_ref, 2 * W, h)
    g = _head_rows(p_ref, 3 * W, h)
    o = _kv_state_step(s_ref, so_ref, q_scr, k_scr, f_scr, v, None)
    y_ref[...] = _head_norm_cols(o, HEAD_NORM_EPS, False) * ng_ref[...] * jax.nn.silu(g)


def _hgrn_sample_call(p_t, s_t, logits_c, norm_g_c, layer):
    bsz = p_t.shape[1]
    return pl.pallas_call(
        functools.partial(_hgrn_sample_kernel, layer),
        grid=(N_HEADS,),
        in_specs=[_full(p_t.shape), _state_block(bsz), _head_block(2), _head_block(1)],
        out_specs=[_head_block(bsz), _state_block(bsz)],
        out_shape=[jax.ShapeDtypeStruct((MIX_W, bsz), F32), jax.ShapeDtypeStruct(s_t.shape, F32)],
        scratch_shapes=[pltpu.VMEM((HEAD_DIM, bsz), F32)] * 3,
        compiler_params=_params("parallel"),
        name="hgrn_sample",
    )(p_t, s_t, logits_c, norm_g_c)


def _ret_sample_kernel(pos0, p_ref, s_ref, inv_ref, y_ref, so_ref, q_scr, k_scr):
    h = pl.program_id(0)
    W = MIX_W
    ang = jnp.float32(pos0) * inv_ref[...]
    cosv, sinv = jnp.cos(ang), jnp.sin(ang)

    def rot(z):
        z1, z2 = z[0:HALF_DIM, :], z[HALF_DIM:HEAD_DIM, :]
        return jnp.concatenate([z1 * cosv - z2 * sinv, z1 * sinv + z2 * cosv], axis=0)

    q_scr[...] = rot(_head_rows(p_ref, 0, h))
    k_scr[...] = rot(_head_rows(p_ref, W, h)) * HEAD_DIM ** -0.5
    v = _head_rows(p_ref, 2 * W, h)
    g = _head_rows(p_ref, 3 * W, h)
    gamma = jnp.float32(math.exp(RET_LOG_GAMMA[N_HEADS - 1]))
    for hh in range(N_HEADS - 2, -1, -1):
        gamma = jnp.where(h == hh, jnp.float32(math.exp(RET_LOG_GAMMA[hh])), gamma)
    o = _kv_state_step(s_ref, so_ref, q_scr, k_scr, None, v, gamma)
    y_ref[...] = _head_norm_cols(o, HEAD_NORM_EPS, False) * jax.nn.silu(g)


def _ret_sample_call(p_t, s_t, pos0):
    bsz = p_t.shape[1]
    inv = _rotary_inv().reshape(HALF_DIM, 1)
    return pl.pallas_call(
        functools.partial(_ret_sample_kernel, pos0),
        grid=(N_HEADS,),
        in_specs=[_full(p_t.shape), _state_block(bsz), _full(inv.shape)],
        out_specs=[_head_block(bsz), _state_block(bsz)],
        out_shape=[jax.ShapeDtypeStruct((MIX_W, bsz), F32), jax.ShapeDtypeStruct(s_t.shape, F32)],
        scratch_shapes=[pltpu.VMEM((HEAD_DIM, bsz), F32)] * 2,
        compiler_params=_params("parallel"),
        name="ret_sample",
    )(p_t, s_t, inv)


def _mlstm_sample_kernel(p_ref, pg_ref, conv_ref, c_ref, n_ref, m_ref, cw_ref, cb_ref, igb_ref, fgb_ref, ng_ref,
                         y_ref, co_ref, no_ref, mo_ref, q_scr, k_scr):
    h = pl.program_id(0)
    W = MIX_W

    def conv_rows(base):
        idx = pl.ds(pl.multiple_of(base + h * HEAD_DIM, HEAD_DIM), HEAD_DIM)
        cw = cw_ref[idx, :]
        acc = cw[:, MLSTM_CONV - 1:MLSTM_CONV] * p_ref[idx, :]
        for j in range(MLSTM_CONV - 1):
            acc = acc + cw[:, j:j + 1] * conv_ref[j, idx, :]
        return jax.nn.silu(acc + cb_ref[idx, :])

    q = conv_rows(0)
    k = conv_rows(W) * HEAD_DIM ** -0.5
    v = _head_rows(p_ref, 2 * W, h)
    og = _head_rows(p_ref, 3 * W, h)
    ig = pg_ref[pl.ds(h, 1), :] + igb_ref[pl.ds(h, 1), :]
    lf = jax.nn.log_sigmoid(pg_ref[pl.ds(N_HEADS + h, 1), :] + fgb_ref[pl.ds(h, 1), :])
    m_prev = m_ref[pl.ds(h, 1), :]
    n0 = n_ref[...]
    m_inter = lf + m_prev
    m_t = jnp.maximum(m_inter, ig)
    sc = jnp.sum(q * k, axis=0, keepdims=True) * jnp.exp(ig - m_t)
    w_inter = jnp.exp(m_inter - m_t)
    a_prev = jnp.exp(lf + m_prev - m_t)
    a_s = jnp.exp(ig + lf - lf - m_t)
    q_scr[...] = q
    k_scr[...] = k * a_s

    def body(i, qc):
        c0 = c_ref[i]
        co_ref[i] = c0 * a_prev + k_scr[pl.ds(i, 1), :] * v
        return qc + q_scr[pl.ds(i, 1), :] * c0

    qc = lax.fori_loop(0, HEAD_DIM, body, jnp.zeros_like(v))
    num = sc * v + w_inter * qc
    den = sc + w_inter * jnp.sum(q * n0, axis=0, keepdims=True)
    hh = num / jnp.maximum(jnp.abs(den), jnp.exp(-m_t))
    no_ref[...] = n0 * a_prev + k * a_s
    mo_ref[pl.ds(h, 1), :] = m_t
    y_ref[...] = jax.nn.sigmoid(og) * (_head_norm_cols(hh, HEAD_NORM_EPS, True) * ng_ref[...])


def _mlstm_sample_call(p_t, pg_t, conv_t, c_t, n_t, m_t, prm):
    bsz = p_t.shape[1]
    nblk = pl.BlockSpec((None, HEAD_DIM, bsz), lambda h: (h, 0, 0))
    return pl.pallas_call(
        _mlstm_sample_kernel,
        grid=(N_HEADS,),
        in_specs=[_full(p_t.shape), _full(pg_t.shape), _full(conv_t.shape), _state_block(bsz), nblk,
                  _full(m_t.shape), _full((2 * MIX_W, MLSTM_CONV)), _full((2 * MIX_W, 1)),
                  _full((N_HEADS, 1)), _full((N_HEADS, 1)), _head_block(1)],
        out_specs=[_head_block(bsz), _state_block(bsz), nblk, _full(m_t.shape)],
        out_shape=[jax.ShapeDtypeStruct((MIX_W, bsz), F32), jax.ShapeDtypeStruct(c_t.shape, F32),
                   jax.ShapeDtypeStruct(n_t.shape, F32), jax.ShapeDtypeStruct(m_t.shape, F32)],
        scratch_shapes=[pltpu.VMEM((HEAD_DIM, bsz), F32)] * 2,
        compiler_params=_params("arbitrary"),
        name="mlstm_sample",
    )(p_t, pg_t, conv_t, c_t, n_t, m_t, prm["cw_t"], prm["cb_c"], prm["igb_c"], prm["fgb_c"], prm["ng_c"])


def _merge_kernel(x_ref, xn_ref, ya_ref, yb_ref, yc_ref, yd_ref, wg_ref, wb_ref, wo_ref, g2_ref, x1_ref, xn2_ref):
    gates = jax.nn.sigmoid(_dot(xn_ref[...], wg_ref[...]))
    merged = None
    for m, y_ref in enumerate((ya_ref, yb_ref, yc_ref, yd_ref)):
        br = _dot(y_ref[...].astype(BF16), wb_ref[m])
        term = gates[:, m * D_MODEL:(m + 1) * D_MODEL] * br
        merged = term if merged is None else merged + term
    x1 = x_ref[...] + _dot(merged.astype(BF16), wo_ref[...])
    x1_ref[...] = x1
    xn2 = x1 * lax.rsqrt(jnp.mean(x1 * x1, axis=-1, keepdims=True) + NORM_EPS) * g2_ref[...]
    xn2_ref[...] = xn2.T.astype(BF16)


def _merge_call(x2d, xn, ys, w_gate, w_branch, w_out, g2, tm):
    n = x2d.shape[0]
    row = lambda i: (i, 0)
    return pl.pallas_call(
        _merge_kernel,
        grid=(n // tm,),
        in_specs=[pl.BlockSpec((tm, D_MODEL), row)] * 2 + [pl.BlockSpec((tm, MIX_W), row)] * 4
        + [_full(w_gate.shape), _full(w_branch.shape), _full(w_out.shape), _full((1, D_MODEL))],
        out_specs=[pl.BlockSpec((tm, D_MODEL), row), pl.BlockSpec((D_MODEL, tm), lambda i: (0, i))],
        out_shape=[jax.ShapeDtypeStruct((n, D_MODEL), F32), jax.ShapeDtypeStruct((D_MODEL, n), BF16)],
        compiler_params=_params("parallel"),
        name="merge_out",
    )(x2d, xn, *ys, w_gate, w_branch, w_out, g2.reshape(1, D_MODEL))


def _top_ranked(s, count, break_ties):
    idx = _iota(s.shape, 0)
    vals = []
    for r in range(count):
        mx = jnp.max(s, axis=0, keepdims=True)
        sel = s == mx
        if break_ties:
            sel = idx == jnp.min(jnp.where(sel, idx, s.shape[0]), axis=0, keepdims=True)
        s = jnp.where(sel, -(2.0 ** (RANK_EXP0 + r)), s)
        vals.append(mx)
    taken = s <= -(2.0 ** RANK_EXP0)
    exponent = (pltpu.bitcast(s, jnp.int32) >> 23) & 0xFF
    rank = jnp.where(taken, (exponent - (127 + RANK_EXP0)).astype(F32), float(count))
    ranked = jnp.sum(taken.astype(F32), axis=0, keepdims=True)
    return vals, rank, ranked


_LATTICE_WIDTHS = tuple(PEER_TOPK // (i + 1) for i in range(PEER_TOPK))


def _peer_head_stats(s1, s2, break_ties):
    v1, rank1, n1 = _top_ranked(s1, PEER_TOPK, break_ties)
    v2, rank2, n2 = _top_ranked(s2, PEER_TOPK, break_ties)
    rows = [v1[i] + v2[j] for i in range(PEER_TOPK) for j in range(_LATTICE_WIDTHS[i])]
    pad = (-len(rows)) % 8
    cand = jnp.concatenate(rows + [jnp.full((pad, s1.shape[1]), NEG_BIG, F32)], axis=0)
    _, crank, _ = _top_ranked(cand, PEER_TOPK, True)
    chosen = crank < float(PEER_TOPK)
    z = jnp.sum(jnp.where(chosen, jnp.exp(cand - rows[0]), 0.0), axis=0, keepdims=True)
    taken = chosen.astype(F32)
    lim = jnp.zeros_like(s1)
    off = 0
    for i in range(PEER_TOPK):
        count_i = jnp.sum(taken[off:off + _LATTICE_WIDTHS[i], :], axis=0, keepdims=True)
        lim = jnp.where(rank1 == float(i), count_i, lim)
        off += _LATTICE_WIDTHS[i]
    tied = jnp.maximum(n1, n2) - float(PEER_TOPK)
    stats = (lim, jnp.exp(s1 - v1[0]) / z, rank2.astype(BF16), jnp.exp(s2 - v2[0]).astype(BF16))
    return stats, tied


def _peer_score_kernel(xn_ref, wq_ref, keys_ref, lim_ref, cf_ref, r2_ref, e2_ref, s_scr):
    q_t = _dot(wq_ref[...], xn_ref[...])
    out_refs = (lim_ref, cf_ref, r2_ref, e2_ref)
    tied = []
    q_hi = q_t.astype(BF16)
    q_lo = (q_t - q_hi.astype(F32)).astype(BF16)

    def scores(h, c):
        rows = slice((2 * h + c) * PEER_QHALF, (2 * h + c + 1) * PEER_QHALF)
        return _dot(keys_ref[h, c], jnp.concatenate([q_hi[rows], q_hi[rows], q_lo[rows]], axis=0))

    for h in range(PEER_HEADS):
        s1 = scores(h, 0)
        s2 = scores(h, 1)
        s_scr[h, 0] = s1
        s_scr[h, 1] = s2
        stats, tied_h = _peer_head_stats(s1, s2, False)
        for ref, val in zip(out_refs, stats):
            ref[h] = val
        tied.append(jnp.max(tied_h))

    for h in range(PEER_HEADS):
        @pl.when(tied[h] > 0.0)
        def _(h=h):
            exact, _ = _peer_head_stats(s_scr[h, 0], s_scr[h, 1], True)
            for ref, val in zip(out_refs, exact):
                ref[h] = val


def _peer_score_call(xn2_t, wq_t, keys, tm):
    n = xn2_t.shape[1]
    blk = pl.BlockSpec((PEER_HEADS, PEER_N_KEYS, tm), lambda i: (0, 0, i))
    big = lambda dt: jax.ShapeDtypeStruct((PEER_HEADS, PEER_N_KEYS, n), dt)
    return pl.pallas_call(
        _peer_score_kernel,
        grid=(n // tm,),
        in_specs=[pl.BlockSpec((D_MODEL, tm), lambda i: (0, i)), _full(wq_t.shape), _full(keys.shape)],
        out_specs=[blk] * 4,
        out_shape=[big(F32), big(F32), big(BF16), big(BF16)],
        scratch_shapes=[pltpu.VMEM((PEER_HEADS, 2, PEER_N_KEYS, tm), F32)],
        compiler_params=_params("parallel"),
        name="peer_scores",
    )(xn2_t, wq_t, keys)


def _peer_expert_kernel(xn_ref, x1_ref, u_ref, vt_ref, lim_ref, cf_ref, r2_ref, e2_ref, *rest):
    gain_ref = rest[0] if len(rest) == 3 else None
    o_ref, acc_ref = rest[-2:]
    j = pl.program_id(1)

    @pl.when(j == 0)
    def _():
        acc_ref[...] = jnp.zeros_like(acc_ref)

    te = u_ref.shape[0]
    tm = xn_ref.shape[1]
    per_tile = te // PEER_N_KEYS
    pieces = []
    for al in range(per_tile):
        a = j * per_tile + al
        hid = _dot(u_ref[al * PEER_N_KEYS:(al + 1) * PEER_N_KEYS, :], xn_ref[...])
        act = hid * (1.0 + lax.erf(hid * INV_SQRT2))
        grouped = (PEER_N_KEYS // BF16_ROWS, BF16_ROWS, tm)
        gate = jnp.zeros(grouped, BF16)
        for h in range(PEER_HEADS):
            lim = jnp.broadcast_to(lim_ref[h, pl.ds(a, 1), :], (BF16_ROWS, tm)).astype(BF16)
            cf = jnp.broadcast_to(0.5 * cf_ref[h, pl.ds(a, 1), :], (BF16_ROWS, tm)).astype(BF16)
            r2 = r2_ref[h].reshape(grouped)
            e2 = e2_ref[h].reshape(grouped)
            gate = gate + jnp.where(r2 < lim[None], e2, jnp.zeros((), BF16)) * cf[None]
        pieces.append(act.astype(BF16) * gate.reshape(PEER_N_KEYS, tm))
    acc_ref[...] += _dot(vt_ref[...], jnp.concatenate(pieces, axis=0))

    @pl.when(j == pl.num_programs(1) - 1)
    def _():
        x = x1_ref[...] + acc_ref[...].T
        if gain_ref is not None:
            x = x * lax.rsqrt(jnp.mean(x * x, axis=-1, keepdims=True) + NORM_EPS) * gain_ref[...]
        o_ref[...] = x


def _peer_expert_call(xn2_t, x1, u, v_t, stats, tm, final_gain=None):
    n = xn2_t.shape[1]
    n_tiles, _, te = v_t.shape
    tok = pl.BlockSpec((tm, D_MODEL), lambda i, j: (i, 0))
    blk = pl.BlockSpec((PEER_HEADS, PEER_N_KEYS, tm), lambda i, j: (0, 0, i))
    gain = [] if final_gain is None else [final_gain.reshape(1, D_MODEL)]
    return pl.pallas_call(
        _peer_expert_kernel,
        grid=(n // tm, n_tiles),
        in_specs=[pl.BlockSpec((D_MODEL, tm), lambda i, j: (0, i)), tok,
                  pl.BlockSpec((te, D_MODEL), lambda i, j: (j, 0)),
                  pl.BlockSpec((None, D_MODEL, te), lambda i, j: (j, 0, 0)),
                  blk, blk, blk, blk] + [_full((1, D_MODEL))] * len(gain),
        out_specs=tok,
        out_shape=jax.ShapeDtypeStruct((n, D_MODEL), F32),
        scratch_shapes=[pltpu.VMEM((D_MODEL, tm), F32)],
        compiler_params=_params("parallel", "arbitrary"),
        name="peer_experts",
    )(xn2_t, x1, u, v_t, *stats, *gain)


def _hi_lo_hi(w):
    hi = w.astype(BF16)
    lo = (w - hi.astype(F32)).astype(BF16)
    return jnp.concatenate([hi, lo, hi], axis=-1)


def _layer_weights(l, w_in, rw, hg, ml, w_branch, w_out, ln1_g, ln2_g, peer_w_q, peer_keys, peer_u, peer_v):
    o3 = RWKV_COLS + 2 * 4 * MIX_W
    o4 = o3 + MLSTM_COLS
    gcol = o3 + 3 * MIX_W
    w = w_in[l]
    w_main = jnp.concatenate([w[:, 0:gcol], w[:, gcol + 2 * N_HEADS:o4]], axis=1).astype(BF16)
    w_mlg = jnp.pad(w[:, gcol:gcol + 2 * N_HEADS], ((0, 0), (0, LANES - 2 * N_HEADS)))
    w_mlg = _hi_lo_hi(w_mlg.T).T
    row = lambda a: a[l].reshape(1, -1)
    col = lambda a: a[l].reshape(-1, 1)
    rwkv = dict(mu=row(rw["mu"]), w0=row(rw["w0"]), wup=rw["w_up"][l], a0=row(rw["a0"]), aup=rw["a_up"][l],
                gup=rw["g_up"][l], kk=row(rw["k_k"]), ka=row(rw["k_a"]), rk=row(rw["r_k"]),
                lng=row(rw["ln_g"]), lnb=row(rw["ln_b"]),
                mu_c=col(rw["mu"]), w0_c=col(rw["w0"]), wup_t=rw["w_up"][l].T, a0_c=col(rw["a0"]),
                aup_t=rw["a_up"][l].T, gup_t=rw["g_up"][l].T, kk_c=col(rw["k_k"]), ka_c=col(rw["k_a"]),
                rk_c=col(rw["r_k"]), lng_c=col(rw["ln_g"]), lnb_c=col(rw["ln_b"]))
    gate_b = jnp.pad(jnp.concatenate([ml["ig_b"][l], ml["fg_b"][l]]), (0, LANES - 2 * N_HEADS)).reshape(1, LANES)
    mlstm = dict(cw=ml["conv_w"][l], cb=row(ml["conv_b"]), gate_b=gate_b, ng=row(ml["norm_g"]),
                 cw_t=ml["conv_w"][l].T, cb_c=col(ml["conv_b"]), igb_c=col(ml["ig_b"]), fgb_c=col(ml["fg_b"]),
                 ng_c=col(ml["norm_g"]))
    return dict(
        ln1=ln1_g[l], ln2=ln2_g[l], w_main=w_main, w_mlg=w_mlg, w_gate=w[:, o4:].astype(BF16),
        w_branch=w_branch[l].astype(BF16), w_out=w_out[l].astype(BF16),
        rwkv=rwkv, hgrn_logits=hg["lb_logits"], hgrn_logits_c=hg["lb_logits"].T,
        hgrn_ng=row(hg["norm_g"]), hgrn_ng_c=col(hg["norm_g"]), mlstm=mlstm,
        wq_t=peer_w_q[l].T.astype(BF16), keys=_hi_lo_hi(peer_keys[l]), u=peer_u[l].astype(BF16),
        v_t=jnp.swapaxes(peer_v[l].astype(BF16).reshape(-1, PEER_EXPERT_TILE, D_MODEL), 1, 2))


def _prompt_mixers(proj, bsz, lw, layer):
    _, pr, ph, pt, pm, pg = proj
    ya, shift, s_a = _rwkv_prompt_call(pr, bsz, lw["rwkv"])
    yb, s_b = _hgrn_prompt_call(ph, bsz, lw["hgrn_logits"], lw["hgrn_ng"], layer)
    yc, s_c = _ret_prompt_call(pt, bsz, 0)
    m = lw["mlstm"]
    yd, conv, c_d, n_d, m_d = _mlstm_prompt_call(pm, pg, bsz, m["cw"], m["cb"], m["gate_b"], m["ng"])
    states = (shift[:, 0, :], s_a, s_b, s_c, conv[:, 8 - (MLSTM_CONV - 1):, :], c_d, n_d, m_d[:, 0, :N_HEADS])
    return (ya, yb, yc, yd), states


def _sample_mixers(proj, st, lw, layer, pos0):
    _, pr, ph, pt, pm, pg = proj
    shift, s_a, s_b, s_c, conv, c_d, n_d, m_d = st
    to_t = lambda a: jnp.moveaxis(a, 0, -1)
    from_t = lambda a: jnp.moveaxis(a, -1, 0)
    ya, s_a2 = _rwkv_sample_call(pr.T, shift.T, to_t(s_a), lw["rwkv"])
    yb, s_b2 = _hgrn_sample_call(ph.T, to_t(s_b), lw["hgrn_logits_c"], lw["hgrn_ng_c"], layer)
    yc, s_c2 = _ret_sample_call(pt.T, to_t(s_c), pos0)
    yd, c_d2, n_d2, m_d2 = _mlstm_sample_call(pm.T, pg.T, to_t(conv), to_t(c_d), to_t(n_d), m_d.T, lw["mlstm"])
    conv2 = jnp.concatenate([conv[:, 1:, :], pm[:, None, 0:2 * MIX_W]], axis=1)
    states = (pr, from_t(s_a2), from_t(s_b2), from_t(s_c2), conv2, from_t(c_d2), from_t(n_d2), m_d2.T)
    return (ya.T, yb.T, yc.T, yd.T), states


def _token_tiles(n):
    pick = lambda want: want if n % want == 0 else n
    return pick(256), pick(256), pick(512)


def _run_trunk(x, states, pos0, weights, lnf_g, prompt):
    bsz, t, _ = x.shape
    n = bsz * t
    x2d = x.reshape(n, D_MODEL)
    tm, tm_score, tm_peer = _token_tiles(n)
    new_states = []
    for layer, lw in enumerate(weights):
        proj = _proj_call(x2d, lw["ln1"], lw["w_main"], lw["w_mlg"], tm)
        if prompt:
            ys, st_new = _prompt_mixers(proj, bsz, lw, layer)
        else:
            ys, st_new = _sample_mixers(proj, tuple(s[layer] for s in states), lw, layer, pos0)
        new_states.append(st_new)
        x1, xn2 = _merge_call(x2d, proj[0], ys, lw["w_gate"], lw["w_branch"], lw["w_out"], lw["ln2"], tm)
        stats = _peer_score_call(xn2, lw["wq_t"], lw["keys"], tm_score)
        last = layer == len(weights) - 1
        x2d = _peer_expert_call(xn2, x1, lw["u"], lw["v_t"], stats, tm_peer, lnf_g if last else None)
    y = x2d.reshape(bsz, t, D_MODEL)
    stacked = tuple(jnp.stack([s[i] for s in new_states], axis=0) for i in range(8))
    return y, stacked


def kernel(x_prompt, x_sample, state_rwkv_shift, state_rwkv_S, state_hgrn_S, state_ret_S, state_mlstm_conv, state_mlstm_C, state_mlstm_n, state_mlstm_m, ln1_g, w_in, rwkv_mu, rwkv_w0, rwkv_w_up, rwkv_a0, rwkv_a_up, rwkv_g_up, rwkv_k_k, rwkv_k_a, rwkv_r_k, rwkv_ln_g, rwkv_ln_b, hgrn_lb_logits, hgrn_norm_g, mlstm_conv_w, mlstm_conv_b, mlstm_ig_b, mlstm_fg_b, mlstm_norm_g, w_branch, w_out, ln2_g, peer_w_q, peer_keys, peer_u, peer_v, lnf_g):
    depth = w_in.shape[0]
    rw = dict(mu=rwkv_mu, w0=rwkv_w0, w_up=rwkv_w_up, a0=rwkv_a0, a_up=rwkv_a_up, g_up=rwkv_g_up,
              k_k=rwkv_k_k, k_a=rwkv_k_a, r_k=rwkv_r_k, ln_g=rwkv_ln_g, ln_b=rwkv_ln_b)
    hg = dict(lb_logits=hgrn_lb_logits, norm_g=hgrn_norm_g)
    ml = dict(conv_w=mlstm_conv_w, conv_b=mlstm_conv_b, ig_b=mlstm_ig_b, fg_b=mlstm_fg_b, norm_g=mlstm_norm_g)
    weights = [_layer_weights(l, w_in, rw, hg, ml, w_branch, w_out, ln1_g, ln2_g, peer_w_q, peer_keys,
                              peer_u, peer_v) for l in range(depth)]
    y_prompt, p_states = _run_trunk(x_prompt, None, 0, weights, lnf_g, True)
    sample_init = (state_rwkv_shift, state_rwkv_S, state_hgrn_S, state_ret_S, state_mlstm_conv,
                   state_mlstm_C, state_mlstm_n, state_mlstm_m)
    y_sample, s_states = _run_trunk(x_sample, sample_init, PAST_LEN, weights, lnf_g, False)
    return (y_prompt, y_sample) + p_states + s_states
```
